```python
import math
import jax, jax.numpy as jnp
from jax import lax
import numpy as np

D_MODEL = 1024
BATCH = 4
SEQ = 4096
DEPTH = 1

CTX_LEN = 256
GRID_W = 64
DIFF_HEADS = 4
DIFF_HEAD_DIM = 64
DIFF_V_DIM = 2 * DIFF_HEAD_DIM
RET_HEADS = 4
RET_QK_DIM = 64
RET_V_DIM = 2 * RET_QK_DIM
DIFF_WIDTH = DIFF_HEADS * DIFF_V_DIM
RET_WIDTH = RET_HEADS * RET_V_DIM
MIX_WIDTH = DIFF_WIDTH + RET_WIDTH
DIFF_QK_COLS = DIFF_HEADS * 2 * DIFF_HEAD_DIM
RET_QK_COLS = RET_HEADS * RET_QK_DIM
SPLIT_POINTS = (
    DIFF_QK_COLS,
    2 * DIFF_QK_COLS,
    2 * DIFF_QK_COLS + DIFF_WIDTH,
    2 * DIFF_QK_COLS + DIFF_WIDTH + RET_QK_COLS,
    2 * DIFF_QK_COLS + DIFF_WIDTH + 2 * RET_QK_COLS,
    2 * DIFF_QK_COLS + DIFF_WIDTH + 2 * RET_QK_COLS + RET_WIDTH,
)
IN_WIDTH = SPLIT_POINTS[-1] + RET_WIDTH
D_FF = 4 * D_MODEL
Q_BLOCK = 128
RET_CHUNK = 128
ROPE_HEAD_DIM = 64
ROPE_PAIRS = ROPE_HEAD_DIM // 4
ROPE_BASE = 10000.0
N_MOD = 6
EPS = 1e-6

kernel_name = "hybrid_diffattn_retention_dit_block"


def rmsnorm(x, g):
    xf = x.astype(jnp.float32)
    y = xf * lax.rsqrt(jnp.mean(xf * xf, axis=-1, keepdims=True) + EPS)
    return (y * g.astype(jnp.float32)).astype(x.dtype)


def modulate(h, shift, scale):
    return h * (1.0 + scale) + shift


def axial_rope_tables(n_tokens):
    rows = n_tokens // GRID_W
    row = jnp.repeat(jnp.arange(rows), GRID_W).astype(jnp.float32)
    col = jnp.tile(jnp.arange(GRID_W), rows).astype(jnp.float32)
    inv = ROPE_BASE ** (-jnp.arange(ROPE_PAIRS, dtype=jnp.float32) / ROPE_PAIRS)
    ang = jnp.stack([row[:, None] * inv, col[:, None] * inv], axis=1)
    return jnp.cos(ang), jnp.sin(ang)


def apply_rope(x, cos, sin):
    B, n, G, d = x.shape
    xf = x.astype(jnp.float32).reshape(B, n, G, 2, 2, ROPE_PAIRS)
    a, b = xf[..., 0, :], xf[..., 1, :]
    cs = cos[None, :, None]
    sn = sin[None, :, None]
    out = jnp.stack([a * cs - b * sn, a * sn + b * cs], axis=-2)
    return out.reshape(B, n, G, d).astype(x.dtype)


def diff_attention(q, k, v, lam):
    B, nq, H, _, d = q.shape
    nb = nq // Q_BLOCK
    qb = jnp.moveaxis(q.reshape(B, nb, Q_BLOCK, H, 2, d), 1, 0)
    scale = DIFF_HEAD_DIM ** -0.5

    def block(qi):
        s = jnp.einsum('bqhcd,bkhcd->bchqk', qi, k).astype(jnp.float32) * scale
        p = jax.nn.softmax(s, axis=-1)
        a = p[:, 0] - lam * p[:, 1]
        return jnp.einsum('bhqk,bkhe->bqhe', a.astype(v.dtype), v)

    o = lax.map(block, qb)
    return jnp.moveaxis(o, 0, 1).reshape(B, nq, H, v.shape[-1])


def retention_chunkwise(q, k, v, log_gamma, state0):
    B, n, H, dk = q.shape
    dv = v.shape[-1]
    nc = n // RET_CHUNK

    def chunks(t):
        return jnp.moveaxis(t.astype(jnp.float32).reshape(B, nc, RET_CHUNK, H, t.shape[-1]), 1, 0)

    pos = jnp.arange(RET_CHUNK, dtype=jnp.float32)
    rel = pos[:, None] - pos[None, :]
    decay = jnp.where(rel[None] >= 0,
                      jnp.exp(jnp.maximum(rel, 0.0)[None] * log_gamma[:, None, None]), 0.0)
    q_dec = jnp.exp((pos[:, None] + 1.0) * log_gamma[None, :])
    k_dec = jnp.exp((RET_CHUNK - 1.0 - pos)[:, None] * log_gamma[None, :])
    c_dec = jnp.exp(RET_CHUNK * log_gamma)

    def step(state, qkv):
        qc, kc, vc = qkv
        scores = jnp.einsum('bihd,bjhd->bhij', qc, kc) * decay
        inner = jnp.einsum('bhij,bjhe->bihe', scores, vc)
        cross = jnp.einsum('bihd,bhde->bihe', qc, state) * q_dec[None, :, :, None]
        new_state = state * c_dec[None, :, None, None] + jnp.einsum(
            'bjhd,bjhe->bhde', kc * k_dec[None, :, :, None], vc)
        return new_state, inner + cross

    state, o = lax.scan(step, state0, (chunks(q), chunks(k), chunks(v)))
    return jnp.moveaxis(o, 0, 1).reshape(B, n, H, dv), state


def retention_output(o, gate, gn_g):
    B, n, H, dv = o.shape
    mu = jnp.mean(o, axis=-1, keepdims=True)
    var = jnp.mean(jnp.square(o - mu), axis=-1, keepdims=True)
    on = ((o - mu) * lax.rsqrt(var + EPS)).reshape(B, n, H * dv) * gn_g.astype(jnp.float32)
    return (on * jax.nn.silu(gate.astype(jnp.float32))).astype(gate.dtype)


def sq_relu_mlp(h, w1, w2):
    return jnp.square(jax.nn.relu(h @ w1)) @ w2


def hybrid_layer(layer_idx, x, xc, c, c_ctx, cos, sin, w_ada, b_ada, n1, n2, w_in,
                 lq1, lk1, lq2, lk2, sub_g, dec_f, dec_b, gn_g, w_out, w1, w2, update_ctx):
    B, n, _ = x.shape
    nctx = xc.shape[1]
    mod = (jax.nn.silu(c) @ w_ada + b_ada).reshape(B, N_MOD, D_MODEL)
    mod_c = (jax.nn.silu(c_ctx)[None] @ w_ada + b_ada).reshape(1, N_MOD, D_MODEL)
    sh1, sc1, g1, sh2, sc2, g2 = [mod[:, i, None, :] for i in range(N_MOD)]
    csh1, csc1, cg1, csh2, csc2, cg2 = [mod_c[:, i, None, :] for i in range(N_MOD)]

    h = modulate(rmsnorm(x, n1), sh1, sc1)
    hc = modulate(rmsnorm(xc, n1), csh1, csc1)
    q_d, k_d, v_d, q_r, k_r, v_r, g_r = jnp.split(h @ w_in, SPLIT_POINTS, axis=-1)
    qc_d, kc_d, vc_d, qc_r, kc_r, vc_r, gc_r = jnp.split(hc @ w_in, SPLIT_POINTS, axis=-1)

    lam_init = 0.8 - 0.6 * math.exp(-0.3 * layer_idx)
    lam = (jnp.exp(jnp.sum(lq1.astype(jnp.float32) * lk1.astype(jnp.float32)))
           - jnp.exp(jnp.sum(lq2.astype(jnp.float32) * lk2.astype(jnp.float32))) + lam_init)
    qd = apply_rope(q_d.reshape(B, n, 2 * DIFF_HEADS, DIFF_HEAD_DIM), cos, sin).reshape(
        B, n, DIFF_HEADS, 2, DIFF_HEAD_DIM)
    kd = apply_rope(k_d.reshape(B, n, 2 * DIFF_HEADS, DIFF_HEAD_DIM), cos, sin).reshape(
        B, n, DIFF_HEADS, 2, DIFF_HEAD_DIM)
    vd = v_d.reshape(B, n, DIFF_HEADS, DIFF_V_DIM)
    qdc = qc_d.reshape(B, nctx, DIFF_HEADS, 2, DIFF_HEAD_DIM)
    kdc = kc_d.reshape(B, nctx, DIFF_HEADS, 2, DIFF_HEAD_DIM)
    vdc = vc_d.reshape(B, nctx, DIFF_HEADS, DIFF_V_DIM)
    k_all = jnp.concatenate([kdc, kd], axis=1)
    v_all = jnp.concatenate([vdc, vd], axis=1)
    od = rmsnorm(diff_attention(qd, k_all, v_all, lam), sub_g) * (1.0 - lam_init)

    lg_f = -jax.nn.softplus(-dec_f.astype(jnp.float32))
    lg_b = -jax.nn.softplus(-dec_b.astype(jnp.float32))
    kscale = RET_QK_DIM ** -0.5
    qr = apply_rope(q_r.reshape(B, n, RET_HEADS, RET_QK_DIM), cos, sin)
    kr = apply_rope(k_r.reshape(B, n, RET_HEADS, RET_QK_DIM), cos, sin) * kscale
    vr = v_r.reshape(B, n, RET_HEADS, RET_V_DIM)
    qrc = qc_r.reshape(B, nctx, RET_HEADS, RET_QK_DIM)
    krc = kc_r.reshape(B, nctx, RET_HEADS, RET_QK_DIM) * kscale
    vrc = vc_r.reshape(B, nctx, RET_HEADS, RET_V_DIM)
    zero_state = jnp.zeros((B, RET_HEADS, RET_QK_DIM, RET_V_DIM), jnp.float32)
    flip = lambda t: jnp.flip(t, axis=1)
    oc_f, s_f = retention_chunkwise(qrc, krc, vrc, lg_f, zero_state)
    oc_b, s_b = retention_chunkwise(flip(qrc), flip(krc), flip(vrc), lg_b, zero_state)
    o_f, _ = retention_chunkwise(qr, kr, vr, lg_f, s_f)
    o_b, _ = retention_chunkwise(flip(qr), flip(kr), flip(vr), lg_b, s_b)
    orr = retention_output(o_f + flip(o_b), g_r, gn_g)

    y = jnp.concatenate([od.reshape(B, n, DIFF_WIDTH), orr], axis=-1) @ w_out
    x = x + g1 * y
    x = x + g2 * sq_relu_mlp(modulate(rmsnorm(x, n2), sh2, sc2), w1, w2)

    if update_ctx:
        odc = rmsnorm(diff_attention(qdc, kdc, vdc, lam), sub_g) * (1.0 - lam_init)
        orc = retention_output(oc_f + flip(oc_b), gc_r, gn_g)
        yc = jnp.concatenate([odc.reshape(B, nctx, DIFF_WIDTH), orc], axis=-1) @ w_out
        xc = xc + cg1 * yc
        xc = xc + cg2 * sq_relu_mlp(modulate(rmsnorm(xc, n2), csh2, csc2), w1, w2)
    return x, xc


def setup_inputs(seed: int = 0) -> dict:
    key = jax.random.key(seed)
    ks = jax.random.split(key, 20)
    f32 = jnp.float32
    nrm = lambda k, shape, s: jax.random.normal(k, shape, f32) * s
    base_logit = jnp.asarray(np.log(2.0 ** (5.0 + np.arange(RET_HEADS)) - 1.0), f32)
    return {
        "x": nrm(ks[0], (BATCH, SEQ, D_MODEL), 1.0),
        "c": nrm(ks[1], (BATCH, D_MODEL), 1.0),
        "ctx": nrm(ks[2], (BATCH, CTX_LEN, D_MODEL), 1.0),
        "c_ctx": nrm(ks[3], (D_MODEL,), 1.0),
        "w_ada": nrm(ks[4], (DEPTH, D_MODEL, N_MOD * D_MODEL), 0.5 * D_MODEL ** -0.5),
        "b_ada": nrm(ks[5], (DEPTH, N_MOD * D_MODEL), 0.02),
        "norm1_g": 1.0 + nrm(ks[6], (DEPTH, D_MODEL), 0.02),
        "norm2_g": 1.0 + nrm(ks[7], (DEPTH, D_MODEL), 0.02),
        "w_in": nrm(ks[8], (DEPTH, D_MODEL, IN_WIDTH), D_MODEL ** -0.5),
        "lambda_q1": nrm(ks[9], (DEPTH, DIFF_HEAD_DIM), 0.1),
        "lambda_k1": nrm(ks[10], (DEPTH, DIFF_HEAD_DIM), 0.1),
        "lambda_q2": nrm(ks[11], (DEPTH, DIFF_HEAD_DIM), 0.1),
        "lambda_k2": nrm(ks[12], (DEPTH, DIFF_HEAD_DIM), 0.1),
        "diff_subln_g": 1.0 + nrm(ks[13], (DEPTH, DIFF_V_DIM), 0.02),
        "ret_decay_fwd": base_logit[None] + nrm(ks[14], (DEPTH, RET_HEADS), 0.1),
        "ret_decay_bwd": base_logit[None] + nrm(ks[15], (DEPTH, RET_HEADS), 0.1),
        "ret_gn_g": 1.0 + nrm(ks[16], (DEPTH, RET_WIDTH), 0.02),
        "w_out": nrm(ks[17], (DEPTH, MIX_WIDTH, D_MODEL), MIX_WIDTH ** -0.5),
        "w_mlp1": nrm(ks[18], (DEPTH, D_MODEL, D_FF), D_MODEL ** -0.5),
        "w_mlp2": nrm(ks[19], (DEPTH, D_FF, D_MODEL), D_FF ** -0.5),
        "final_g": 1.0 + nrm(jax.random.fold_in(key, 99), (D_MODEL,), 0.02),
    }


def reference(x, c, ctx, c_ctx, w_ada, b_ada, norm1_g, norm2_g, w_in, lambda_q1, lambda_k1,
              lambda_q2, lambda_k2, diff_subln_g, ret_decay_fwd, ret_decay_bwd, ret_gn_g,
              w_out, w_mlp1, w_mlp2, final_g):
    cos, sin = axial_rope_tables(x.shape[1])
    xc = ctx
    for l in range(DEPTH):
        x, xc = hybrid_layer(l, x, xc, c, c_ctx, cos, sin, w_ada[l], b_ada[l], norm1_g[l], norm2_g[l],
                             w_in[l], lambda_q1[l], lambda_k1[l], lambda_q2[l], lambda_k2[l],
                             diff_subln_g[l], ret_decay_fwd[l], ret_decay_bwd[l], ret_gn_g[l],
                             w_out[l], w_mlp1[l], w_mlp2[l], update_ctx=(l < DEPTH - 1))
    return rmsnorm(x, final_g)
```

```python
import functools
import math

import jax
import jax.numpy as jnp
from jax import lax
from jax.experimental import pallas as pl
from jax.experimental.pallas import tpu as pltpu

F32 = jnp.float32
BF16 = jnp.bfloat16

CTX_LEN = 256
GRID_W = 64
HEADS = 4
HEAD_DIM = 64
V_DIM = 128
ROPE_PAIRS = 16
ROPE_BASE = 10000.0
N_MOD = 6
EPS = 1e-6
LAM_INIT = 0.8 - 0.6 * math.exp(-0.3 * 0)
LOG2E = 1.4426950408889634

LANES = 128
VMEM_LIMIT_BYTES = 56 * 1024 * 1024

PROJ_TM = 512
ATTN_QB = 256
ATTN_TC = 512
RET_C = 256
MLP_TM = 512
FF_CHUNK = 1024
ADA_TN = 1536


def _cparams(sem):
    return pltpu.CompilerParams(dimension_semantics=sem, vmem_limit_bytes=VMEM_LIMIT_BYTES)


def _rmsnorm(xf, g):
    return xf * lax.rsqrt(jnp.mean(xf * xf, axis=-1, keepdims=True) + EPS) * g


def _silu(x):
    return x * (1.0 / (1.0 + jnp.exp(-x)))


def _adaln_kernel(c_ref, w_ref, b_ref, o_ref):
    a = _silu(c_ref[...])
    o_ref[...] = jnp.dot(a, w_ref[...], preferred_element_type=F32,
                         precision=lax.Precision.HIGHEST) + b_ref[...]


def _adaln(c_rows, w_ada, b_ada):
    r, d = c_rows.shape
    n_out = w_ada.shape[1]
    return pl.pallas_call(
        _adaln_kernel,
        grid=(n_out // ADA_TN,),
        in_specs=[pl.BlockSpec((r, d), lambda j: (0, 0)),
                  pl.BlockSpec((d, ADA_TN), lambda j: (0, j)),
                  pl.BlockSpec((1, ADA_TN), lambda j: (0, j))],
        out_specs=pl.BlockSpec((r, ADA_TN), lambda j: (0, j)),
        out_shape=jax.ShapeDtypeStruct((r, n_out), F32),
        compiler_params=_cparams(("arbitrary",)),
        name="adaln",
    )(c_rows, w_ada, b_ada)


def _rope_tables(n_tokens):
    rows = n_tokens // GRID_W
    row = jnp.repeat(jnp.arange(rows), GRID_W).astype(F32)
    col = jnp.tile(jnp.arange(GRID_W), rows).astype(F32)
    inv = ROPE_BASE ** (-jnp.arange(ROPE_PAIRS, dtype=F32) / ROPE_PAIRS)
    ang_r = row[:, None] * inv
    ang_c = col[:, None] * inv
    zeros = jnp.zeros_like(ang_r)
    cos64 = jnp.concatenate([jnp.cos(ang_r)] * 2 + [jnp.cos(ang_c)] * 2, axis=1)
    sa64 = jnp.concatenate([-jnp.sin(ang_r), zeros, -jnp.sin(ang_c), zeros], axis=1)
    sb64 = jnp.concatenate([zeros, jnp.sin(ang_r), zeros, jnp.sin(ang_c)], axis=1)
    dup = lambda t: jnp.concatenate([t, t], axis=1)
    return dup(cos64), dup(sa64), dup(sb64)


def _proj_kernel(x_ref, sh_ref, sc_ref, g_ref, w_ref, *rest, rope, with_q):
    if rope:
        c_ref, sa_ref, sb_ref = rest[:3]
        outs = rest[3:]
    else:
        outs = rest
    h = _rmsnorm(x_ref[...], g_ref[...])
    h = h * (1.0 + sc_ref[...]) + sh_ref[...]
    hb = h.astype(BF16)

    def project(lo, hi):
        return jnp.dot(hb, w_ref[:, lo:hi], preferred_element_type=F32)

    def rotate(y):
        if not rope:
            return y
        c, sa, sb = c_ref[...], sa_ref[...], sb_ref[...]
        slabs = []
        for s in range(y.shape[1] // LANES):
            ys = y[:, s * LANES:(s + 1) * LANES]
            slabs.append(ys * c + pltpu.roll(ys, LANES - ROPE_PAIRS, 1) * sa
                         + pltpu.roll(ys, ROPE_PAIRS, 1) * sb)
        return jnp.concatenate(slabs, axis=1)

    scale = HEAD_DIM ** -0.5
    if with_q:
        qd_ref, kd_ref, vd_ref, qr_ref, kr_ref, vr_ref, gr_ref = outs
        qd_ref[...] = (rotate(project(0, 512)) * scale).astype(BF16)
        qr_ref[...] = rotate(project(1536, 1792)).astype(BF16)
        gr_ref[...] = project(2560, 3072).astype(BF16)
    else:
        kd_ref, vd_ref, kr_ref, vr_ref = outs
    kd_ref[...] = rotate(project(512, 1024)).astype(BF16)
    vd_ref[...] = project(1024, 1536).astype(BF16)
    kr_ref[...] = (rotate(project(1792, 2048)) * scale).astype(BF16)
    vr_ref[...] = project(2048, 2560).astype(BF16)


def _project(x, sh, sc, g, w_bf16, tables, tm, with_q):
    b, n, d = x.shape
    rope = tables is not None
    widths = (512, 512, 512, 256, 256, 512, 512) if with_q else (512, 512, 256, 512)
    tok = lambda w: pl.BlockSpec((None, tm, w), lambda t, bi: (bi, t, 0))
    row = pl.BlockSpec((None, 1, d), lambda t, bi: (bi, 0, 0))
    in_specs = [tok(d), row, row,
                pl.BlockSpec((1, d), lambda t, bi: (0, 0)),
                pl.BlockSpec(w_bf16.shape, lambda t, bi: (0, 0))]
    args = [x, sh, sc, g, w_bf16]
    if rope:
        in_specs += [pl.BlockSpec((tm, LANES), lambda t, bi: (t, 0))] * 3
        args += list(tables)
    return pl.pallas_call(
        functools.partial(_proj_kernel, rope=rope, with_q=with_q),
        grid=(n // tm, b),
        in_specs=in_specs,
        out_specs=[tok(w) for w in widths],
        out_shape=[jax.ShapeDtypeStruct((b, n, w), BF16) for w in widths],
        compiler_params=_cparams(("arbitrary", "arbitrary")),
        name="proj_latent" if with_q else "proj_ctx",
    )(*args)


def _attn_kernel(lq1_ref, lk1_ref, lq2_ref, lk2_ref, subg_ref, q_ref, k_ref, kc_ref,
                 vt_ref, vct_ref, o_ref):
    n = q_ref.shape[0]
    nctx = kc_ref.shape[0]
    lam = (jnp.exp(jnp.sum(lq1_ref[...] * lk1_ref[...], axis=-1, keepdims=True))
           - jnp.exp(jnp.sum(lq2_ref[...] * lk2_ref[...], axis=-1, keepdims=True)) + LAM_INIT)
    lane = lax.broadcasted_iota(jnp.int32, (1, LANES), 1)
    first_map = lane < HEAD_DIM
    subg = subg_ref[...]

    chunks = [(kc_ref, vct_ref, 0, nctx)]
    chunks += [(k_ref, vt_ref, c0, ATTN_TC) for c0 in range(0, n, ATTN_TC)]

    def one_map(qm):
        m = l = acc = None
        for kr, vr, c0, tc in chunks:
            kch = kr[c0:c0 + tc, :]
            vch = vr[:, c0:c0 + tc]
            s = lax.dot_general(kch, qm, (((1,), (1,)), ((), ())),
                                preferred_element_type=F32) * LOG2E
            mc = jnp.max(s, axis=0, keepdims=True)
            if m is None:
                m = mc
                p = jnp.exp2(s - m)
                l = jnp.sum(p, axis=0, keepdims=True)
                acc = jnp.dot(vch, p.astype(BF16), preferred_element_type=F32)
            else:
                m_new = jnp.maximum(m, mc)
                alpha = jnp.exp2(m - m_new)
                p = jnp.exp2(s - m_new)
                l = l * alpha + jnp.sum(p, axis=0, keepdims=True)
                acc = acc * alpha + jnp.dot(vch, p.astype(BF16), preferred_element_type=F32)
                m = m_new
        return acc / l

    def qblock(i, carry):
        r0 = pl.multiple_of(i * ATTN_QB, ATTN_QB)
        q = q_ref[pl.ds(r0, ATTN_QB), :]
        zero = jnp.zeros_like(q)
        o1 = one_map(jnp.where(first_map, q, zero))
        o2 = one_map(jnp.where(first_map, zero, q))
        o = (o1 - lam * o2).T
        o = _rmsnorm(o, subg) * (1.0 - LAM_INIT)
        o_ref[pl.ds(r0, ATTN_QB), :] = o.astype(o_ref.dtype)
        return carry

    lax.fori_loop(0, n // ATTN_QB, qblock, 0)


def _diff_attention(lq1, lk1, lq2, lk2, subg, qd, kd, kd_c, vt, vct):
    b, n, _ = qd.shape
    nctx = kd_c.shape[1]
    vec = lambda w: pl.BlockSpec((1, w), lambda bi, h: (0, 0))
    head = lambda rows: pl.BlockSpec((None, rows, LANES), lambda bi, h: (bi, 0, h))
    headt = lambda cols: pl.BlockSpec((None, None, V_DIM, cols), lambda bi, h: (bi, h, 0, 0))
    return pl.pallas_call(
        _attn_kernel,
        grid=(b, HEADS),
        in_specs=[vec(HEAD_DIM)] * 4 + [vec(V_DIM), head(n), head(n), head(nctx), headt(n), headt(nctx)],
        out_specs=head(n),
        out_shape=jax.ShapeDtypeStruct((b, n, HEADS * V_DIM), BF16),
        compiler_params=_cparams(("arbitrary", "arbitrary")),
        name="diff_attn",
    )(lq1, lk1, lq2, lk2, subg, qd, kd, kd_c, vt, vct)


def _ret_kernel(decf_ref, decb_ref, gn_ref, q_ref, k_ref, v_ref, g_ref, kc_ref, vc_ref,
                o_ref, sb_scr):
    n = q_ref.shape[0]
    c = RET_C
    nc = n // c
    pair = pl.program_id(1)

    def log_gamma(dec):
        z = -dec
        return -(jnp.maximum(z, 0.0) + jnp.log(1.0 + jnp.exp(-jnp.abs(z))))

    lgf_all = log_gamma(decf_ref[...])
    lgb_all = log_gamma(decb_ref[...])
    hsel = lax.broadcasted_iota(jnp.int32, (1, HEADS), 1)

    def pick(vec, hh):
        return jnp.sum(jnp.where(hsel == 2 * pair + hh, vec, 0.0), axis=-1, keepdims=True)

    ii = lax.broadcasted_iota(jnp.int32, (c, c), 0).astype(F32)
    jj = lax.broadcasted_iota(jnp.int32, (c, c), 1).astype(F32)
    rel = ii - jj
    pos = lax.broadcasted_iota(jnp.int32, (c, LANES), 0).astype(F32)
    lane = lax.broadcasted_iota(jnp.int32, (1, LANES), 1)

    heads = []
    for hh in range(2):
        lgf, lgb = pick(lgf_all, hh), pick(lgb_all, hh)
        decay = (jnp.where(rel >= 0, jnp.exp(jnp.maximum(rel, 0.0) * lgf), 0.0)
                 + jnp.where(rel <= 0, jnp.exp(jnp.maximum(-rel, 0.0) * lgb), 0.0))
        heads.append(dict(
            decay=decay,
            qdec_f=jnp.exp((pos + 1.0) * lgf),
            qdec_b=jnp.exp((c - pos) * lgb),
            kdec_f=jnp.exp((c - 1.0 - pos) * lgf),
            kdec_b=jnp.exp(pos * lgb),
            cdec_f=jnp.exp(c * lgf),
            cdec_b=jnp.exp(c * lgb),
            qmask=(lane < HEAD_DIM) if hh == 0 else (lane >= HEAD_DIM),
            vlo=hh * V_DIM,
        ))

    def kv_state(k, v, kdec):
        kd = (k.astype(F32) * kdec).astype(BF16)
        return lax.dot_general(kd, v, (((0,), (0,)), ((), ())), preferred_element_type=F32)

    kc = kc_ref[...]
    sf0, sb0 = [], []
    for hd in heads:
        vc = vc_ref[:, hd["vlo"]:hd["vlo"] + V_DIM]
        sf0.append(kv_state(kc, vc, hd["kdec_f"]))
        sb0.append(kv_state(kc, vc, hd["kdec_b"]))

    def bwd(t, sb):
        ci = nc - 1 - t
        r0 = pl.multiple_of(ci * c, c)
        k = k_ref[pl.ds(r0, c), :]
        new = []
        for hh, hd in enumerate(heads):
            sb_scr[ci, hh] = sb[hh].astype(BF16)
            v = v_ref[pl.ds(r0, c), hd["vlo"]:hd["vlo"] + V_DIM]
            new.append(sb[hh] * hd["cdec_b"] + kv_state(k, v, hd["kdec_b"]))
        return tuple(new)

    lax.fori_loop(0, nc, bwd, tuple(sb0))

    gn = gn_ref[...]

    def fwd(ci, sf):
        r0 = pl.multiple_of(ci * c, c)
        q = q_ref[pl.ds(r0, c), :]
        k = k_ref[pl.ds(r0, c), :]
        zero = jnp.zeros_like(q)
        new = []
        for hh, hd in enumerate(heads):
            lo = hd["vlo"]
            v = v_ref[pl.ds(r0, c), lo:lo + V_DIM]
            qm = jnp.where(hd["qmask"], q, zero)
            scores = lax.dot_general(qm, k, (((1,), (1,)), ((), ())), preferred_element_type=F32)
            a = (scores * hd["decay"]).astype(BF16)
            o = jnp.dot(a, v, preferred_element_type=F32)
            o = o + jnp.dot(qm, sf[hh].astype(BF16), preferred_element_type=F32) * hd["qdec_f"]
            o = o + jnp.dot(qm, sb_scr[ci, hh], preferred_element_type=F32) * hd["qdec_b"]
            mu = jnp.mean(o, axis=-1, keepdims=True)
            var = jnp.mean(jnp.square(o - mu), axis=-1, keepdims=True)
            on = (o - mu) * lax.rsqrt(var + EPS) * gn[:, lo:lo + V_DIM]
            gate = g_ref[pl.ds(r0, c), lo:lo + V_DIM].astype(F32)
            o_ref[pl.ds(r0, c), lo:lo + V_DIM] = (on * _silu(gate)).astype(o_ref.dtype)
            new.append(sf[hh] * hd["cdec_f"] + kv_state(k, v, hd["kdec_f"]))
        return tuple(new)

    lax.fori_loop(0, nc, fwd, tuple(sf0))


def _retention(dec_f, dec_b, gn_g, qr, kr, vr, gr, kr_c, vr_c):
    b, n, _ = qr.shape
    nctx = kr_c.shape[1]
    assert nctx == RET_C and n % RET_C == 0
    pairs = HEADS // 2
    small = pl.BlockSpec((1, HEADS), lambda bi, p: (0, 0))
    qk = lambda rows: pl.BlockSpec((None, rows, LANES), lambda bi, p: (bi, 0, p))
    wide = lambda rows: pl.BlockSpec((None, rows, 2 * V_DIM), lambda bi, p: (bi, 0, p))
    return pl.pallas_call(
        _ret_kernel,
        grid=(b, pairs),
        in_specs=[small, small, pl.BlockSpec((1, 2 * V_DIM), lambda bi, p: (0, p)),
                  qk(n), qk(n), wide(n), wide(n), qk(nctx), wide(nctx)],
        out_specs=wide(n),
        out_shape=jax.ShapeDtypeStruct((b, n, HEADS * V_DIM), BF16),
        scratch_shapes=[pltpu.VMEM((n // RET_C, 2, LANES, V_DIM), BF16)],
        compiler_params=_cparams(("arbitrary", "arbitrary")),
        name="retention",
    )(dec_f, dec_b, gn_g, qr, kr, vr, gr, kr_c, vr_c)


def _mlp_kernel(x_ref, od_ref, or_ref, g1_ref, sh2_ref, sc2_ref, g2_ref, n2_ref, fg_ref,
                wo_ref, w1_ref, w2_ref, o_ref):
    half = od_ref.shape[1]
    y = (jnp.dot(od_ref[...], wo_ref[0:half, :], preferred_element_type=F32)
         + jnp.dot(or_ref[...], wo_ref[half:2 * half, :], preferred_element_type=F32))
    x1 = x_ref[...] + g1_ref[...] * y
    h = _rmsnorm(x1, n2_ref[...]) * (1.0 + sc2_ref[...]) + sh2_ref[...]
    hb = h.astype(BF16)
    d_ff = w1_ref.shape[1]
    acc = None
    for f0 in range(0, d_ff, FF_CHUNK):
        u = jnp.dot(hb, w1_ref[:, f0:f0 + FF_CHUNK], preferred_element_type=F32)
        u = jnp.square(jnp.maximum(u, 0.0)).astype(BF16)
        part = jnp.dot(u, w2_ref[f0:f0 + FF_CHUNK, :], preferred_element_type=F32)
        acc = part if acc is None else acc + part
    x2 = x1 + g2_ref[...] * acc
    o_ref[...] = _rmsnorm(x2, fg_ref[...])


def _out_mlp(x, od, orr, g1, sh2, sc2, g2, n2, fg, wo, w1, w2):
    b, n, d = x.shape
    tm = MLP_TM
    tok = lambda w: pl.BlockSpec((None, tm, w), lambda bi, t: (bi, t, 0))
    row = pl.BlockSpec((None, 1, d), lambda bi, t: (bi, 0, 0))
    gain = pl.BlockSpec((1, d), lambda bi, t: (0, 0))
    resident = lambda a: pl.BlockSpec(a.shape, lambda bi, t: (0, 0), pipeline_mode=pl.Buffered(1))
    return pl.pallas_call(
        _mlp_kernel,
        grid=(b, n // tm),
        in_specs=[tok(d), tok(od.shape[2]), tok(orr.shape[2]), row, row, row, row, gain, gain,
                  resident(wo), resident(w1), resident(w2)],
        out_specs=tok(d),
        out_shape=jax.ShapeDtypeStruct((b, n, d), F32),
        compiler_params=_cparams(("arbitrary", "arbitrary")),
        name="out_mlp",
    )(x, od, orr, g1, sh2, sc2, g2, n2, fg, wo, w1, w2)


def kernel(x, c, ctx, c_ctx, w_ada, b_ada, norm1_g, norm2_g, w_in, lambda_q1, lambda_k1, lambda_q2,
           lambda_k2, diff_subln_g, ret_decay_fwd, ret_decay_bwd, ret_gn_g, w_out, w_mlp1, w_mlp2, final_g):
    assert w_ada.shape[0] == 1, "single-layer block"
    b, n, d = x.shape
    nctx = ctx.shape[1]

    c_rows = jnp.concatenate([c, c_ctx[None], jnp.zeros((8 - b - 1, d), F32)], axis=0)
    mod = _adaln(c_rows, w_ada[0], b_ada).reshape(8, N_MOD, d)
    sh1, sc1, g1, sh2, sc2, g2 = [mod[:b, i][:, None, :] for i in range(N_MOD)]
    csh1 = jnp.broadcast_to(mod[b, 0][None, None, :], (b, 1, d))
    csc1 = jnp.broadcast_to(mod[b, 1][None, None, :], (b, 1, d))

    w_in_b = w_in[0].astype(BF16)
    qd, kd, vd, qr, kr, vr, gr = _project(x, sh1, sc1, norm1_g, w_in_b, _rope_tables(n), PROJ_TM, True)
    kd_c, vd_c, kr_c, vr_c = _project(ctx, csh1, csc1, norm1_g, w_in_b, None, nctx, False)

    to_t = lambda v: v.reshape(b, v.shape[1], HEADS, V_DIM).transpose(0, 2, 3, 1)
    od = _diff_attention(lambda_q1, lambda_k1, lambda_q2, lambda_k2, diff_subln_g,
                         qd, kd, kd_c, to_t(vd), to_t(vd_c))
    orr = _retention(ret_decay_fwd, ret_decay_bwd, ret_gn_g, qr, kr, vr, gr, kr_c, vr_c)

    return _out_mlp(x, od, orr, g1, sh2, sc2, g2, norm2_g, final_g[None],
                    w_out[0].astype(BF16), w_mlp1[0].astype(BF16), w_mlp2[0].astype(BF16))
```

```python
import functools
import math

import jax
import jax.numpy as jnp
from jax import lax
from jax.experimental import pallas as pl
from jax.experimental.pallas import tpu as pltpu

F32 = jnp.float32
BF16 = jnp.bfloat16

CTX_LEN = 256
GRID_W = 64
HEADS = 4
HEAD_DIM = 64
V_DIM = 128
ROPE_PAIRS = 16
ROPE_BASE = 10000.0
N_MOD = 6
EPS = 1e-6
LAM_INIT = 0.8 - 0.6 * math.exp(-0.3 * 0)
LOG2E = 1.4426950408889634

LANES = 128
VMEM_LIMIT_BYTES = 56 * 1024 * 1024

PROJ_TM = 512
ATTN_QB = 256
ATTN_TC = 512
RET_C = 256
MLP_TM = 512
FF_CHUNK = 1024
ADA_TN = 1536


def _cparams(sem):
    return pltpu.CompilerParams(dimension_semantics=sem, vmem_limit_bytes=VMEM_LIMIT_BYTES)


def _rmsnorm(xf, g):
    return xf * lax.rsqrt(jnp.mean(xf * xf, axis=-1, keepdims=True) + EPS) * g


def _silu(x):
    return x * (1.0 / (1.0 + jnp.exp(-x)))


def _adaln_kernel(c_ref, w_ref, b_ref, o_ref):
    a = _silu(c_ref[...])
    o_ref[...] = jnp.dot(a, w_ref[...], preferred_element_type=F32,
                         precision=lax.Precision.HIGHEST) + b_ref[...]


def _adaln(c_rows, w_ada, b_ada):
    r, d = c_rows.shape
    n_out = w_ada.shape[1]
    return pl.pallas_call(
        _adaln_kernel,
        grid=(n_out // ADA_TN,),
        in_specs=[pl.BlockSpec((r, d), lambda j: (0, 0)),
                  pl.BlockSpec((d, ADA_TN), lambda j: (0, j)),
                  pl.BlockSpec((1, ADA_TN), lambda j: (0, j))],
        out_specs=pl.BlockSpec((r, ADA_TN), lambda j: (0, j)),
        out_shape=jax.ShapeDtypeStruct((r, n_out), F32),
        compiler_params=_cparams(("arbitrary",)),
        name="adaln",
    )(c_rows, w_ada, b_ada)


def _rope_tables(n_tokens):
    rows = n_tokens // GRID_W
    row = jnp.repeat(jnp.arange(rows), GRID_W).astype(F32)
    col = jnp.tile(jnp.arange(GRID_W), rows).astype(F32)
    inv = ROPE_BASE ** (-jnp.arange(ROPE_PAIRS, dtype=F32) / ROPE_PAIRS)
    ang_r = row[:, None] * inv
    ang_c = col[:, None] * inv
    zeros = jnp.zeros_like(ang_r)
    cos64 = jnp.concatenate([jnp.cos(ang_r)] * 2 + [jnp.cos(ang_c)] * 2, axis=1)
    sa64 = jnp.concatenate([-jnp.sin(ang_r), zeros, -jnp.sin(ang_c), zeros], axis=1)
    sb64 = jnp.concatenate([zeros, jnp.sin(ang_r), zeros, jnp.sin(ang_c)], axis=1)
    dup = lambda t: jnp.concatenate([t, t], axis=1)
    return dup(cos64), dup(sa64), dup(sb64)


def _proj_kernel(x_ref, sh_ref, sc_ref, g_ref, w_ref, *rest, rope, with_q):
    if rope:
        c_ref, sa_ref, sb_ref = rest[:3]
        outs = rest[3:]
    else:
        outs = rest
    h = _rmsnorm(x_ref[...], g_ref[...])
    h = h * (1.0 + sc_ref[...]) + sh_ref[...]
    hb = h.astype(BF16)

    def project(lo, hi):
        return jnp.dot(hb, w_ref[:, lo:hi], preferred_element_type=F32)

    def rotate(y):
        if not rope:
            return y
        c, sa, sb = c_ref[...], sa_ref[...], sb_ref[...]
        slabs = []
        for s in range(y.shape[1] // LANES):
            ys = y[:, s * LANES:(s + 1) * LANES]
            slabs.append(ys * c + pltpu.roll(ys, LANES - ROPE_PAIRS, 1) * sa
                         + pltpu.roll(ys, ROPE_PAIRS, 1) * sb)
        return jnp.concatenate(slabs, axis=1)

    scale = HEAD_DIM ** -0.5
    if with_q:
        qd_ref, kd_ref, vd_ref, qr_ref, kr_ref, vr_ref, gr_ref = outs
        qd_ref[...] = (rotate(project(0, 512)) * (scale * LOG2E)).astype(BF16)
        qr_ref[...] = rotate(project(1536, 1792)).astype(BF16)
        gr_ref[...] = project(2560, 3072).astype(BF16)
    else:
        kd_ref, vd_ref, kr_ref, vr_ref = outs
    kd_ref[...] = rotate(project(512, 1024)).astype(BF16)
    vd_ref[...] = project(1024, 1536).astype(BF16)
    kr_ref[...] = (rotate(project(1792, 2048)) * scale).astype(BF16)
    vr_ref[...] = project(2048, 2560).astype(BF16)


def _project(x, sh, sc, g, w_bf16, tables, tm, with_q):
    b, n, d = x.shape
    rope = tables is not None
    widths = (512, 512, 512, 256, 256, 512, 512) if with_q else (512, 512, 256, 512)
    tok = lambda w: pl.BlockSpec((None, tm, w), lambda t, bi: (bi, t, 0))
    row = pl.BlockSpec((None, 1, d), lambda t, bi: (bi, 0, 0))
    in_specs = [tok(d), row, row,
                pl.BlockSpec((1, d), lambda t, bi: (0, 0)),
                pl.BlockSpec(w_bf16.shape, lambda t, bi: (0, 0))]
    args = [x, sh, sc, g, w_bf16]
    if rope:
        in_specs += [pl.BlockSpec((tm, LANES), lambda t, bi: (t, 0))] * 3
        args += list(tables)
    return pl.pallas_call(
        functools.partial(_proj_kernel, rope=rope, with_q=with_q),
        grid=(n // tm, b),
        in_specs=in_specs,
        out_specs=[tok(w) for w in widths],
        out_shape=[jax.ShapeDtypeStruct((b, n, w), BF16) for w in widths],
        compiler_params=_cparams(("arbitrary", "arbitrary")),
        name="proj_latent" if with_q else "proj_ctx",
    )(*args)


def _attn_kernel(lq1_ref, lk1_ref, lq2_ref, lk2_ref, subg_ref, q_ref, k_ref, kc_ref,
                 vt_ref, vct_ref, o_ref, sa_scr, sb_scr):
    n = q_ref.shape[0]
    nctx = kc_ref.shape[0]
    lam = (jnp.exp(jnp.sum(lq1_ref[...] * lk1_ref[...], axis=-1, keepdims=True))
           - jnp.exp(jnp.sum(lq2_ref[...] * lk2_ref[...], axis=-1, keepdims=True)) + LAM_INIT)
    lane = lax.broadcasted_iota(jnp.int32, (1, LANES), 1)
    first_map = lane < HEAD_DIM
    subg = subg_ref[...]
    nt = (((1,), (1,)), ((), ()))

    nq = n // ATTN_QB

    def scores(i, s_scr):
        r0 = pl.multiple_of(i * ATTN_QB, ATTN_QB)
        q = q_ref[pl.ds(r0, ATTN_QB), :]
        zero = jnp.zeros_like(q)
        q12 = jnp.concatenate([jnp.where(first_map, q, zero), jnp.where(first_map, zero, q)], axis=0)
        s_c = lax.dot_general(kc_ref[...], q12, nt, preferred_element_type=F32)
        s_l = lax.dot_general(k_ref[...], q12, nt, preferred_element_type=F32)
        s_scr[0:nctx, :] = s_c
        s_scr[nctx:nctx + n, :] = s_l
        return jnp.maximum(jnp.max(s_c, axis=0, keepdims=True), jnp.max(s_l, axis=0, keepdims=True))

    def finish(i, s_scr, m):
        p_c = jnp.exp2(s_scr[0:nctx, :] - m)
        p_l = jnp.exp2(s_scr[nctx:nctx + n, :] - m)
        l = jnp.sum(p_c, axis=0, keepdims=True) + jnp.sum(p_l, axis=0, keepdims=True)
        acc = (jnp.dot(vct_ref[...], p_c.astype(BF16), preferred_element_type=F32)
               + jnp.dot(vt_ref[...], p_l.astype(BF16), preferred_element_type=F32))
        ot = acc / l
        o = (ot[:, :ATTN_QB] - lam * ot[:, ATTN_QB:]).T
        o = _rmsnorm(o, subg) * (1.0 - LAM_INIT)
        r0 = pl.multiple_of(i * ATTN_QB, ATTN_QB)
        o_ref[pl.ds(r0, ATTN_QB), :] = o.astype(o_ref.dtype)

    def step(j, m_a):
        m_b = scores(2 * j + 1, sb_scr)
        finish(2 * j, sa_scr, m_a)
        m_a = scores(2 * j + 2, sa_scr)
        finish(2 * j + 1, sb_scr, m_b)
        return m_a

    m_a = lax.fori_loop(0, nq // 2 - 1, step, scores(0, sa_scr))
    m_b = scores(nq - 1, sb_scr)
    finish(nq - 2, sa_scr, m_a)
    finish(nq - 1, sb_scr, m_b)


def _diff_attention(lq1, lk1, lq2, lk2, subg, qd, kd, kd_c, vt, vct):
    b, n, _ = qd.shape
    nctx = kd_c.shape[1]
    vec = lambda w: pl.BlockSpec((1, w), lambda bi, h: (0, 0))
    head = lambda rows: pl.BlockSpec((None, rows, LANES), lambda bi, h: (bi, 0, h))
    headt = lambda cols: pl.BlockSpec((None, None, V_DIM, cols), lambda bi, h: (bi, h, 0, 0))
    return pl.pallas_call(
        _attn_kernel,
        grid=(b, HEADS),
        in_specs=[vec(HEAD_DIM)] * 4 + [vec(V_DIM), head(n), head(n), head(nctx), headt(n), headt(nctx)],
        out_specs=head(n),
        out_shape=jax.ShapeDtypeStruct((b, n, HEADS * V_DIM), BF16),
        scratch_shapes=[pltpu.VMEM((nctx + n, 2 * ATTN_QB), F32)] * 2,
        compiler_params=_cparams(("arbitrary", "arbitrary")),
        name="diff_attn",
    )(lq1, lk1, lq2, lk2, subg, qd, kd, kd_c, vt, vct)


def _ret_kernel(decf_ref, decb_ref, gn_ref, q_ref, k_ref, v_ref, g_ref, kc_ref, vc_ref,
                o_ref, sb_scr):
    n = q_ref.shape[0]
    c = RET_C
    nc = n // c
    pair = pl.program_id(1)

    def log_gamma(dec):
        z = -dec
        return -(jnp.maximum(z, 0.0) + jnp.log(1.0 + jnp.exp(-jnp.abs(z))))

    lgf_all = log_gamma(decf_ref[...])
    lgb_all = log_gamma(decb_ref[...])
    hsel = lax.broadcasted_iota(jnp.int32, (1, HEADS), 1)

    def pick(vec, hh):
        return jnp.sum(jnp.where(hsel == 2 * pair + hh, vec, 0.0), axis=-1, keepdims=True)

    ii = lax.broadcasted_iota(jnp.int32, (c, c), 0).astype(F32)
    jj = lax.broadcasted_iota(jnp.int32, (c, c), 1).astype(F32)
    rel = ii - jj
    pos = lax.broadcasted_iota(jnp.int32, (c, LANES), 0).astype(F32)
    lane = lax.broadcasted_iota(jnp.int32, (1, LANES), 1)

    heads = []
    for hh in range(2):
        lgf, lgb = pick(lgf_all, hh), pick(lgb_all, hh)
        decay = (jnp.where(rel >= 0, jnp.exp(jnp.maximum(rel, 0.0) * lgf), 0.0)
                 + jnp.where(rel <= 0, jnp.exp(jnp.maximum(-rel, 0.0) * lgb), 0.0))
        heads.append(dict(
            decay=decay,
            qdec_f=jnp.exp((pos + 1.0) * lgf),
            qdec_b=jnp.exp((c - pos) * lgb),
            kdec_f=jnp.exp((c - 1.0 - pos) * lgf),
            kdec_b=jnp.exp(pos * lgb),
            cdec_f=jnp.exp(c * lgf),
            cdec_b=jnp.exp(c * lgb),
            qmask=(lane < HEAD_DIM) if hh == 0 else (lane >= HEAD_DIM),
            vlo=hh * V_DIM,
        ))

    def kv_state(k, v, kdec):
        kd = (k.astype(F32) * kdec).astype(BF16)
        return lax.dot_general(kd, v, (((0,), (0,)), ((), ())), preferred_element_type=F32)

    kc = kc_ref[...]
    sf0, sb0 = [], []
    for hd in heads:
        vc = vc_ref[:, hd["vlo"]:hd["vlo"] + V_DIM]
        sf0.append(kv_state(kc, vc, hd["kdec_f"]))
        sb0.append(kv_state(kc, vc, hd["kdec_b"]))

    def bwd(t, sb):
        ci = nc - 1 - t
        r0 = pl.multiple_of(ci * c, c)
        k = k_ref[pl.ds(r0, c), :]
        new = []
        for hh, hd in enumerate(heads):
            sb_scr[ci, hh] = sb[hh].astype(BF16)
            v = v_ref[pl.ds(r0, c), hd["vlo"]:hd["vlo"] + V_DIM]
            new.append(sb[hh] * hd["cdec_b"] + kv_state(k, v, hd["kdec_b"]))
        return tuple(new)

    lax.fori_loop(0, nc, bwd, tuple(sb0))

    gn = gn_ref[...]

    def fwd(ci, sf):
        r0 = pl.multiple_of(ci * c, c)
        q = q_ref[pl.ds(r0, c), :]
        k = k_ref[pl.ds(r0, c), :]
        zero = jnp.zeros_like(q)
        new = []
        for hh, hd in enumerate(heads):
            lo = hd["vlo"]
            v = v_ref[pl.ds(r0, c), lo:lo + V_DIM]
            qm = jnp.where(hd["qmask"], q, zero)
            scores = lax.dot_general(qm, k, (((1,), (1,)), ((), ())), preferred_element_type=F32)
            a = (scores * hd["decay"]).astype(BF16)
            o = jnp.dot(a, v, preferred_element_type=F32)
            o = o + jnp.dot(qm, sf[hh].astype(BF16), preferred_element_type=F32) * hd["qdec_f"]
            o = o + jnp.dot(qm, sb_scr[ci, hh], preferred_element_type=F32) * hd["qdec_b"]
            mu = jnp.mean(o, axis=-1, keepdims=True)
            var = jnp.mean(jnp.square(o - mu), axis=-1, keepdims=True)
            on = (o - mu) * lax.rsqrt(var + EPS) * gn[:, lo:lo + V_DIM]
            gate = g_ref[pl.ds(r0, c), lo:lo + V_DIM].astype(F32)
            o_ref[pl.ds(r0, c), lo:lo + V_DIM] = (on * _silu(gate)).astype(o_ref.dtype)
            new.append(sf[hh] * hd["cdec_f"] + kv_state(k, v, hd["kdec_f"]))
        return tuple(new)

    lax.fori_loop(0, nc, fwd, tuple(sf0))


def _retention(dec_f, dec_b, gn_g, qr, kr, vr, gr, kr_c, vr_c):
    b, n, _ = qr.shape
    nctx = kr_c.shape[1]
    assert nctx == RET_C and n % RET_C == 0
    pairs = HEADS // 2
    small = pl.BlockSpec((1, HEADS), lambda bi, p: (0, 0))
    qk = lambda rows: pl.BlockSpec((None, rows, LANES), lambda bi, p: (bi, 0, p))
    wide = lambda rows: pl.BlockSpec((None, rows, 2 * V_DIM), lambda bi, p: (bi, 0, p))
    return pl.pallas_call(
        _ret_kernel,
        grid=(b, pairs),
        in_specs=[small, small, pl.BlockSpec((1, 2 * V_DIM), lambda bi, p: (0, p)),
                  qk(n), qk(n), wide(n), wide(n), qk(nctx), wide(nctx)],
        out_specs=wide(n),
        out_shape=jax.ShapeDtypeStruct((b, n, HEADS * V_DIM), BF16),
        scratch_shapes=[pltpu.VMEM((n // RET_C, 2, LANES, V_DIM), BF16)],
        compiler_params=_cparams(("arbitrary", "arbitrary")),
        name="retention",
    )(dec_f, dec_b, gn_g, qr, kr, vr, gr, kr_c, vr_c)


def _mlp_kernel(x_ref, od_ref, or_ref, g1_ref, sh2_ref, sc2_ref, g2_ref, n2_ref, fg_ref,
                wo_ref, w1_ref, w2_ref, o_ref):
    half = od_ref.shape[1]
    y = (jnp.dot(od_ref[...], wo_ref[0:half, :], preferred_element_type=F32)
         + jnp.dot(or_ref[...], wo_ref[half:2 * half, :], preferred_element_type=F32))
    x1 = x_ref[...] + g1_ref[...] * y
    h = _rmsnorm(x1, n2_ref[...]) * (1.0 + sc2_ref[...]) + sh2_ref[...]
    hb = h.astype(BF16)
    d_ff = w1_ref.shape[1]
    acc = None
    for f0 in range(0, d_ff, FF_CHUNK):
        u = jnp.dot(hb, w1_ref[:, f0:f0 + FF_CHUNK], preferred_element_type=F32)
        u = jnp.square(jnp.maximum(u, 0.0)).astype(BF16)
        part = jnp.dot(u, w2_ref[f0:f0 + FF_CHUNK, :], preferred_element_type=F32)
        acc = part if acc is None else acc + part
    x2 = x1 + g2_ref[...] * acc
    o_ref[...] = _rmsnorm(x2, fg_ref[...])


def _out_mlp(x, od, orr, g1, sh2, sc2, g2, n2, fg, wo, w1, w2):
    b, n, d = x.shape
    tm = MLP_TM
    tok = lambda w: pl.BlockSpec((None, tm, w), lambda bi, t: (bi, t, 0))
    row = pl.BlockSpec((None, 1, d), lambda bi, t: (bi, 0, 0))
    gain = pl.BlockSpec((1, d), lambda bi, t: (0, 0))
    resident = lambda a: pl.BlockSpec(a.shape, lambda bi, t: (0, 0), pipeline_mode=pl.Buffered(1))
    return pl.pallas_call(
        _mlp_kernel,
        grid=(b, n // tm),
        in_specs=[tok(d), tok(od.shape[2]), tok(orr.shape[2]), row, row, row, row, gain, gain,
                  resident(wo), resident(w1), resident(w2)],
        out_specs=tok(d),
        out_shape=jax.ShapeDtypeStruct((b, n, d), F32),
        compiler_params=_cparams(("arbitrary", "arbitrary")),
        name="out_mlp",
    )(x, od, orr, g1, sh2, sc2, g2, n2, fg, wo, w1, w2)


def kernel(x, c, ctx, c_ctx, w_ada, b_ada, norm1_g, norm2_g, w_in, lambda_q1, lambda_k1, lambda_q2,
           lambda_k2, diff_subln_g, ret_decay_fwd, ret_decay_bwd, ret_gn_g, w_out, w_mlp1, w_mlp2, final_g):
    assert w_ada.shape[0] == 1, "single-layer block"
    b, n, d = x.shape
    nctx = ctx.shape[1]

    c_rows = jnp.concatenate([c, c_ctx[None], jnp.zeros((8 - b - 1, d), F32)], axis=0)
    mod = _adaln(c_rows, w_ada[0], b_ada).reshape(8, N_MOD, d)
    sh1, sc1, g1, sh2, sc2, g2 = [mod[:b, i][:, None, :] for i in range(N_MOD)]
    csh1 = jnp.broadcast_to(mod[b, 0][None, None, :], (b, 1, d))
    csc1 = jnp.broadcast_to(mod[b, 1][None, None, :], (b, 1, d))

    w_in_b = w_in[0].astype(BF16)
    qd, kd, vd, qr, kr, vr, gr = _project(x, sh1, sc1, norm1_g, w_in_b, _rope_tables(n), PROJ_TM, True)
    kd_c, vd_c, kr_c, vr_c = _project(ctx, csh1, csc1, norm1_g, w_in_b, None, nctx, False)

    to_t = lambda v: v.reshape(b, v.shape[1], HEADS, V_DIM).transpose(0, 2, 3, 1)
    od = _diff_attention(lambda_q1, lambda_k1, lambda_q2, lambda_k2, diff_subln_g,
                         qd, kd, kd_c, to_t(vd), to_t(vd_c))
    orr = _retention(ret_decay_fwd, ret_decay_bwd, ret_gn_g, qr, kr, vr, gr, kr_c, vr_c)

    return _out_mlp(x, od, orr, g1, sh2, sc2, g2, norm2_g, final_g[None],
                    w_out[0].astype(BF16), w_mlp1[0].astype(BF16), w_mlp2[0].astype(BF16))
```

```python
import functools
import math

import numpy as np
import jax
import jax.numpy as jnp
from jax import lax
from jax.experimental import pallas as pl
from jax.experimental.pallas import tpu as pltpu

F32 = jnp.float32
BF16 = jnp.bfloat16

GRID_W = 64
HEADS = 4
HEAD_DIM = 64
V_DIM = 128
ROPE_PAIRS = 16
ROPE_BASE = 10000.0
N_MOD = 6
EPS = 1e-6
LAM_INIT = 0.8 - 0.6 * math.exp(-0.3 * 0)
LOG2E = 1.4426950408889634
QK_SCALE = HEAD_DIM ** -0.5

LANES = 128
VMEM_LIMIT_BYTES = 56 * 1024 * 1024

PROJ_TM = 512
ATTN_QB = 256
RET_C = 256
MLP_TM = 512
FF_CHUNK = 1024
ADA_TN = 1536

COL_QD, COL_KD, COL_VD = (0, 512), (512, 1024), (1024, 1536)
COL_QR, COL_KR, COL_VR, COL_GR = (1536, 1792), (1792, 2048), (2048, 2560), (2560, 3072)


def _cparams(sem):
    return pltpu.CompilerParams(dimension_semantics=sem, vmem_limit_bytes=VMEM_LIMIT_BYTES)


def _rmsnorm(xf, g):
    return xf * lax.rsqrt(jnp.mean(xf * xf, axis=-1, keepdims=True) + EPS) * g


def _silu(x):
    return x * (1.0 / (1.0 + jnp.exp(-x)))


def _adaln_kernel(c_ref, w_ref, b_ref, o_ref):
    a = _silu(c_ref[...])
    o_ref[...] = jnp.dot(a, w_ref[...], preferred_element_type=F32,
                         precision=lax.Precision.HIGHEST) + b_ref[...]


def _adaln(c_rows, w_ada, b_ada):
    r, d = c_rows.shape
    n_out = w_ada.shape[1]
    return pl.pallas_call(
        _adaln_kernel,
        grid=(n_out // ADA_TN,),
        in_specs=[pl.BlockSpec((r, d), lambda j: (0, 0)),
                  pl.BlockSpec((d, ADA_TN), lambda j: (0, j)),
                  pl.BlockSpec((1, ADA_TN), lambda j: (0, j))],
        out_specs=pl.BlockSpec((r, ADA_TN), lambda j: (0, j)),
        out_shape=jax.ShapeDtypeStruct((r, n_out), F32),
        compiler_params=_cparams(("arbitrary",)),
        name="adaln",
    )(c_rows, w_ada, b_ada)


def _rope_tables(n_tokens):
    rows = n_tokens // GRID_W
    row = np.repeat(np.arange(rows), GRID_W).astype(np.float64)
    col = np.tile(np.arange(GRID_W), rows).astype(np.float64)
    inv = (np.float32(ROPE_BASE) ** (-np.arange(ROPE_PAIRS, dtype=np.float32) / ROPE_PAIRS)).astype(np.float64)
    ang_r = row[:, None] * inv
    ang_c = col[:, None] * inv
    zeros = np.zeros_like(ang_r)
    cos64 = np.concatenate([np.cos(ang_r)] * 2 + [np.cos(ang_c)] * 2, axis=1)
    sa64 = np.concatenate([-np.sin(ang_r), zeros, -np.sin(ang_c), zeros], axis=1)
    sb64 = np.concatenate([zeros, np.sin(ang_r), zeros, np.sin(ang_c)], axis=1)
    dup = lambda t: jnp.asarray(np.concatenate([t, t], axis=1), dtype=F32)
    return dup(cos64), dup(sa64), dup(sb64)


def _modulated(x_ref, sh_ref, sc_ref, g_ref):
    h = _rmsnorm(x_ref[...], g_ref[...])
    return (h * (1.0 + sc_ref[...]) + sh_ref[...]).astype(BF16)


def _proj_latent_kernel(x_ref, sh_ref, sc_ref, g_ref, w_ref, c_ref, sa_ref, sb_ref,
                        qd_ref, kd_ref, vt_ref, qr_ref, kr_ref, vr_ref, gr_ref):
    hb = _modulated(x_ref, sh_ref, sc_ref, g_ref)
    c, sa, sb = c_ref[...], sa_ref[...], sb_ref[...]

    def project(cols):
        return jnp.dot(hb, w_ref[:, cols[0]:cols[1]], preferred_element_type=F32)

    def rotate(y):
        slabs = []
        for s in range(y.shape[1] // LANES):
            ys = y[:, s * LANES:(s + 1) * LANES]
            slabs.append(ys * c + pltpu.roll(ys, LANES - ROPE_PAIRS, 1) * sa
                         + pltpu.roll(ys, ROPE_PAIRS, 1) * sb)
        return jnp.concatenate(slabs, axis=1)

    qd_ref[...] = (rotate(project(COL_QD)) * (QK_SCALE * LOG2E)).astype(BF16)
    kd_ref[...] = rotate(project(COL_KD)).astype(BF16)
    vd = project(COL_VD)
    for h in range(HEADS):
        vt_ref[h] = vd[:, h * V_DIM:(h + 1) * V_DIM].T.astype(BF16)
    qr_ref[...] = rotate(project(COL_QR)).astype(BF16)
    kr_ref[...] = (rotate(project(COL_KR)) * QK_SCALE).astype(BF16)
    vr_ref[...] = project(COL_VR).astype(BF16)
    gr_ref[...] = _silu(project(COL_GR)).astype(BF16)


def _project_latent(x, sh, sc, g, w_bf16, tables, n_keys):
    b, n, d = x.shape
    tm = PROJ_TM
    tok = lambda w: pl.BlockSpec((None, tm, w), lambda t, bi: (bi, t, 0))
    row = pl.BlockSpec((None, 1, d), lambda t, bi: (bi, 0, 0))
    table = pl.BlockSpec((tm, LANES), lambda t, bi: (t, 0))
    out_specs = [tok(512), tok(512),
                 pl.BlockSpec((None, HEADS, V_DIM, tm), lambda t, bi: (bi, 0, 0, t)),
                 tok(256), tok(256), tok(512), tok(512)]
    out_shape = [jax.ShapeDtypeStruct((b, n, 512), BF16),
                 jax.ShapeDtypeStruct((b, n_keys, 512), BF16),
                 jax.ShapeDtypeStruct((b, HEADS, V_DIM, n_keys), BF16),
                 jax.ShapeDtypeStruct((b, n, 256), BF16),
                 jax.ShapeDtypeStruct((b, n_keys, 256), BF16),
                 jax.ShapeDtypeStruct((b, n_keys, 512), BF16),
                 jax.ShapeDtypeStruct((b, n, 512), BF16)]
    return pl.pallas_call(
        _proj_latent_kernel,
        grid=(n // tm, b),
        in_specs=[tok(d), row, row, pl.BlockSpec((1, d), lambda t, bi: (0, 0)),
                  pl.BlockSpec(w_bf16.shape, lambda t, bi: (0, 0)), table, table, table],
        out_specs=out_specs,
        out_shape=out_shape,
        compiler_params=_cparams(("arbitrary", "arbitrary")),
        name="proj_latent",
    )(x, sh, sc, g, w_bf16, *tables)


def _proj_ctx_kernel(x_ref, sh_ref, sc_ref, g_ref, w_ref, kd_in, vt_in, kr_in, vr_in,
                     kd_ref, vt_ref, kr_ref, vr_ref):
    del kd_in, vt_in, kr_in, vr_in
    hb = _modulated(x_ref, sh_ref, sc_ref, g_ref)
    b, nctx = kd_ref.shape[0], kd_ref.shape[1]

    def project(cols):
        return jnp.dot(hb, w_ref[:, cols[0]:cols[1]], preferred_element_type=F32)

    kd, vd, kr, vr = project(COL_KD), project(COL_VD), project(COL_KR) * QK_SCALE, project(COL_VR)
    for bi in range(b):
        rows = slice(bi * nctx, (bi + 1) * nctx)
        kd_ref[bi] = kd[rows].astype(BF16)
        kr_ref[bi] = kr[rows].astype(BF16)
        vr_ref[bi] = vr[rows].astype(BF16)
        for h in range(HEADS):
            vt_ref[bi, h] = vd[rows, h * V_DIM:(h + 1) * V_DIM].T.astype(BF16)


def _project_ctx(ctx, sh, sc, g, w_bf16, kd, vt, kr, vr):
    b, nctx, d = ctx.shape
    n = kd.shape[1] - nctx
    assert n % nctx == 0
    blk = n // nctx
    full = lambda a: pl.BlockSpec(a.shape, lambda i: (0,) * a.ndim)
    anyspec = pl.BlockSpec(memory_space=pl.ANY)
    x2 = ctx.reshape(b * nctx, d)
    return pl.pallas_call(
        _proj_ctx_kernel,
        grid=(1,),
        in_specs=[full(x2), full(sh), full(sc), full(g), full(w_bf16)] + [anyspec] * 4,
        out_specs=[pl.BlockSpec((b, nctx, 512), lambda i: (0, blk, 0)),
                   pl.BlockSpec((b, HEADS, V_DIM, nctx), lambda i: (0, 0, 0, blk)),
                   pl.BlockSpec((b, nctx, 256), lambda i: (0, blk, 0)),
                   pl.BlockSpec((b, nctx, 512), lambda i: (0, blk, 0))],
        out_shape=[jax.ShapeDtypeStruct(a.shape, a.dtype) for a in (kd, vt, kr, vr)],
        input_output_aliases={5: 0, 6: 1, 7: 2, 8: 3},
        compiler_params=_cparams(("arbitrary",)),
        name="proj_ctx",
    )(x2, sh, sc, g, w_bf16, kd, vt, kr, vr)


def _attn_kernel(lq1_ref, lk1_ref, lq2_ref, lk2_ref, subg_ref, q_ref, k_ref, vt_ref, o_ref,
                 sa_scr, sb_scr):
    n = q_ref.shape[0]
    lam = (jnp.exp(jnp.sum(lq1_ref[...] * lk1_ref[...], axis=-1, keepdims=True))
           - jnp.exp(jnp.sum(lq2_ref[...] * lk2_ref[...], axis=-1, keepdims=True)) + LAM_INIT)
    lane = lax.broadcasted_iota(jnp.int32, (1, LANES), 1)
    first_map = lane < HEAD_DIM
    subg = subg_ref[...]
    nq = n // ATTN_QB

    def scores(i, s_scr):
        r0 = pl.multiple_of(i * ATTN_QB, ATTN_QB)
        q = q_ref[pl.ds(r0, ATTN_QB), :]
        zero = jnp.zeros_like(q)
        q12 = jnp.concatenate([jnp.where(first_map, q, zero), jnp.where(first_map, zero, q)], axis=0)
        s = lax.dot_general(k_ref[...], q12, (((1,), (1,)), ((), ())), preferred_element_type=F32)
        s_scr[...] = s
        return jnp.max(s, axis=0, keepdims=True)

    def finish(i, s_scr, m):
        p = jnp.exp2(s_scr[...] - m)
        l = jnp.sum(p, axis=0, keepdims=True)
        ot = jnp.dot(vt_ref[...], p.astype(BF16), preferred_element_type=F32) / l
        o = (ot[:, :ATTN_QB] - lam * ot[:, ATTN_QB:]).T
        o = _rmsnorm(o, subg) * (1.0 - LAM_INIT)
        r0 = pl.multiple_of(i * ATTN_QB, ATTN_QB)
        o_ref[pl.ds(r0, ATTN_QB), :] = o.astype(o_ref.dtype)

    def step(j, m_a):
        m_b = scores(2 * j + 1, sb_scr)
        finish(2 * j, sa_scr, m_a)
        m_a = scores(2 * j + 2, sa_scr)
        finish(2 * j + 1, sb_scr, m_b)
        return m_a

    m_a = lax.fori_loop(0, nq // 2 - 1, step, scores(0, sa_scr))
    m_b = scores(nq - 1, sb_scr)
    finish(nq - 2, sa_scr, m_a)
    finish(nq - 1, sb_scr, m_b)


def _diff_attention(lq1, lk1, lq2, lk2, subg, qd, kd, vt):
    b, n, _ = qd.shape
    n_keys = kd.shape[1]
    assert (n // ATTN_QB) % 2 == 0
    vec = lambda w: pl.BlockSpec((1, w), lambda bi, h: (0, 0))
    head = lambda rows: pl.BlockSpec((None, rows, LANES), lambda bi, h: (bi, 0, h))
    return pl.pallas_call(
        _attn_kernel,
        grid=(b, HEADS),
        in_specs=[vec(HEAD_DIM)] * 4 + [vec(V_DIM), head(n), head(n_keys),
                  pl.BlockSpec((None, None, V_DIM, n_keys), lambda bi, h: (bi, h, 0, 0))],
        out_specs=head(n),
        out_shape=jax.ShapeDtypeStruct((b, n, HEADS * V_DIM), BF16),
        scratch_shapes=[pltpu.VMEM((n_keys, 2 * ATTN_QB), F32)] * 2,
        compiler_params=_cparams(("arbitrary", "arbitrary")),
        name="diff_attn",
    )(lq1, lk1, lq2, lk2, subg, qd, kd, vt)


def _ret_kernel(decf_ref, decb_ref, gn_ref, q_ref, k_ref, v_ref, g_ref, o_ref, sb_scr):
    n = q_ref.shape[0]
    c = RET_C
    nc = n // c
    pair = pl.program_id(1)

    def log_gamma(dec):
        z = -dec
        return -(jnp.maximum(z, 0.0) + jnp.log(1.0 + jnp.exp(-jnp.abs(z))))

    lgf_all = log_gamma(decf_ref[...])
    lgb_all = log_gamma(decb_ref[...])
    hsel = lax.broadcasted_iota(jnp.int32, (1, HEADS), 1)

    def pick(vec, hh):
        return jnp.sum(jnp.where(hsel == 2 * pair + hh, vec, 0.0), axis=-1, keepdims=True)

    ii = lax.broadcasted_iota(jnp.int32, (c, c), 0).astype(F32)
    jj = lax.broadcasted_iota(jnp.int32, (c, c), 1).astype(F32)
    rel = ii - jj
    pos = lax.broadcasted_iota(jnp.int32, (c, LANES), 0).astype(F32)
    lane = lax.broadcasted_iota(jnp.int32, (1, LANES), 1)

    heads = []
    for hh in range(2):
        lgf, lgb = pick(lgf_all, hh), pick(lgb_all, hh)
        decay = (jnp.where(rel >= 0, jnp.exp(jnp.maximum(rel, 0.0) * lgf), 0.0)
                 + jnp.where(rel <= 0, jnp.exp(jnp.maximum(-rel, 0.0) * lgb), 0.0))
        heads.append(dict(
            decay=decay,
            qdec_f=jnp.exp((pos + 1.0) * lgf),
            qdec_b=jnp.exp((c - pos) * lgb),
            kdec_f=jnp.exp((c - 1.0 - pos) * lgf),
            kdec_b=jnp.exp(pos * lgb),
            cdec_f=jnp.exp(c * lgf),
            cdec_b=jnp.exp(c * lgb),
            qmask=(lane < HEAD_DIM) if hh == 0 else (lane >= HEAD_DIM),
            vlo=hh * V_DIM,
        ))

    def kv_state(k, v, kdec):
        kd = (k.astype(F32) * kdec).astype(BF16)
        return lax.dot_general(kd, v, (((0,), (0,)), ((), ())), preferred_element_type=F32)

    kc = k_ref[n:n + c, :]
    sf0, sb0 = [], []
    for hd in heads:
        vc = v_ref[n:n + c, hd["vlo"]:hd["vlo"] + V_DIM]
        sf0.append(kv_state(kc, vc, hd["kdec_f"]))
        sb0.append(kv_state(kc, vc, hd["kdec_b"]))

    def bwd(t, sb):
        ci = nc - 1 - t
        r0 = pl.multiple_of(ci * c, c)
        k = k_ref[pl.ds(r0, c), :]
        new = []
        for hh, hd in enumerate(heads):
            sb_scr[ci, hh] = sb[hh].astype(BF16)
            v = v_ref[pl.ds(r0, c), hd["vlo"]:hd["vlo"] + V_DIM]
            new.append(sb[hh] * hd["cdec_b"] + kv_state(k, v, hd["kdec_b"]))
        return tuple(new)

    lax.fori_loop(0, nc, bwd, tuple(sb0), unroll=4)

    gn = gn_ref[...]

    def fwd(ci, sf):
        r0 = pl.multiple_of(ci * c, c)
        q = q_ref[pl.ds(r0, c), :]
        k = k_ref[pl.ds(r0, c), :]
        zero = jnp.zeros_like(q)
        new = []
        for hh, hd in enumerate(heads):
            lo = hd["vlo"]
            v = v_ref[pl.ds(r0, c), lo:lo + V_DIM]
            qm = jnp.where(hd["qmask"], q, zero)
            scores = lax.dot_general(qm, k, (((1,), (1,)), ((), ())), preferred_element_type=F32)
            a = (scores * hd["decay"]).astype(BF16)
            o = jnp.dot(a, v, preferred_element_type=F32)
            o = o + jnp.dot(qm, sf[hh].astype(BF16), preferred_element_type=F32) * hd["qdec_f"]
            o = o + jnp.dot(qm, sb_scr[ci, hh], preferred_element_type=F32) * hd["qdec_b"]
            mu = jnp.mean(o, axis=-1, keepdims=True)
            var = jnp.mean(jnp.square(o - mu), axis=-1, keepdims=True)
            on = (o - mu) * lax.rsqrt(var + EPS) * gn[:, lo:lo + V_DIM]
            gate = g_ref[pl.ds(r0, c), lo:lo + V_DIM].astype(F32)
            o_ref[pl.ds(r0, c), lo:lo + V_DIM] = (on * gate).astype(o_ref.dtype)
            new.append(sf[hh] * hd["cdec_f"] + kv_state(k, v, hd["kdec_f"]))
        return tuple(new)

    lax.fori_loop(0, nc, fwd, tuple(sf0), unroll=4)


def _retention(dec_f, dec_b, gn_g, qr, kr, vr, gr):
    b, n, _ = qr.shape
    n_keys = kr.shape[1]
    assert n_keys - n == RET_C and n % (4 * RET_C) == 0
    pairs = HEADS // 2
    small = pl.BlockSpec((1, HEADS), lambda bi, p: (0, 0))
    qk = lambda rows: pl.BlockSpec((None, rows, LANES), lambda bi, p: (bi, 0, p))
    wide = lambda rows: pl.BlockSpec((None, rows, 2 * V_DIM), lambda bi, p: (bi, 0, p))
    return pl.pallas_call(
        _ret_kernel,
        grid=(b, pairs),
        in_specs=[small, small, pl.BlockSpec((1, 2 * V_DIM), lambda bi, p: (0, p)),
                  qk(n), qk(n_keys), wide(n_keys), wide(n)],
        out_specs=wide(n),
        out_shape=jax.ShapeDtypeStruct((b, n, HEADS * V_DIM), BF16),
        scratch_shapes=[pltpu.VMEM((n // RET_C, 2, LANES, V_DIM), BF16)],
        compiler_params=_cparams(("arbitrary", "arbitrary")),
        name="retention",
    )(dec_f, dec_b, gn_g, qr, kr, vr, gr)


def _mlp_kernel(x_ref, od_ref, or_ref, g1_ref, sh2_ref, sc2_ref, g2_ref, n2_ref, fg_ref,
                wo_ref, w1_ref, w2_ref, o_ref):
    half = od_ref.shape[1]
    y = (jnp.dot(od_ref[...], wo_ref[0:half, :], preferred_element_type=F32)
         + jnp.dot(or_ref[...], wo_ref[half:2 * half, :], preferred_element_type=F32))
    x1 = x_ref[...] + g1_ref[...] * y
    h = _rmsnorm(x1, n2_ref[...]) * (1.0 + sc2_ref[...]) + sh2_ref[...]
    hb = h.astype(BF16)
    d_ff = w1_ref.shape[1]
    acc = None
    for f0 in range(0, d_ff, FF_CHUNK):
        u = jnp.dot(hb, w1_ref[:, f0:f0 + FF_CHUNK], preferred_element_type=F32)
        u = jnp.square(jnp.maximum(u, 0.0)).astype(BF16)
        part = jnp.dot(u, w2_ref[f0:f0 + FF_CHUNK, :], preferred_element_type=F32)
        acc = part if acc is None else acc + part
    x2 = x1 + g2_ref[...] * acc
    o_ref[...] = _rmsnorm(x2, fg_ref[...])


def _out_mlp(x, od, orr, g1, sh2, sc2, g2, n2, fg, wo, w1, w2):
    b, n, d = x.shape
    tm = MLP_TM
    tok = lambda w: pl.BlockSpec((None, tm, w), lambda bi, t: (bi, t, 0))
    row = pl.BlockSpec((None, 1, d), lambda bi, t: (bi, 0, 0))
    gain = pl.BlockSpec((1, d), lambda bi, t: (0, 0))
    resident = lambda a: pl.BlockSpec(a.shape, lambda bi, t: (0, 0), pipeline_mode=pl.Buffered(1))
    return pl.pallas_call(
        _mlp_kernel,
        grid=(b, n // tm),
        in_specs=[tok(d), tok(od.shape[2]), tok(orr.shape[2]), row, row, row, row, gain, gain,
                  resident(wo), resident(w1), resident(w2)],
        out_specs=tok(d),
        out_shape=jax.ShapeDtypeStruct((b, n, d), F32),
        compiler_params=_cparams(("arbitrary", "arbitrary")),
        name="out_mlp",
    )(x, od, orr, g1, sh2, sc2, g2, n2, fg, wo, w1, w2)


def kernel(x, c, ctx, c_ctx, w_ada, b_ada, norm1_g, norm2_g, w_in, lambda_q1, lambda_k1, lambda_q2,
           lambda_k2, diff_subln_g, ret_decay_fwd, ret_decay_bwd, ret_gn_g, w_out, w_mlp1, w_mlp2, final_g):
    assert w_ada.shape[0] == 1, "single-layer block"
    b, n, d = x.shape
    nctx = ctx.shape[1]

    c_rows = jnp.concatenate([c, c_ctx[None], jnp.zeros((8 - b - 1, d), F32)], axis=0)
    mod = _adaln(c_rows, w_ada[0], b_ada).reshape(8, N_MOD, d)
    sh1, sc1, g1, sh2, sc2, g2 = [mod[:b, i][:, None, :] for i in range(N_MOD)]

    w_in_b = w_in[0].astype(BF16)
    qd, kd, vt, qr, kr, vr, gr = _project_latent(x, sh1, sc1, norm1_g, w_in_b, _rope_tables(n), n + nctx)
    kd, vt, kr, vr = _project_ctx(ctx, mod[b, 0][None], mod[b, 1][None], norm1_g, w_in_b, kd, vt, kr, vr)

    od = _diff_attention(lambda_q1, lambda_k1, lambda_q2, lambda_k2, diff_subln_g, qd, kd, vt)
    orr = _retention(ret_decay_fwd, ret_decay_bwd, ret_gn_g, qr, kr, vr, gr)

    return _out_mlp(x, od, orr, g1, sh2, sc2, g2, norm2_g, final_g[None],
                    w_out[0].astype(BF16), w_mlp1[0].astype(BF16), w_mlp2[0].astype(BF16))
```

```python
import functools
import math

import numpy as np
import jax
import jax.numpy as jnp
from jax import lax
from jax.experimental import pallas as pl
from jax.experimental.pallas import tpu as pltpu

F32 = jnp.float32
BF16 = jnp.bfloat16

GRID_W = 64
HEADS = 4
HEAD_DIM = 64
V_DIM = 128
VT_ROWS = V_DIM + 16
ROPE_PAIRS = 16
ROPE_BASE = 10000.0
N_MOD = 6
EPS = 1e-6
LAM_INIT = 0.8 - 0.6 * math.exp(-0.3 * 0)
LOG2E = 1.4426950408889634
QK_SCALE = HEAD_DIM ** -0.5

LANES = 128
VMEM_LIMIT_BYTES = 56 * 1024 * 1024

PROJ_TM = 512
ATTN_QB = 256
RET_C = 256
MLP_TM = 512
FF_CHUNK = 1024
ADA_TN = 1536

COL_QD, COL_KD, COL_VD = (0, 512), (512, 1024), (1024, 1536)
COL_QR, COL_KR, COL_VR, COL_GR = (1536, 1792), (1792, 2048), (2048, 2560), (2560, 3072)


def _cparams(sem):
    return pltpu.CompilerParams(dimension_semantics=sem, vmem_limit_bytes=VMEM_LIMIT_BYTES)


def _rmsnorm(xf, g):
    return xf * lax.rsqrt(jnp.mean(xf * xf, axis=-1, keepdims=True) + EPS) * g


def _silu(x):
    return x * (1.0 / (1.0 + jnp.exp(-x)))


def _adaln_kernel(c_ref, w_ref, b_ref, o_ref):
    a = _silu(c_ref[...])
    o_ref[...] = jnp.dot(a, w_ref[...], preferred_element_type=F32,
                         precision=lax.Precision.HIGHEST) + b_ref[...]


def _adaln(c_rows, w_ada, b_ada):
    r, d = c_rows.shape
    n_out = w_ada.shape[1]
    return pl.pallas_call(
        _adaln_kernel,
        grid=(n_out // ADA_TN,),
        in_specs=[pl.BlockSpec((r, d), lambda j: (0, 0)),
                  pl.BlockSpec((d, ADA_TN), lambda j: (0, j)),
                  pl.BlockSpec((1, ADA_TN), lambda j: (0, j))],
        out_specs=pl.BlockSpec((r, ADA_TN), lambda j: (0, j)),
        out_shape=jax.ShapeDtypeStruct((r, n_out), F32),
        compiler_params=_cparams(("arbitrary",)),
        name="adaln",
    )(c_rows, w_ada, b_ada)


def _rope_tables(n_tokens):
    rows = n_tokens // GRID_W
    row = np.repeat(np.arange(rows), GRID_W).astype(np.float64)
    col = np.tile(np.arange(GRID_W), rows).astype(np.float64)
    inv = (np.float32(ROPE_BASE) ** (-np.arange(ROPE_PAIRS, dtype=np.float32) / ROPE_PAIRS)).astype(np.float64)
    ang_r = row[:, None] * inv
    ang_c = col[:, None] * inv
    zeros = np.zeros_like(ang_r)
    cos64 = np.concatenate([np.cos(ang_r)] * 2 + [np.cos(ang_c)] * 2, axis=1)
    sa64 = np.concatenate([-np.sin(ang_r), zeros, -np.sin(ang_c), zeros], axis=1)
    sb64 = np.concatenate([zeros, np.sin(ang_r), zeros, np.sin(ang_c)], axis=1)
    dup = lambda t: jnp.asarray(np.concatenate([t, t], axis=1), dtype=F32)
    return dup(cos64), dup(sa64), dup(sb64)


def _modulated(x_ref, sh_ref, sc_ref, g_ref):
    h = _rmsnorm(x_ref[...], g_ref[...])
    return (h * (1.0 + sc_ref[...]) + sh_ref[...]).astype(BF16)


def _proj_latent_kernel(x_ref, sh_ref, sc_ref, g_ref, w_ref, c_ref, sa_ref, sb_ref,
                        qd_ref, kd_ref, vt_ref, qr_ref, kr_ref, vr_ref, gr_ref):
    hb = _modulated(x_ref, sh_ref, sc_ref, g_ref)
    c, sa, sb = c_ref[...], sa_ref[...], sb_ref[...]

    def project(cols):
        return jnp.dot(hb, w_ref[:, cols[0]:cols[1]], preferred_element_type=F32)

    def rotate(y):
        slabs = []
        for s in range(y.shape[1] // LANES):
            ys = y[:, s * LANES:(s + 1) * LANES]
            slabs.append(ys * c + pltpu.roll(ys, LANES - ROPE_PAIRS, 1) * sa
                         + pltpu.roll(ys, ROPE_PAIRS, 1) * sb)
        return jnp.concatenate(slabs, axis=1)

    qd_ref[...] = (rotate(project(COL_QD)) * (QK_SCALE * LOG2E)).astype(BF16)
    kd_ref[...] = rotate(project(COL_KD)).astype(BF16)
    vd = project(COL_VD)
    ones = jnp.ones((VT_ROWS - V_DIM, vd.shape[0]), BF16)
    for h in range(HEADS):
        vt_ref[h, 0:V_DIM, :] = vd[:, h * V_DIM:(h + 1) * V_DIM].T.astype(BF16)
        vt_ref[h, V_DIM:VT_ROWS, :] = ones
    qr_ref[...] = rotate(project(COL_QR)).astype(BF16)
    kr_ref[...] = (rotate(project(COL_KR)) * QK_SCALE).astype(BF16)
    vr_ref[...] = project(COL_VR).astype(BF16)
    gr_ref[...] = _silu(project(COL_GR)).astype(BF16)


def _project_latent(x, sh, sc, g, w_bf16, tables, n_keys):
    b, n, d = x.shape
    tm = PROJ_TM
    tok = lambda w: pl.BlockSpec((None, tm, w), lambda t, bi: (bi, t, 0))
    row = pl.BlockSpec((None, 1, d), lambda t, bi: (bi, 0, 0))
    table = pl.BlockSpec((tm, LANES), lambda t, bi: (t, 0))
    out_specs = [tok(512), tok(512),
                 pl.BlockSpec((None, HEADS, VT_ROWS, tm), lambda t, bi: (bi, 0, 0, t)),
                 tok(256), tok(256), tok(512), tok(512)]
    out_shape = [jax.ShapeDtypeStruct((b, n, 512), BF16),
                 jax.ShapeDtypeStruct((b, n_keys, 512), BF16),
                 jax.ShapeDtypeStruct((b, HEADS, VT_ROWS, n_keys), BF16),
                 jax.ShapeDtypeStruct((b, n, 256), BF16),
                 jax.ShapeDtypeStruct((b, n_keys, 256), BF16),
                 jax.ShapeDtypeStruct((b, n_keys, 512), BF16),
                 jax.ShapeDtypeStruct((b, n, 512), BF16)]
    return pl.pallas_call(
        _proj_latent_kernel,
        grid=(n // tm, b),
        in_specs=[tok(d), row, row, pl.BlockSpec((1, d), lambda t, bi: (0, 0)),
                  pl.BlockSpec(w_bf16.shape, lambda t, bi: (0, 0)), table, table, table],
        out_specs=out_specs,
        out_shape=out_shape,
        compiler_params=_cparams(("arbitrary", "arbitrary")),
        name="proj_latent",
    )(x, sh, sc, g, w_bf16, *tables)


def _proj_ctx_kernel(x_ref, sh_ref, sc_ref, g_ref, w_ref, kd_in, vt_in, kr_in, vr_in,
                     kd_ref, vt_ref, kr_ref, vr_ref):
    del kd_in, vt_in, kr_in, vr_in
    hb = _modulated(x_ref, sh_ref, sc_ref, g_ref)
    b, nctx = kd_ref.shape[0], kd_ref.shape[1]

    def project(cols):
        return jnp.dot(hb, w_ref[:, cols[0]:cols[1]], preferred_element_type=F32)

    kd, vd, kr, vr = project(COL_KD), project(COL_VD), project(COL_KR) * QK_SCALE, project(COL_VR)
    for bi in range(b):
        rows = slice(bi * nctx, (bi + 1) * nctx)
        kd_ref[bi] = kd[rows].astype(BF16)
        kr_ref[bi] = kr[rows].astype(BF16)
        vr_ref[bi] = vr[rows].astype(BF16)
        for h in range(HEADS):
            vt_ref[bi, h, 0:V_DIM, :] = vd[rows, h * V_DIM:(h + 1) * V_DIM].T.astype(BF16)
            vt_ref[bi, h, V_DIM:VT_ROWS, :] = jnp.ones((VT_ROWS - V_DIM, nctx), BF16)


def _project_ctx(ctx, sh, sc, g, w_bf16, kd, vt, kr, vr):
    b, nctx, d = ctx.shape
    n = kd.shape[1] - nctx
    assert n % nctx == 0
    blk = n // nctx
    full = lambda a: pl.BlockSpec(a.shape, lambda i: (0,) * a.ndim)
    anyspec = pl.BlockSpec(memory_space=pl.ANY)
    x2 = ctx.reshape(b * nctx, d)
    return pl.pallas_call(
        _proj_ctx_kernel,
        grid=(1,),
        in_specs=[full(x2), full(sh), full(sc), full(g), full(w_bf16)] + [anyspec] * 4,
        out_specs=[pl.BlockSpec((b, nctx, 512), lambda i: (0, blk, 0)),
                   pl.BlockSpec((b, HEADS, VT_ROWS, nctx), lambda i: (0, 0, 0, blk)),
                   pl.BlockSpec((b, nctx, 256), lambda i: (0, blk, 0)),
                   pl.BlockSpec((b, nctx, 512), lambda i: (0, blk, 0))],
        out_shape=[jax.ShapeDtypeStruct(a.shape, a.dtype) for a in (kd, vt, kr, vr)],
        input_output_aliases={5: 0, 6: 1, 7: 2, 8: 3},
        compiler_params=_cparams(("arbitrary",)),
        name="proj_ctx",
    )(x2, sh, sc, g, w_bf16, kd, vt, kr, vr)


def _attn_kernel(lq1_ref, lk1_ref, lq2_ref, lk2_ref, subg_ref, q_ref, k_ref, vt_ref, o_ref,
                 sa_scr, sb_scr):
    n = q_ref.shape[0]
    lam = (jnp.exp(jnp.sum(lq1_ref[...] * lk1_ref[...], axis=-1, keepdims=True))
           - jnp.exp(jnp.sum(lq2_ref[...] * lk2_ref[...], axis=-1, keepdims=True)) + LAM_INIT)
    lane = lax.broadcasted_iota(jnp.int32, (1, LANES), 1)
    first_map = lane < HEAD_DIM
    subg = subg_ref[...]
    nq = n // ATTN_QB

    def scores(i, s_scr):
        r0 = pl.multiple_of(i * ATTN_QB, ATTN_QB)
        q = q_ref[pl.ds(r0, ATTN_QB), :]
        zero = jnp.zeros_like(q)
        ms = []
        for mp, qm in enumerate((jnp.where(first_map, q, zero), jnp.where(first_map, zero, q))):
            s = lax.dot_general(k_ref[...], qm, (((1,), (1,)), ((), ())), preferred_element_type=F32)
            s_scr[mp] = s
            ms.append(jnp.max(s, axis=0, keepdims=True))
        return tuple(ms)

    def finish(i, s_scr, ms):
        ots = []
        for mp in range(2):
            p = jnp.exp2(s_scr[mp] - ms[mp]).astype(BF16)
            acc = jnp.dot(vt_ref[...], p, preferred_element_type=F32)
            ots.append(acc[0:V_DIM, :] / acc[V_DIM:V_DIM + 1, :])
        o = (ots[0] - lam * ots[1]).T
        o = _rmsnorm(o, subg) * (1.0 - LAM_INIT)
        r0 = pl.multiple_of(i * ATTN_QB, ATTN_QB)
        o_ref[pl.ds(r0, ATTN_QB), :] = o.astype(o_ref.dtype)

    def step(j, m_a):
        m_b = scores(2 * j + 1, sb_scr)
        finish(2 * j, sa_scr, m_a)
        m_a = scores(2 * j + 2, sa_scr)
        finish(2 * j + 1, sb_scr, m_b)
        return m_a

    m_a = lax.fori_loop(0, nq // 2 - 1, step, scores(0, sa_scr))
    m_b = scores(nq - 1, sb_scr)
    finish(nq - 2, sa_scr, m_a)
    finish(nq - 1, sb_scr, m_b)


def _diff_attention(lq1, lk1, lq2, lk2, subg, qd, kd, vt):
    b, n, _ = qd.shape
    n_keys = kd.shape[1]
    assert (n // ATTN_QB) % 2 == 0
    vec = lambda w: pl.BlockSpec((1, w), lambda bi, h: (0, 0))
    head = lambda rows: pl.BlockSpec((None, rows, LANES), lambda bi, h: (bi, 0, h))
    return pl.pallas_call(
        _attn_kernel,
        grid=(b, HEADS),
        in_specs=[vec(HEAD_DIM)] * 4 + [vec(V_DIM), head(n), head(n_keys),
                  pl.BlockSpec((None, None, VT_ROWS, n_keys), lambda bi, h: (bi, h, 0, 0))],
        out_specs=head(n),
        out_shape=jax.ShapeDtypeStruct((b, n, HEADS * V_DIM), BF16),
        scratch_shapes=[pltpu.VMEM((2, n_keys, ATTN_QB), F32)] * 2,
        compiler_params=_cparams(("arbitrary", "arbitrary")),
        name="diff_attn",
    )(lq1, lk1, lq2, lk2, subg, qd, kd, vt)


def _ret_kernel(decf_ref, decb_ref, gn_ref, q_ref, k_ref, v_ref, g_ref, o_ref, sb_scr):
    n = q_ref.shape[0]
    c = RET_C
    nc = n // c
    pair = pl.program_id(1)

    def log_gamma(dec):
        z = -dec
        return -(jnp.maximum(z, 0.0) + jnp.log(1.0 + jnp.exp(-jnp.abs(z))))

    lgf_all = log_gamma(decf_ref[...])
    lgb_all = log_gamma(decb_ref[...])
    hsel = lax.broadcasted_iota(jnp.int32, (1, HEADS), 1)

    def pick(vec, hh):
        return jnp.sum(jnp.where(hsel == 2 * pair + hh, vec, 0.0), axis=-1, keepdims=True)

    ii = lax.broadcasted_iota(jnp.int32, (c, c), 0).astype(F32)
    jj = lax.broadcasted_iota(jnp.int32, (c, c), 1).astype(F32)
    rel = ii - jj
    pos = lax.broadcasted_iota(jnp.int32, (c, LANES), 0).astype(F32)
    lane = lax.broadcasted_iota(jnp.int32, (1, LANES), 1)

    heads = []
    for hh in range(2):
        lgf, lgb = pick(lgf_all, hh), pick(lgb_all, hh)
        decay = (jnp.where(rel >= 0, jnp.exp(jnp.maximum(rel, 0.0) * lgf), 0.0)
                 + jnp.where(rel <= 0, jnp.exp(jnp.maximum(-rel, 0.0) * lgb), 0.0))
        heads.append(dict(
            decay=decay,
            qdec_f=jnp.exp((pos + 1.0) * lgf),
            qdec_b=jnp.exp((c - pos) * lgb),
            kdec_f=jnp.exp((c - 1.0 - pos) * lgf),
            kdec_b=jnp.exp(pos * lgb),
            cdec_f=jnp.exp(c * lgf),
            cdec_b=jnp.exp(c * lgb),
            qmask=(lane < HEAD_DIM) if hh == 0 else (lane >= HEAD_DIM),
            vlo=hh * V_DIM,
        ))

    def kv_state(k, v, kdec):
        kd = (k.astype(F32) * kdec).astype(BF16)
        return lax.dot_general(kd, v, (((0,), (0,)), ((), ())), preferred_element_type=F32)

    kc = k_ref[n:n + c, :]
    sf0, sb0 = [], []
    for hd in heads:
        vc = v_ref[n:n + c, hd["vlo"]:hd["vlo"] + V_DIM]
        sf0.append(kv_state(kc, vc, hd["kdec_f"]))
        sb0.append(kv_state(kc, vc, hd["kdec_b"]))

    def bwd(t, sb):
        ci = nc - 1 - t
        r0 = pl.multiple_of(ci * c, c)
        k = k_ref[pl.ds(r0, c), :]
        new = []
        for hh, hd in enumerate(heads):
            sb_scr[ci, hh] = sb[hh].astype(BF16)
            v = v_ref[pl.ds(r0, c), hd["vlo"]:hd["vlo"] + V_DIM]
            new.append(sb[hh] * hd["cdec_b"] + kv_state(k, v, hd["kdec_b"]))
        return tuple(new)

    lax.fori_loop(0, nc, bwd, tuple(sb0), unroll=4)

    gn = gn_ref[...]

    def fwd(ci, sf):
        r0 = pl.multiple_of(ci * c, c)
        q = q_ref[pl.ds(r0, c), :]
        k = k_ref[pl.ds(r0, c), :]
        zero = jnp.zeros_like(q)
        new = []
        for hh, hd in enumerate(heads):
            lo = hd["vlo"]
            v = v_ref[pl.ds(r0, c), lo:lo + V_DIM]
            qm = jnp.where(hd["qmask"], q, zero)
            scores = lax.dot_general(qm, k, (((1,), (1,)), ((), ())), preferred_element_type=F32)
            a = (scores * hd["decay"]).astype(BF16)
            o = jnp.dot(a, v, preferred_element_type=F32)
            o = o + jnp.dot(qm, sf[hh].astype(BF16), preferred_element_type=F32) * hd["qdec_f"]
            o = o + jnp.dot(qm, sb_scr[ci, hh], preferred_element_type=F32) * hd["qdec_b"]
            mu = jnp.mean(o, axis=-1, keepdims=True)
            var = jnp.mean(jnp.square(o - mu), axis=-1, keepdims=True)
            on = (o - mu) * lax.rsqrt(var + EPS) * gn[:, lo:lo + V_DIM]
            gate = g_ref[pl.ds(r0, c), lo:lo + V_DIM].astype(F32)
            o_ref[pl.ds(r0, c), lo:lo + V_DIM] = (on * gate).astype(o_ref.dtype)
            new.append(sf[hh] * hd["cdec_f"] + kv_state(k, v, hd["kdec_f"]))
        return tuple(new)

    lax.fori_loop(0, nc, fwd, tuple(sf0), unroll=4)


def _retention(dec_f, dec_b, gn_g, qr, kr, vr, gr):
    b, n, _ = qr.shape
    n_keys = kr.shape[1]
    assert n_keys - n == RET_C and n % (4 * RET_C) == 0
    pairs = HEADS // 2
    small = pl.BlockSpec((1, HEADS), lambda bi, p: (0, 0))
    qk = lambda rows: pl.BlockSpec((None, rows, LANES), lambda bi, p: (bi, 0, p))
    wide = lambda rows: pl.BlockSpec((None, rows, 2 * V_DIM), lambda bi, p: (bi, 0, p))
    return pl.pallas_call(
        _ret_kernel,
        grid=(b, pairs),
        in_specs=[small, small, pl.BlockSpec((1, 2 * V_DIM), lambda bi, p: (0, p)),
                  qk(n), qk(n_keys), wide(n_keys), wide(n)],
        out_specs=wide(n),
        out_shape=jax.ShapeDtypeStruct((b, n, HEADS * V_DIM), BF16),
        scratch_shapes=[pltpu.VMEM((n // RET_C, 2, LANES, V_DIM), BF16)],
        compiler_params=_cparams(("arbitrary", "arbitrary")),
        name="retention",
    )(dec_f, dec_b, gn_g, qr, kr, vr, gr)


def _mlp_kernel(x_ref, od_ref, or_ref, g1_ref, sh2_ref, sc2_ref, g2_ref, n2_ref, fg_ref,
                wo_ref, w1_ref, w2_ref, o_ref):
    half = od_ref.shape[1]
    y = (jnp.dot(od_ref[...], wo_ref[0:half, :], preferred_element_type=F32)
         + jnp.dot(or_ref[...], wo_ref[half:2 * half, :], preferred_element_type=F32))
    x1 = x_ref[...] + g1_ref[...] * y
    h = _rmsnorm(x1, n2_ref[...]) * (1.0 + sc2_ref[...]) + sh2_ref[...]
    hb = h.astype(BF16)
    d_ff = w1_ref.shape[1]
    acc = None
    for f0 in range(0, d_ff, FF_CHUNK):
        u = jnp.dot(hb, w1_ref[:, f0:f0 + FF_CHUNK], preferred_element_type=F32)
        u = jnp.square(jnp.maximum(u, 0.0)).astype(BF16)
        part = jnp.dot(u, w2_ref[f0:f0 + FF_CHUNK, :], preferred_element_type=F32)
        acc = part if acc is None else acc + part
    x2 = x1 + g2_ref[...] * acc
    o_ref[...] = _rmsnorm(x2, fg_ref[...])


def _out_mlp(x, od, orr, g1, sh2, sc2, g2, n2, fg, wo, w1, w2):
    b, n, d = x.shape
    tm = MLP_TM
    tok = lambda w: pl.BlockSpec((None, tm, w), lambda bi, t: (bi, t, 0))
    row = pl.BlockSpec((None, 1, d), lambda bi, t: (bi, 0, 0))
    gain = pl.BlockSpec((1, d), lambda bi, t: (0, 0))
    resident = lambda a: pl.BlockSpec(a.shape, lambda bi, t: (0, 0), pipeline_mode=pl.Buffered(1))
    return pl.pallas_call(
        _mlp_kernel,
        grid=(b, n // tm),
        in_specs=[tok(d), tok(od.shape[2]), tok(orr.shape[2]), row, row, row, row, gain, gain,
                  resident(wo), resident(w1), resident(w2)],
        out_specs=tok(d),
        out_shape=jax.ShapeDtypeStruct((b, n, d), F32),
        compiler_params=_cparams(("arbitrary", "arbitrary")),
        name="out_mlp",
    )(x, od, orr, g1, sh2, sc2, g2, n2, fg, wo, w1, w2)


def kernel(x, c, ctx, c_ctx, w_ada, b_ada, norm1_g, norm2_g, w_in, lambda_q1, lambda_k1, lambda_q2,
           lambda_k2, diff_subln_g, ret_decay_fwd, ret_decay_bwd, ret_gn_g, w_out, w_mlp1, w_mlp2, final_g):
    assert w_ada.shape[0] == 1, "single-layer block"
    b, n, d = x.shape
    nctx = ctx.shape[1]

    c_rows = jnp.concatenate([c, c_ctx[None], jnp.zeros((8 - b - 1, d), F32)], axis=0)
    mod = _adaln(c_rows, w_ada[0], b_ada).reshape(8, N_MOD, d)
    sh1, sc1, g1, sh2, sc2, g2 = [mod[:b, i][:, None, :] for i in range(N_MOD)]

    w_in_b = w_in[0].astype(BF16)
    qd, kd, vt, qr, kr, vr, gr = _project_latent(x, sh1, sc1, norm1_g, w_in_b, _rope_tables(n), n + nctx)
    kd, vt, kr, vr = _project_ctx(ctx, mod[b, 0][None], mod[b, 1][None], norm1_g, w_in_b, kd, vt, kr, vr)

    od = _diff_attention(lambda_q1, lambda_k1, lambda_q2, lambda_k2, diff_subln_g, qd, kd, vt)
    orr = _retention(ret_decay_fwd, ret_decay_bwd, ret_gn_g, qr, kr, vr, gr)

    return _out_mlp(x, od, orr, g1, sh2, sc2, g2, norm2_g, final_g[None],
                    w_out[0].astype(BF16), w_mlp1[0].astype(BF16), w_mlp2[0].astype(BF16))
```

```python
import functools
import math

import numpy as np
import jax
import jax.numpy as jnp
from jax import lax
from jax.experimental import pallas as pl
from jax.experimental.pallas import tpu as pltpu

F32 = jnp.float32
BF16 = jnp.bfloat16

GRID_W = 64
HEADS = 4
HEAD_DIM = 64
V_DIM = 128
VT_ROWS = V_DIM + 16
ROPE_PAIRS = 16
ROPE_BASE = 10000.0
N_MOD = 6
EPS = 1e-6
LAM_INIT = 0.8 - 0.6 * math.exp(-0.3 * 0)
LOG2E = 1.4426950408889634
QK_SCALE = HEAD_DIM ** -0.5

LANES = 128
VMEM_LIMIT_BYTES = 56 * 1024 * 1024

PROJ_TM = 512
ATTN_QB = 256
ATTN_GROUPS = 8
RET_C = 256
MLP_TM = 512
FF_CHUNK = 1024
ADA_TN = 1536

COL_QD, COL_KD, COL_VD = (0, 512), (512, 1024), (1024, 1536)
COL_QR, COL_KR, COL_VR, COL_GR = (1536, 1792), (1792, 2048), (2048, 2560), (2560, 3072)


def _cparams(sem):
    return pltpu.CompilerParams(dimension_semantics=sem, vmem_limit_bytes=VMEM_LIMIT_BYTES)


def _rmsnorm(xf, g):
    return xf * lax.rsqrt(jnp.mean(xf * xf, axis=-1, keepdims=True) + EPS) * g


def _silu(x):
    return x * (1.0 / (1.0 + jnp.exp(-x)))


def _adaln_kernel(c_ref, w_ref, b_ref, o_ref):
    a = _silu(c_ref[...])
    o_ref[...] = jnp.dot(a, w_ref[...], preferred_element_type=F32,
                         precision=lax.Precision.HIGHEST) + b_ref[...]


def _adaln(c_rows, w_ada, b_ada):
    r, d = c_rows.shape
    n_out = w_ada.shape[1]
    return pl.pallas_call(
        _adaln_kernel,
        grid=(n_out // ADA_TN,),
        in_specs=[pl.BlockSpec((r, d), lambda j: (0, 0)),
                  pl.BlockSpec((d, ADA_TN), lambda j: (0, j)),
                  pl.BlockSpec((1, ADA_TN), lambda j: (0, j))],
        out_specs=pl.BlockSpec((r, ADA_TN), lambda j: (0, j)),
        out_shape=jax.ShapeDtypeStruct((r, n_out), F32),
        compiler_params=_cparams(("arbitrary",)),
        name="adaln",
    )(c_rows, w_ada, b_ada)


def _rope_tables(n_tokens):
    rows = n_tokens // GRID_W
    row = np.repeat(np.arange(rows), GRID_W).astype(np.float64)
    col = np.tile(np.arange(GRID_W), rows).astype(np.float64)
    inv = (np.float32(ROPE_BASE) ** (-np.arange(ROPE_PAIRS, dtype=np.float32) / ROPE_PAIRS)).astype(np.float64)
    ang_r = row[:, None] * inv
    ang_c = col[:, None] * inv
    zeros = np.zeros_like(ang_r)
    cos64 = np.concatenate([np.cos(ang_r)] * 2 + [np.cos(ang_c)] * 2, axis=1)
    sa64 = np.concatenate([-np.sin(ang_r), zeros, -np.sin(ang_c), zeros], axis=1)
    sb64 = np.concatenate([zeros, np.sin(ang_r), zeros, np.sin(ang_c)], axis=1)
    dup = lambda t: jnp.asarray(np.concatenate([t, t], axis=1), dtype=F32)
    return dup(cos64), dup(sa64), dup(sb64)


def _modulated(x_ref, sh_ref, sc_ref, g_ref):
    h = _rmsnorm(x_ref[...], g_ref[...])
    return (h * (1.0 + sc_ref[...]) + sh_ref[...]).astype(BF16)


def _proj_latent_kernel(x_ref, sh_ref, sc_ref, g_ref, w_ref, c_ref, sa_ref, sb_ref,
                        qd_ref, kd_ref, vt_ref, qr_ref, kr_ref, vr_ref, gr_ref):
    hb = _modulated(x_ref, sh_ref, sc_ref, g_ref)
    c, sa, sb = c_ref[...], sa_ref[...], sb_ref[...]

    def project(cols):
        return jnp.dot(hb, w_ref[:, cols[0]:cols[1]], preferred_element_type=F32)

    def rotate(y):
        slabs = []
        for s in range(y.shape[1] // LANES):
            ys = y[:, s * LANES:(s + 1) * LANES]
            slabs.append(ys * c + pltpu.roll(ys, LANES - ROPE_PAIRS, 1) * sa
                         + pltpu.roll(ys, ROPE_PAIRS, 1) * sb)
        return jnp.concatenate(slabs, axis=1)

    qd_ref[...] = (rotate(project(COL_QD)) * (QK_SCALE * LOG2E)).astype(BF16)
    kd_ref[...] = rotate(project(COL_KD)).astype(BF16)
    vd = project(COL_VD)
    ones = jnp.ones((VT_ROWS - V_DIM, vd.shape[0]), BF16)
    for h in range(HEADS):
        vt_ref[h, 0:V_DIM, :] = vd[:, h * V_DIM:(h + 1) * V_DIM].T.astype(BF16)
        vt_ref[h, V_DIM:VT_ROWS, :] = ones
    qr_ref[...] = rotate(project(COL_QR)).astype(BF16)
    kr_ref[...] = (rotate(project(COL_KR)) * QK_SCALE).astype(BF16)
    vr_ref[...] = project(COL_VR).astype(BF16)
    gr_ref[...] = _silu(project(COL_GR)).astype(BF16)


def _project_latent(x, sh, sc, g, w_bf16, tables, n_keys):
    b, n, d = x.shape
    tm = PROJ_TM
    tok = lambda w: pl.BlockSpec((None, tm, w), lambda t, bi: (bi, t, 0))
    row = pl.BlockSpec((None, 1, d), lambda t, bi: (bi, 0, 0))
    table = pl.BlockSpec((tm, LANES), lambda t, bi: (t, 0))
    out_specs = [tok(512), tok(512),
                 pl.BlockSpec((None, HEADS, VT_ROWS, tm), lambda t, bi: (bi, 0, 0, t)),
                 tok(256), tok(256), tok(512), tok(512)]
    out_shape = [jax.ShapeDtypeStruct((b, n, 512), BF16),
                 jax.ShapeDtypeStruct((b, n_keys, 512), BF16),
                 jax.ShapeDtypeStruct((b, HEADS, VT_ROWS, n_keys), BF16),
                 jax.ShapeDtypeStruct((b, n, 256), BF16),
                 jax.ShapeDtypeStruct((b, n_keys, 256), BF16),
                 jax.ShapeDtypeStruct((b, n_keys, 512), BF16),
                 jax.ShapeDtypeStruct((b, n, 512), BF16)]
    return pl.pallas_call(
        _proj_latent_kernel,
        grid=(n // tm, b),
        in_specs=[tok(d), row, row, pl.BlockSpec((1, d), lambda t, bi: (0, 0)),
                  pl.BlockSpec(w_bf16.shape, lambda t, bi: (0, 0)), table, table, table],
        out_specs=out_specs,
        out_shape=out_shape,
        compiler_params=_cparams(("arbitrary", "arbitrary")),
        name="proj_latent",
    )(x, sh, sc, g, w_bf16, *tables)


def _proj_ctx_kernel(x_ref, sh_ref, sc_ref, g_ref, w_ref, kd_in, vt_in, kr_in, vr_in,
                     kd_ref, vt_ref, kr_ref, vr_ref):
    del kd_in, vt_in, kr_in, vr_in
    hb = _modulated(x_ref, sh_ref, sc_ref, g_ref)
    b, nctx = kd_ref.shape[0], kd_ref.shape[1]

    def project(cols):
        return jnp.dot(hb, w_ref[:, cols[0]:cols[1]], preferred_element_type=F32)

    kd, vd, kr, vr = project(COL_KD), project(COL_VD), project(COL_KR) * QK_SCALE, project(COL_VR)
    for bi in range(b):
        rows = slice(bi * nctx, (bi + 1) * nctx)
        kd_ref[bi] = kd[rows].astype(BF16)
        kr_ref[bi] = kr[rows].astype(BF16)
        vr_ref[bi] = vr[rows].astype(BF16)
        for h in range(HEADS):
            vt_ref[bi, h, 0:V_DIM, :] = vd[rows, h * V_DIM:(h + 1) * V_DIM].T.astype(BF16)
            vt_ref[bi, h, V_DIM:VT_ROWS, :] = jnp.ones((VT_ROWS - V_DIM, nctx), BF16)


def _project_ctx(ctx, sh, sc, g, w_bf16, kd, vt, kr, vr):
    b, nctx, d = ctx.shape
    n = kd.shape[1] - nctx
    assert n % nctx == 0
    blk = n // nctx
    full = lambda a: pl.BlockSpec(a.shape, lambda i: (0,) * a.ndim)
    anyspec = pl.BlockSpec(memory_space=pl.ANY)
    x2 = ctx.reshape(b * nctx, d)
    return pl.pallas_call(
        _proj_ctx_kernel,
        grid=(1,),
        in_specs=[full(x2), full(sh), full(sc), full(g), full(w_bf16)] + [anyspec] * 4,
        out_specs=[pl.BlockSpec((b, nctx, 512), lambda i: (0, blk, 0)),
                   pl.BlockSpec((b, HEADS, VT_ROWS, nctx), lambda i: (0, 0, 0, blk)),
                   pl.BlockSpec((b, nctx, 256), lambda i: (0, blk, 0)),
                   pl.BlockSpec((b, nctx, 512), lambda i: (0, blk, 0))],
        out_shape=[jax.ShapeDtypeStruct(a.shape, a.dtype) for a in (kd, vt, kr, vr)],
        input_output_aliases={5: 0, 6: 1, 7: 2, 8: 3},
        compiler_params=_cparams(("arbitrary",)),
        name="proj_ctx",
    )(x2, sh, sc, g, w_bf16, kd, vt, kr, vr)


def _attn_kernel(lq1_ref, lk1_ref, lq2_ref, lk2_ref, subg_ref, q_ref, k_ref, vt_ref, o_ref,
                 sa_scr, sb_scr):
    n = q_ref.shape[0]
    lam = (jnp.exp(jnp.sum(lq1_ref[...] * lk1_ref[...], axis=-1, keepdims=True))
           - jnp.exp(jnp.sum(lq2_ref[...] * lk2_ref[...], axis=-1, keepdims=True)) + LAM_INIT)
    lane = lax.broadcasted_iota(jnp.int32, (1, LANES), 1)
    first_map = lane < HEAD_DIM
    subg = subg_ref[...]
    nq = n // ATTN_QB

    n_keys = k_ref.shape[0]
    bounds = [round(g * n_keys / ATTN_GROUPS / 256) * 256 for g in range(ATTN_GROUPS + 1)]
    groups = list(zip(bounds[:-1], bounds[1:]))

    def scores(i, s_scr):
        r0 = pl.multiple_of(i * ATTN_QB, ATTN_QB)
        q = q_ref[pl.ds(r0, ATTN_QB), :]
        zero = jnp.zeros_like(q)
        qms = (jnp.where(first_map, q, zero), jnp.where(first_map, zero, q))
        ms = [None, None]
        for lo, hi in groups:
            for mp in range(2):
                s = lax.dot_general(k_ref[lo:hi, :], qms[mp], (((1,), (1,)), ((), ())),
                                    preferred_element_type=F32)
                s_scr[mp, lo:hi, :] = s
                mg = jnp.max(s, axis=0, keepdims=True)
                ms[mp] = mg if ms[mp] is None else jnp.maximum(ms[mp], mg)
            yield tuple(ms)

    def finish(i, s_scr, ms):
        accs = [None, None]
        for g, (lo, hi) in enumerate(groups):
            for mp in range(2):
                p = jnp.exp2(s_scr[mp, lo:hi, :] - ms[mp]).astype(BF16)
                part = jnp.dot(vt_ref[:, lo:hi], p, preferred_element_type=F32)
                accs[mp] = part if accs[mp] is None else accs[mp] + part
            if g == len(groups) - 1:
                ots = [a[0:V_DIM, :] / a[V_DIM:V_DIM + 1, :] for a in accs]
                o = (ots[0] - lam * ots[1]).T
                o = _rmsnorm(o, subg) * (1.0 - LAM_INIT)
                r0 = pl.multiple_of(i * ATTN_QB, ATTN_QB)
                o_ref[pl.ds(r0, ATTN_QB), :] = o.astype(o_ref.dtype)
            yield None

    def run(*stages):
        last = [None] * len(stages)
        for vals in zip(*stages):
            last = list(vals)
        return last

    def step(j, m_a):
        m_b, _ = run(scores(2 * j + 1, sb_scr), finish(2 * j, sa_scr, m_a))
        m_a, _ = run(scores(2 * j + 2, sa_scr), finish(2 * j + 1, sb_scr, m_b))
        return m_a

    (m_a,) = run(scores(0, sa_scr))
    m_a = lax.fori_loop(0, nq // 2 - 1, step, m_a)
    m_b, _ = run(scores(nq - 1, sb_scr), finish(nq - 2, sa_scr, m_a))
    run(finish(nq - 1, sb_scr, m_b))


def _diff_attention(lq1, lk1, lq2, lk2, subg, qd, kd, vt):
    b, n, _ = qd.shape
    n_keys = kd.shape[1]
    assert (n // ATTN_QB) % 2 == 0
    vec = lambda w: pl.BlockSpec((1, w), lambda bi, h: (0, 0))
    head = lambda rows: pl.BlockSpec((None, rows, LANES), lambda bi, h: (bi, 0, h))
    return pl.pallas_call(
        _attn_kernel,
        grid=(b, HEADS),
        in_specs=[vec(HEAD_DIM)] * 4 + [vec(V_DIM), head(n), head(n_keys),
                  pl.BlockSpec((None, None, VT_ROWS, n_keys), lambda bi, h: (bi, h, 0, 0))],
        out_specs=head(n),
        out_shape=jax.ShapeDtypeStruct((b, n, HEADS * V_DIM), BF16),
        scratch_shapes=[pltpu.VMEM((2, n_keys, ATTN_QB), F32)] * 2,
        compiler_params=_cparams(("arbitrary", "arbitrary")),
        name="diff_attn",
    )(lq1, lk1, lq2, lk2, subg, qd, kd, vt)


def _ret_kernel(decf_ref, decb_ref, gn_ref, q_ref, k_ref, v_ref, g_ref, o_ref, sb_scr):
    n = q_ref.shape[0]
    c = RET_C
    nc = n // c
    pair = pl.program_id(1)

    def log_gamma(dec):
        z = -dec
        return -(jnp.maximum(z, 0.0) + jnp.log(1.0 + jnp.exp(-jnp.abs(z))))

    lgf_all = log_gamma(decf_ref[...])
    lgb_all = log_gamma(decb_ref[...])
    hsel = lax.broadcasted_iota(jnp.int32, (1, HEADS), 1)

    def pick(vec, hh):
        return jnp.sum(jnp.where(hsel == 2 * pair + hh, vec, 0.0), axis=-1, keepdims=True)

    ii = lax.broadcasted_iota(jnp.int32, (c, c), 0).astype(F32)
    jj = lax.broadcasted_iota(jnp.int32, (c, c), 1).astype(F32)
    rel = ii - jj
    pos = lax.broadcasted_iota(jnp.int32, (c, LANES), 0).astype(F32)
    lane = lax.broadcasted_iota(jnp.int32, (1, LANES), 1)

    heads = []
    for hh in range(2):
        lgf, lgb = pick(lgf_all, hh), pick(lgb_all, hh)
        decay = (jnp.where(rel >= 0, jnp.exp(jnp.maximum(rel, 0.0) * lgf), 0.0)
                 + jnp.where(rel <= 0, jnp.exp(jnp.maximum(-rel, 0.0) * lgb), 0.0))
        heads.append(dict(
            decay=decay,
            qdec_f=jnp.exp((pos + 1.0) * lgf),
            qdec_b=jnp.exp((c - pos) * lgb),
            kdec_f=jnp.exp((c - 1.0 - pos) * lgf),
            kdec_b=jnp.exp(pos * lgb),
            cdec_f=jnp.exp(c * lgf),
            cdec_b=jnp.exp(c * lgb),
            qmask=(lane < HEAD_DIM) if hh == 0 else (lane >= HEAD_DIM),
            vlo=hh * V_DIM,
        ))

    def kv_state(k, v, kdec):
        kd = (k.astype(F32) * kdec).astype(BF16)
        return lax.dot_general(kd, v, (((0,), (0,)), ((), ())), preferred_element_type=F32)

    kc = k_ref[n:n + c, :]
    sf0, sb0 = [], []
    for hd in heads:
        vc = v_ref[n:n + c, hd["vlo"]:hd["vlo"] + V_DIM]
        sf0.append(kv_state(kc, vc, hd["kdec_f"]))
        sb0.append(kv_state(kc, vc, hd["kdec_b"]))

    def bwd(t, sb):
        ci = nc - 1 - t
        r0 = pl.multiple_of(ci * c, c)
        k = k_ref[pl.ds(r0, c), :]
        new = []
        for hh, hd in enumerate(heads):
            sb_scr[ci, hh] = sb[hh].astype(BF16)
            v = v_ref[pl.ds(r0, c), hd["vlo"]:hd["vlo"] + V_DIM]
            new.append(sb[hh] * hd["cdec_b"] + kv_state(k, v, hd["kdec_b"]))
        return tuple(new)

    lax.fori_loop(0, nc, bwd, tuple(sb0), unroll=4)

    gn = gn_ref[...]

    def fwd(ci, sf):
        r0 = pl.multiple_of(ci * c, c)
        q = q_ref[pl.ds(r0, c), :]
        k = k_ref[pl.ds(r0, c), :]
        zero = jnp.zeros_like(q)
        new = []
        for hh, hd in enumerate(heads):
            lo = hd["vlo"]
            v = v_ref[pl.ds(r0, c), lo:lo + V_DIM]
            qm = jnp.where(hd["qmask"], q, zero)
            scores = lax.dot_general(qm, k, (((1,), (1,)), ((), ())), preferred_element_type=F32)
            a = (scores * hd["decay"]).astype(BF16)
            o = jnp.dot(a, v, preferred_element_type=F32)
            o = o + jnp.dot(qm, sf[hh].astype(BF16), preferred_element_type=F32) * hd["qdec_f"]
            o = o + jnp.dot(qm, sb_scr[ci, hh], preferred_element_type=F32) * hd["qdec_b"]
            mu = jnp.mean(o, axis=-1, keepdims=True)
            var = jnp.mean(jnp.square(o - mu), axis=-1, keepdims=True)
            on = (o - mu) * lax.rsqrt(var + EPS) * gn[:, lo:lo + V_DIM]
            gate = g_ref[pl.ds(r0, c), lo:lo + V_DIM].astype(F32)
            o_ref[pl.ds(r0, c), lo:lo + V_DIM] = (on * gate).astype(o_ref.dtype)
            new.append(sf[hh] * hd["cdec_f"] + kv_state(k, v, hd["kdec_f"]))
        return tuple(new)

    lax.fori_loop(0, nc, fwd, tuple(sf0), unroll=4)


def _retention(dec_f, dec_b, gn_g, qr, kr, vr, gr):
    b, n, _ = qr.shape
    n_keys = kr.shape[1]
    assert n_keys - n == RET_C and n % (4 * RET_C) == 0
    pairs = HEADS // 2
    small = pl.BlockSpec((1, HEADS), lambda bi, p: (0, 0))
    qk = lambda rows: pl.BlockSpec((None, rows, LANES), lambda bi, p: (bi, 0, p))
    wide = lambda rows: pl.BlockSpec((None, rows, 2 * V_DIM), lambda bi, p: (bi, 0, p))
    return pl.pallas_call(
        _ret_kernel,
        grid=(b, pairs),
        in_specs=[small, small, pl.BlockSpec((1, 2 * V_DIM), lambda bi, p: (0, p)),
                  qk(n), qk(n_keys), wide(n_keys), wide(n)],
        out_specs=wide(n),
        out_shape=jax.ShapeDtypeStruct((b, n, HEADS * V_DIM), BF16),
        scratch_shapes=[pltpu.VMEM((n // RET_C, 2, LANES, V_DIM), BF16)],
        compiler_params=_cparams(("arbitrary", "arbitrary")),
        name="retention",
    )(dec_f, dec_b, gn_g, qr, kr, vr, gr)


def _mlp_kernel(x_ref, od_ref, or_ref, g1_ref, sh2_ref, sc2_ref, g2_ref, n2_ref, fg_ref,
                wo_ref, w1_ref, w2_ref, o_ref):
    half = od_ref.shape[1]
    y = (jnp.dot(od_ref[...], wo_ref[0:half, :], preferred_element_type=F32)
         + jnp.dot(or_ref[...], wo_ref[half:2 * half, :], preferred_element_type=F32))
    x1 = x_ref[...] + g1_ref[...] * y
    h = _rmsnorm(x1, n2_ref[...]) * (1.0 + sc2_ref[...]) + sh2_ref[...]
    hb = h.astype(BF16)
    d_ff = w1_ref.shape[1]
    acc = None
    for f0 in range(0, d_ff, FF_CHUNK):
        u = jnp.dot(hb, w1_ref[:, f0:f0 + FF_CHUNK], preferred_element_type=F32)
        u = jnp.square(jnp.maximum(u, 0.0)).astype(BF16)
        part = jnp.dot(u, w2_ref[f0:f0 + FF_CHUNK, :], preferred_element_type=F32)
        acc = part if acc is None else acc + part
    x2 = x1 + g2_ref[...] * acc
    o_ref[...] = _rmsnorm(x2, fg_ref[...])


def _out_mlp(x, od, orr, g1, sh2, sc2, g2, n2, fg, wo, w1, w2):
    b, n, d = x.shape
    tm = MLP_TM
    tok = lambda w: pl.BlockSpec((None, tm, w), lambda bi, t: (bi, t, 0))
    row = pl.BlockSpec((None, 1, d), lambda bi, t: (bi, 0, 0))
    gain = pl.BlockSpec((1, d), lambda bi, t: (0, 0))
    resident = lambda a: pl.BlockSpec(a.shape, lambda bi, t: (0, 0), pipeline_mode=pl.Buffered(1))
    return pl.pallas_call(
        _mlp_kernel,
        grid=(b, n // tm),
        in_specs=[tok(d), tok(od.shape[2]), tok(orr.shape[2]), row, row, row, row, gain, gain,
                  resident(wo), resident(w1), resident(w2)],
        out_specs=tok(d),
        out_shape=jax.ShapeDtypeStruct((b, n, d), F32),
        compiler_params=_cparams(("arbitrary", "arbitrary")),
        name="out_mlp",
    )(x, od, orr, g1, sh2, sc2, g2, n2, fg, wo, w1, w2)


def kernel(x, c, ctx, c_ctx, w_ada, b_ada, norm1_g, norm2_g, w_in, lambda_q1, lambda_k1, lambda_q2,
           lambda_k2, diff_subln_g, ret_decay_fwd, ret_decay_bwd, ret_gn_g, w_out, w_mlp1, w_mlp2, final_g):
    assert w_ada.shape[0] == 1, "single-layer block"
    b, n, d = x.shape
    nctx = ctx.shape[1]

    c_rows = jnp.concatenate([c, c_ctx[None], jnp.zeros((8 - b - 1, d), F32)], axis=0)
    mod = _adaln(c_rows, w_ada[0], b_ada).reshape(8, N_MOD, d)
    sh1, sc1, g1, sh2, sc2, g2 = [mod[:b, i][:, None, :] for i in range(N_MOD)]

    w_in_b = w_in[0].astype(BF16)
    qd, kd, vt, qr, kr, vr, gr = _project_latent(x, sh1, sc1, norm1_g, w_in_b, _rope_tables(n), n + nctx)
    kd, vt, kr, vr = _project_ctx(ctx, mod[b, 0][None], mod[b, 1][None], norm1_g, w_in_b, kd, vt, kr, vr)

    od = _diff_attention(lambda_q1, lambda_k1, lambda_q2, lambda_k2, diff_subln_g, qd, kd, vt)
    orr = _retention(ret_decay_fwd, ret_decay_bwd, ret_gn_g, qr, kr, vr, gr)

    return _out_mlp(x, od, orr, g1, sh2, sc2, g2, norm2_g, final_g[None],
                    w_out[0].astype(BF16), w_mlp1[0].astype(BF16), w_mlp2[0].astype(BF16))
```

```python
import functools
import math

import numpy as np
import jax
import jax.numpy as jnp
from jax import lax
from jax.experimental import pallas as pl
from jax.experimental.pallas import tpu as pltpu

F32 = jnp.float32
BF16 = jnp.bfloat16

GRID_W = 64
HEADS = 4
HEAD_DIM = 64
V_DIM = 128
VT_ROWS = V_DIM + 16
ROPE_PAIRS = 16
ROPE_BASE = 10000.0
N_MOD = 6
EPS = 1e-6
LAM_INIT = 0.8 - 0.6 * math.exp(-0.3 * 0)
LOG2E = 1.4426950408889634
QK_SCALE = HEAD_DIM ** -0.5

LANES = 128
VMEM_LIMIT_BYTES = 56 * 1024 * 1024

PROJ_TM = 512
ATTN_QB = 256
ATTN_GROUPS = 8
RET_C = 256
MLP_TM = 512
FF_CHUNK = 1024
ADA_TN = 1536

COL_QD, COL_KD, COL_VD = (0, 512), (512, 1024), (1024, 1536)
COL_QR, COL_KR, COL_VR, COL_GR = (1536, 1792), (1792, 2048), (2048, 2560), (2560, 3072)


def _cparams(sem, fuse_inputs=None):
    return pltpu.CompilerParams(dimension_semantics=sem, vmem_limit_bytes=VMEM_LIMIT_BYTES,
                                allow_input_fusion=fuse_inputs)


def _rmsnorm(xf, g):
    return xf * lax.rsqrt(jnp.mean(xf * xf, axis=-1, keepdims=True) + EPS) * g


def _silu(x):
    return x * (1.0 / (1.0 + jnp.exp(-x)))


def _adaln_kernel(c_ref, w_ref, b_ref, o_ref):
    a = _silu(c_ref[...]).astype(BF16)
    o_ref[...] = jnp.dot(a, w_ref[...].astype(BF16), preferred_element_type=F32) + b_ref[...]


def _adaln(c_rows, w_ada, b_ada):
    r, d = c_rows.shape
    n_out = w_ada.shape[1]
    return pl.pallas_call(
        _adaln_kernel,
        grid=(n_out // ADA_TN,),
        in_specs=[pl.BlockSpec((r, d), lambda j: (0, 0)),
                  pl.BlockSpec((d, ADA_TN), lambda j: (0, j)),
                  pl.BlockSpec((1, ADA_TN), lambda j: (0, j))],
        out_specs=pl.BlockSpec((r, ADA_TN), lambda j: (0, j)),
        out_shape=jax.ShapeDtypeStruct((r, n_out), F32),
        compiler_params=_cparams(("arbitrary",)),
        name="adaln",
    )(c_rows, w_ada, b_ada)


def _rope_tables(n_tokens):
    rows = n_tokens // GRID_W
    row = np.repeat(np.arange(rows), GRID_W).astype(np.float64)
    col = np.tile(np.arange(GRID_W), rows).astype(np.float64)
    inv = (np.float32(ROPE_BASE) ** (-np.arange(ROPE_PAIRS, dtype=np.float32) / ROPE_PAIRS)).astype(np.float64)
    ang_r = row[:, None] * inv
    ang_c = col[:, None] * inv
    zeros = np.zeros_like(ang_r)
    cos64 = np.concatenate([np.cos(ang_r)] * 2 + [np.cos(ang_c)] * 2, axis=1)
    sa64 = np.concatenate([-np.sin(ang_r), zeros, -np.sin(ang_c), zeros], axis=1)
    sb64 = np.concatenate([zeros, np.sin(ang_r), zeros, np.sin(ang_c)], axis=1)
    dup = lambda t: jnp.asarray(np.concatenate([t, t], axis=1), dtype=F32)
    return dup(cos64), dup(sa64), dup(sb64)


def _modulated(x_ref, sh_ref, sc_ref, g_ref):
    h = _rmsnorm(x_ref[...], g_ref[...])
    return (h * (1.0 + sc_ref[...]) + sh_ref[...]).astype(BF16)


def _proj_latent_kernel(x_ref, sh_ref, sc_ref, g_ref, w_ref, c_ref, sa_ref, sb_ref,
                        qd_ref, kd_ref, vt_ref, qr_ref, kr_ref, vr_ref, gr_ref):
    hb = _modulated(x_ref, sh_ref, sc_ref, g_ref)
    c, sa, sb = c_ref[...], sa_ref[...], sb_ref[...]

    def project(cols):
        return jnp.dot(hb, w_ref[:, cols[0]:cols[1]].astype(BF16), preferred_element_type=F32)

    def rotate(y):
        slabs = []
        for s in range(y.shape[1] // LANES):
            ys = y[:, s * LANES:(s + 1) * LANES]
            slabs.append(ys * c + pltpu.roll(ys, LANES - ROPE_PAIRS, 1) * sa
                         + pltpu.roll(ys, ROPE_PAIRS, 1) * sb)
        return jnp.concatenate(slabs, axis=1)

    qd_ref[...] = (rotate(project(COL_QD)) * (QK_SCALE * LOG2E)).astype(BF16)
    kd_ref[...] = rotate(project(COL_KD)).astype(BF16)
    vd = project(COL_VD)
    ones = jnp.ones((VT_ROWS - V_DIM, vd.shape[0]), BF16)
    for h in range(HEADS):
        vt_ref[h, 0:V_DIM, :] = vd[:, h * V_DIM:(h + 1) * V_DIM].T.astype(BF16)
        vt_ref[h, V_DIM:VT_ROWS, :] = ones
    qr_ref[...] = rotate(project(COL_QR)).astype(BF16)
    kr_ref[...] = (rotate(project(COL_KR)) * QK_SCALE).astype(BF16)
    vr_ref[...] = project(COL_VR).astype(BF16)
    gr_ref[...] = _silu(project(COL_GR)).astype(BF16)


def _project_latent(x, sh, sc, g, w_in, tables, n_keys):
    b, n, d = x.shape
    tm = PROJ_TM
    tok = lambda w: pl.BlockSpec((None, tm, w), lambda t, bi: (bi, t, 0))
    row = pl.BlockSpec((None, 1, d), lambda t, bi: (bi, 0, 0))
    table = pl.BlockSpec((tm, LANES), lambda t, bi: (t, 0))
    out_specs = [tok(512), tok(512),
                 pl.BlockSpec((None, HEADS, VT_ROWS, tm), lambda t, bi: (bi, 0, 0, t)),
                 tok(256), tok(256), tok(512), tok(512)]
    out_shape = [jax.ShapeDtypeStruct((b, n, 512), BF16),
                 jax.ShapeDtypeStruct((b, n_keys, 512), BF16),
                 jax.ShapeDtypeStruct((b, HEADS, VT_ROWS, n_keys), BF16),
                 jax.ShapeDtypeStruct((b, n, 256), BF16),
                 jax.ShapeDtypeStruct((b, n_keys, 256), BF16),
                 jax.ShapeDtypeStruct((b, n_keys, 512), BF16),
                 jax.ShapeDtypeStruct((b, n, 512), BF16)]
    return pl.pallas_call(
        _proj_latent_kernel,
        grid=(n // tm, b),
        in_specs=[tok(d), row, row, pl.BlockSpec((1, d), lambda t, bi: (0, 0)),
                  pl.BlockSpec(w_in.shape, lambda t, bi: (0, 0), pipeline_mode=pl.Buffered(1)),
                  table, table, table],
        out_specs=out_specs,
        out_shape=out_shape,
        compiler_params=_cparams(("arbitrary", "arbitrary")),
        name="proj_latent",
    )(x, sh, sc, g, w_in, *tables)


def _proj_ctx_kernel(x_ref, sh_ref, sc_ref, g_ref, w_ref, kd_in, vt_in, kr_in, vr_in,
                     kd_ref, vt_ref, kr_ref, vr_ref):
    del kd_in, vt_in, kr_in, vr_in
    hb = _modulated(x_ref, sh_ref, sc_ref, g_ref)
    b, nctx = kd_ref.shape[0], kd_ref.shape[1]

    def project(cols):
        return jnp.dot(hb, w_ref[:, cols[0]:cols[1]].astype(BF16), preferred_element_type=F32)

    kd, vd, kr, vr = project(COL_KD), project(COL_VD), project(COL_KR) * QK_SCALE, project(COL_VR)
    for bi in range(b):
        rows = slice(bi * nctx, (bi + 1) * nctx)
        kd_ref[bi] = kd[rows].astype(BF16)
        kr_ref[bi] = kr[rows].astype(BF16)
        vr_ref[bi] = vr[rows].astype(BF16)
        for h in range(HEADS):
            vt_ref[bi, h, 0:V_DIM, :] = vd[rows, h * V_DIM:(h + 1) * V_DIM].T.astype(BF16)
            vt_ref[bi, h, V_DIM:VT_ROWS, :] = jnp.ones((VT_ROWS - V_DIM, nctx), BF16)


def _project_ctx(ctx, sh, sc, g, w_in, kd, vt, kr, vr):
    b, nctx, d = ctx.shape
    n = kd.shape[1] - nctx
    assert n % nctx == 0
    blk = n // nctx
    full = lambda a: pl.BlockSpec(a.shape, lambda i: (0,) * a.ndim)
    anyspec = pl.BlockSpec(memory_space=pl.ANY)
    x2 = ctx.reshape(b * nctx, d)
    return pl.pallas_call(
        _proj_ctx_kernel,
        grid=(1,),
        in_specs=[full(x2), full(sh), full(sc), full(g), full(w_in)] + [anyspec] * 4,
        out_specs=[pl.BlockSpec((b, nctx, 512), lambda i: (0, blk, 0)),
                   pl.BlockSpec((b, HEADS, VT_ROWS, nctx), lambda i: (0, 0, 0, blk)),
                   pl.BlockSpec((b, nctx, 256), lambda i: (0, blk, 0)),
                   pl.BlockSpec((b, nctx, 512), lambda i: (0, blk, 0))],
        out_shape=[jax.ShapeDtypeStruct(a.shape, a.dtype) for a in (kd, vt, kr, vr)],
        input_output_aliases={5: 0, 6: 1, 7: 2, 8: 3},
        compiler_params=_cparams(("arbitrary",)),
        name="proj_ctx",
    )(x2, sh, sc, g, w_in, kd, vt, kr, vr)


def _attn_kernel(lq1_ref, lk1_ref, lq2_ref, lk2_ref, subg_ref, q_ref, k_ref, vt_ref, o_ref,
                 sa_scr, sb_scr):
    n = q_ref.shape[0]
    lam = (jnp.exp(jnp.sum(lq1_ref[...] * lk1_ref[...], axis=-1, keepdims=True))
           - jnp.exp(jnp.sum(lq2_ref[...] * lk2_ref[...], axis=-1, keepdims=True)) + LAM_INIT)
    lane = lax.broadcasted_iota(jnp.int32, (1, LANES), 1)
    first_map = lane < HEAD_DIM
    subg = subg_ref[...]
    nq = n // ATTN_QB

    n_keys = k_ref.shape[0]
    bounds = [round(g * n_keys / ATTN_GROUPS / 256) * 256 for g in range(ATTN_GROUPS + 1)]
    groups = list(zip(bounds[:-1], bounds[1:]))

    def scores(i, s_scr):
        r0 = pl.multiple_of(i * ATTN_QB, ATTN_QB)
        q = q_ref[pl.ds(r0, ATTN_QB), :]
        zero = jnp.zeros_like(q)
        qms = (jnp.where(first_map, q, zero), jnp.where(first_map, zero, q))
        ms = [None, None]
        for lo, hi in groups:
            for mp in range(2):
                s = lax.dot_general(k_ref[lo:hi, :], qms[mp], (((1,), (1,)), ((), ())),
                                    preferred_element_type=F32)
                s_scr[mp, lo:hi, :] = s
                mg = jnp.max(s, axis=0, keepdims=True)
                ms[mp] = mg if ms[mp] is None else jnp.maximum(ms[mp], mg)
            yield tuple(ms)

    def finish(i, s_scr, ms):
        accs = [None, None]
        for g, (lo, hi) in enumerate(groups):
            for mp in range(2):
                p = jnp.exp2(s_scr[mp, lo:hi, :] - ms[mp]).astype(BF16)
                part = jnp.dot(vt_ref[:, lo:hi], p, preferred_element_type=F32)
                accs[mp] = part if accs[mp] is None else accs[mp] + part
            if g == len(groups) - 1:
                ots = [a[0:V_DIM, :] / a[V_DIM:V_DIM + 1, :] for a in accs]
                o = (ots[0] - lam * ots[1]).T
                o = _rmsnorm(o, subg) * (1.0 - LAM_INIT)
                r0 = pl.multiple_of(i * ATTN_QB, ATTN_QB)
                o_ref[pl.ds(r0, ATTN_QB), :] = o.astype(o_ref.dtype)
            yield None

    def run(*stages):
        last = [None] * len(stages)
        for vals in zip(*stages):
            last = list(vals)
        return last

    def step(j, m_a):
        m_b, _ = run(scores(2 * j + 1, sb_scr), finish(2 * j, sa_scr, m_a))
        m_a, _ = run(scores(2 * j + 2, sa_scr), finish(2 * j + 1, sb_scr, m_b))
        return m_a

    (m_a,) = run(scores(0, sa_scr))
    m_a = lax.fori_loop(0, nq // 2 - 1, step, m_a)
    m_b, _ = run(scores(nq - 1, sb_scr), finish(nq - 2, sa_scr, m_a))
    run(finish(nq - 1, sb_scr, m_b))


def _diff_attention(lq1, lk1, lq2, lk2, subg, qd, kd, vt):
    b, n, _ = qd.shape
    n_keys = kd.shape[1]
    assert (n // ATTN_QB) % 2 == 0
    vec = lambda w: pl.BlockSpec((1, w), lambda bi, h: (0, 0))
    head = lambda rows: pl.BlockSpec((None, rows, LANES), lambda bi, h: (bi, 0, h))
    return pl.pallas_call(
        _attn_kernel,
        grid=(b, HEADS),
        in_specs=[vec(HEAD_DIM)] * 4 + [vec(V_DIM), head(n), head(n_keys),
                  pl.BlockSpec((None, None, VT_ROWS, n_keys), lambda bi, h: (bi, h, 0, 0))],
        out_specs=head(n),
        out_shape=jax.ShapeDtypeStruct((b, n, HEADS * V_DIM), BF16),
        scratch_shapes=[pltpu.VMEM((2, n_keys, ATTN_QB), F32)] * 2,
        compiler_params=_cparams(("arbitrary", "arbitrary")),
        name="diff_attn",
    )(lq1, lk1, lq2, lk2, subg, qd, kd, vt)


def _ret_kernel(decf_ref, decb_ref, gn_ref, q_ref, k_ref, v_ref, g_ref, o_ref, sb_scr):
    n = q_ref.shape[0]
    c = RET_C
    nc = n // c
    pair = pl.program_id(1)

    def log_gamma(dec):
        z = -dec
        return -(jnp.maximum(z, 0.0) + jnp.log(1.0 + jnp.exp(-jnp.abs(z))))

    lgf_all = log_gamma(decf_ref[...])
    lgb_all = log_gamma(decb_ref[...])
    hsel = lax.broadcasted_iota(jnp.int32, (1, HEADS), 1)

    def pick(vec, hh):
        return jnp.sum(jnp.where(hsel == 2 * pair + hh, vec, 0.0), axis=-1, keepdims=True)

    ii = lax.broadcasted_iota(jnp.int32, (c, c), 0).astype(F32)
    jj = lax.broadcasted_iota(jnp.int32, (c, c), 1).astype(F32)
    rel = ii - jj
    pos = lax.broadcasted_iota(jnp.int32, (c, LANES), 0).astype(F32)
    lane = lax.broadcasted_iota(jnp.int32, (1, LANES), 1)

    heads = []
    for hh in range(2):
        lgf, lgb = pick(lgf_all, hh), pick(lgb_all, hh)
        decay = (jnp.where(rel >= 0, jnp.exp(jnp.maximum(rel, 0.0) * lgf), 0.0)
                 + jnp.where(rel <= 0, jnp.exp(jnp.maximum(-rel, 0.0) * lgb), 0.0))
        heads.append(dict(
            decay=decay,
            qdec_f=jnp.exp((pos + 1.0) * lgf),
            qdec_b=jnp.exp((c - pos) * lgb),
            kdec_f=jnp.exp((c - 1.0 - pos) * lgf),
            kdec_b=jnp.exp(pos * lgb),
            cdec_f=jnp.exp(c * lgf),
            cdec_b=jnp.exp(c * lgb),
            qmask=(lane < HEAD_DIM) if hh == 0 else (lane >= HEAD_DIM),
            vlo=hh * V_DIM,
        ))

    def kv_state(k, v, kdec):
        kd = (k.astype(F32) * kdec).astype(BF16)
        return lax.dot_general(kd, v, (((0,), (0,)), ((), ())), preferred_element_type=F32)

    kc = k_ref[n:n + c, :]
    sf0, sb0 = [], []
    for hd in heads:
        vc = v_ref[n:n + c, hd["vlo"]:hd["vlo"] + V_DIM]
        sf0.append(kv_state(kc, vc, hd["kdec_f"]))
        sb0.append(kv_state(kc, vc, hd["kdec_b"]))

    def bwd(t, sb):
        ci = nc - 1 - t
        r0 = pl.multiple_of(ci * c, c)
        k = k_ref[pl.ds(r0, c), :]
        new = []
        for hh, hd in enumerate(heads):
            sb_scr[ci, hh] = sb[hh].astype(BF16)
            v = v_ref[pl.ds(r0, c), hd["vlo"]:hd["vlo"] + V_DIM]
            new.append(sb[hh] * hd["cdec_b"] + kv_state(k, v, hd["kdec_b"]))
        return tuple(new)

    lax.fori_loop(0, nc, bwd, tuple(sb0), unroll=4)

    gn = gn_ref[...]

    def fwd(ci, sf):
        r0 = pl.multiple_of(ci * c, c)
        q = q_ref[pl.ds(r0, c), :]
        k = k_ref[pl.ds(r0, c), :]
        zero = jnp.zeros_like(q)
        new = []
        for hh, hd in enumerate(heads):
            lo = hd["vlo"]
            v = v_ref[pl.ds(r0, c), lo:lo + V_DIM]
            qm = jnp.where(hd["qmask"], q, zero)
            scores = lax.dot_general(qm, k, (((1,), (1,)), ((), ())), preferred_element_type=F32)
            a = (scores * hd["decay"]).astype(BF16)
            o = jnp.dot(a, v, preferred_element_type=F32)
            o = o + jnp.dot(qm, sf[hh].astype(BF16), preferred_element_type=F32) * hd["qdec_f"]
            o = o + jnp.dot(qm, sb_scr[ci, hh], preferred_element_type=F32) * hd["qdec_b"]
            mu = jnp.mean(o, axis=-1, keepdims=True)
            var = jnp.mean(jnp.square(o - mu), axis=-1, keepdims=True)
            on = (o - mu) * lax.rsqrt(var + EPS) * gn[:, lo:lo + V_DIM]
            gate = g_ref[pl.ds(r0, c), lo:lo + V_DIM].astype(F32)
            o_ref[pl.ds(r0, c), lo:lo + V_DIM] = (on * gate).astype(o_ref.dtype)
            new.append(sf[hh] * hd["cdec_f"] + kv_state(k, v, hd["kdec_f"]))
        return tuple(new)

    lax.fori_loop(0, nc, fwd, tuple(sf0), unroll=4)


def _retention(dec_f, dec_b, gn_g, qr, kr, vr, gr):
    b, n, _ = qr.shape
    n_keys = kr.shape[1]
    assert n_keys - n == RET_C and n % (4 * RET_C) == 0
    pairs = HEADS // 2
    small = pl.BlockSpec((1, HEADS), lambda bi, p: (0, 0))
    qk = lambda rows: pl.BlockSpec((None, rows, LANES), lambda bi, p: (bi, 0, p))
    wide = lambda rows: pl.BlockSpec((None, rows, 2 * V_DIM), lambda bi, p: (bi, 0, p))
    return pl.pallas_call(
        _ret_kernel,
        grid=(b, pairs),
        in_specs=[small, small, pl.BlockSpec((1, 2 * V_DIM), lambda bi, p: (0, p)),
                  qk(n), qk(n_keys), wide(n_keys), wide(n)],
        out_specs=wide(n),
        out_shape=jax.ShapeDtypeStruct((b, n, HEADS * V_DIM), BF16),
        scratch_shapes=[pltpu.VMEM((n // RET_C, 2, LANES, V_DIM), BF16)],
        compiler_params=_cparams(("arbitrary", "arbitrary")),
        name="retention",
    )(dec_f, dec_b, gn_g, qr, kr, vr, gr)


def _mlp_kernel(x_ref, od_ref, or_ref, g1_ref, sh2_ref, sc2_ref, g2_ref, n2_ref, fg_ref,
                wo_ref, w1_ref, w2_ref, o_ref):
    half = od_ref.shape[1]
    y = (jnp.dot(od_ref[...], wo_ref[0:half, :].astype(BF16), preferred_element_type=F32)
         + jnp.dot(or_ref[...], wo_ref[half:2 * half, :].astype(BF16), preferred_element_type=F32))
    x1 = x_ref[...] + g1_ref[...] * y
    h = _rmsnorm(x1, n2_ref[...]) * (1.0 + sc2_ref[...]) + sh2_ref[...]
    hb = h.astype(BF16)
    d_ff = w1_ref.shape[1]
    acc = None
    for f0 in range(0, d_ff, FF_CHUNK):
        u = jnp.dot(hb, w1_ref[:, f0:f0 + FF_CHUNK].astype(BF16), preferred_element_type=F32)
        u = jnp.square(jnp.maximum(u, 0.0)).astype(BF16)
        part = jnp.dot(u, w2_ref[f0:f0 + FF_CHUNK, :].astype(BF16), preferred_element_type=F32)
        acc = part if acc is None else acc + part
    x2 = x1 + g2_ref[...] * acc
    o_ref[...] = _rmsnorm(x2, fg_ref[...])


def _out_mlp(x, od, orr, g1, sh2, sc2, g2, n2, fg, wo, w1, w2):
    b, n, d = x.shape
    tm = MLP_TM
    tok = lambda w: pl.BlockSpec((None, tm, w), lambda bi, t: (bi, t, 0))
    row = pl.BlockSpec((None, 1, d), lambda bi, t: (bi, 0, 0))
    gain = pl.BlockSpec((1, d), lambda bi, t: (0, 0))
    resident = lambda a: pl.BlockSpec(a.shape, lambda bi, t: (0, 0), pipeline_mode=pl.Buffered(1))
    return pl.pallas_call(
        _mlp_kernel,
        grid=(b, n // tm),
        in_specs=[tok(d), tok(od.shape[2]), tok(orr.shape[2]), row, row, row, row, gain, gain,
                  resident(wo), resident(w1), resident(w2)],
        out_specs=tok(d),
        out_shape=jax.ShapeDtypeStruct((b, n, d), F32),
        compiler_params=_cparams(("arbitrary", "arbitrary")),
        name="out_mlp",
    )(x, od, orr, g1, sh2, sc2, g2, n2, fg, wo, w1, w2)


def kernel(x, c, ctx, c_ctx, w_ada, b_ada, norm1_g, norm2_g, w_in, lambda_q1, lambda_k1, lambda_q2,
           lambda_k2, diff_subln_g, ret_decay_fwd, ret_decay_bwd, ret_gn_g, w_out, w_mlp1, w_mlp2, final_g):
    assert w_ada.shape[0] == 1, "single-layer block"
    b, n, d = x.shape
    nctx = ctx.shape[1]

    c_rows = jnp.concatenate([c, c_ctx[None], jnp.zeros((8 - b - 1, d), F32)], axis=0)
    mod = _adaln(c_rows, w_ada[0], b_ada).reshape(8, N_MOD, d)
    sh1, sc1, g1, sh2, sc2, g2 = [mod[:b, i][:, None, :] for i in range(N_MOD)]

    qd, kd, vt, qr, kr, vr, gr = _project_latent(x, sh1, sc1, norm1_g, w_in[0], _rope_tables(n), n + nctx)
    kd, vt, kr, vr = _project_ctx(ctx, mod[b, 0][None], mod[b, 1][None], norm1_g, w_in[0], kd, vt, kr, vr)

    od = _diff_attention(lambda_q1, lambda_k1, lambda_q2, lambda_k2, diff_subln_g, qd, kd, vt)
    orr = _retention(ret_decay_fwd, ret_decay_bwd, ret_gn_g, qr, kr, vr, gr)

    return _out_mlp(x, od, orr, g1, sh2, sc2, g2, norm2_g, final_g[None],
                    w_out[0], w_mlp1[0], w_mlp2[0])
```

```python
import functools
import math

import numpy as np
import jax
import jax.numpy as jnp
from jax import lax
from jax.experimental import pallas as pl
from jax.experimental.pallas import tpu as pltpu

F32 = jnp.float32
BF16 = jnp.bfloat16

GRID_W = 64
HEADS = 4
HEAD_DIM = 64
V_DIM = 128
VT_ROWS = V_DIM + 16
ROPE_PAIRS = 16
ROPE_BASE = 10000.0
N_MOD = 6
EPS = 1e-6
LAM_INIT = 0.8 - 0.6 * math.exp(-0.3 * 0)
LOG2E = 1.4426950408889634
QK_SCALE = HEAD_DIM ** -0.5

LANES = 128
MXU_DIM = 256
VMEM_LIMIT_BYTES = 56 * 1024 * 1024

PROJ_TM = 1024
ATTN_QB = 256
ATTN_GROUPS = 8
RET_C = 256
MLP_TM = 512
FF_CHUNK = 1024
ADA_TN = 1536

COL_QD, COL_KD, COL_VD = (0, 512), (512, 1024), (1024, 1536)
COL_QR, COL_KR, COL_VR, COL_GR = (1536, 1792), (1792, 2048), (2048, 2560), (2560, 3072)


def _cparams(sem, fuse_inputs=None):
    return pltpu.CompilerParams(dimension_semantics=sem, vmem_limit_bytes=VMEM_LIMIT_BYTES,
                                allow_input_fusion=fuse_inputs)


def _rmsnorm(xf, g):
    return xf * lax.rsqrt(jnp.mean(xf * xf, axis=-1, keepdims=True) + EPS) * g


def _silu(x):
    return x * (1.0 / (1.0 + jnp.exp(-x)))


def _adaln_kernel(c_ref, w_ref, b_ref, o_ref):
    a = _silu(c_ref[...]).astype(BF16)
    o_ref[...] = jnp.dot(a, w_ref[...].astype(BF16), preferred_element_type=F32) + b_ref[...]


def _adaln(c_rows, w_ada, b_ada):
    r, d = c_rows.shape
    n_out = w_ada.shape[1]
    return pl.pallas_call(
        _adaln_kernel,
        grid=(n_out // ADA_TN,),
        in_specs=[pl.BlockSpec((r, d), lambda j: (0, 0)),
                  pl.BlockSpec((d, ADA_TN), lambda j: (0, j)),
                  pl.BlockSpec((1, ADA_TN), lambda j: (0, j))],
        out_specs=pl.BlockSpec((r, ADA_TN), lambda j: (0, j)),
        out_shape=jax.ShapeDtypeStruct((r, n_out), F32),
        compiler_params=_cparams(("arbitrary",)),
        name="adaln",
    )(c_rows, w_ada, b_ada)


def _rope_tables(n_tokens):
    rows = n_tokens // GRID_W
    row = np.repeat(np.arange(rows), GRID_W).astype(np.float64)
    col = np.tile(np.arange(GRID_W), rows).astype(np.float64)
    inv = (np.float32(ROPE_BASE) ** (-np.arange(ROPE_PAIRS, dtype=np.float32) / ROPE_PAIRS)).astype(np.float64)
    ang_r = row[:, None] * inv
    ang_c = col[:, None] * inv
    zeros = np.zeros_like(ang_r)
    cos64 = np.concatenate([np.cos(ang_r)] * 2 + [np.cos(ang_c)] * 2, axis=1)
    sa64 = np.concatenate([-np.sin(ang_r), zeros, -np.sin(ang_c), zeros], axis=1)
    sb64 = np.concatenate([zeros, np.sin(ang_r), zeros, np.sin(ang_c)], axis=1)
    dup = lambda t: jnp.asarray(np.concatenate([t, t], axis=1), dtype=F32)
    return dup(cos64), dup(sa64), dup(sb64)


def _modulated(x_ref, sh_ref, sc_ref, g_ref):
    h = _rmsnorm(x_ref[...], g_ref[...])
    return (h * (1.0 + sc_ref[...]) + sh_ref[...]).astype(BF16)


def _proj_latent_kernel(x_ref, sh_ref, sc_ref, g_ref, w_ref, c_ref, sa_ref, sb_ref,
                        qd_ref, kd_ref, vt_ref, qr_ref, kr_ref, vr_ref, gr_ref):
    hb = _modulated(x_ref, sh_ref, sc_ref, g_ref)
    c, sa, sb = c_ref[...], sa_ref[...], sb_ref[...]

    def project(cols):
        return jnp.dot(hb, w_ref[:, cols[0]:cols[1]].astype(BF16), preferred_element_type=F32)

    def rotate(y):
        slabs = []
        for s in range(y.shape[1] // LANES):
            ys = y[:, s * LANES:(s + 1) * LANES]
            slabs.append(ys * c + pltpu.roll(ys, LANES - ROPE_PAIRS, 1) * sa
                         + pltpu.roll(ys, ROPE_PAIRS, 1) * sb)
        return jnp.concatenate(slabs, axis=1)

    qd_ref[...] = (rotate(project(COL_QD)) * (QK_SCALE * LOG2E)).astype(BF16)
    kd_ref[...] = rotate(project(COL_KD)).astype(BF16)
    vd = project(COL_VD)
    ones = jnp.ones((VT_ROWS - V_DIM, vd.shape[0]), BF16)
    for h in range(HEADS):
        vt_ref[h, 0:V_DIM, :] = vd[:, h * V_DIM:(h + 1) * V_DIM].T.astype(BF16)
        vt_ref[h, V_DIM:VT_ROWS, :] = ones
    qr_ref[...] = rotate(project(COL_QR)).astype(BF16)
    kr_ref[...] = (rotate(project(COL_KR)) * QK_SCALE).astype(BF16)
    vr_ref[...] = project(COL_VR).astype(BF16)
    gr_ref[...] = _silu(project(COL_GR)).astype(BF16)


def _project_latent(x, sh, sc, g, w_in, tables):
    b, n, d = x.shape
    tm = PROJ_TM
    tok = lambda w: pl.BlockSpec((None, tm, w), lambda t, bi: (bi, t, 0))
    row = pl.BlockSpec((None, 1, d), lambda t, bi: (bi, 0, 0))
    table = pl.BlockSpec((tm, LANES), lambda t, bi: (t, 0))
    out_specs = [tok(512), tok(512),
                 pl.BlockSpec((None, HEADS, VT_ROWS, tm), lambda t, bi: (bi, 0, 0, t)),
                 tok(256), tok(256), tok(512), tok(512)]
    out_shape = [jax.ShapeDtypeStruct((b, n, 512), BF16),
                 jax.ShapeDtypeStruct((b, n, 512), BF16),
                 jax.ShapeDtypeStruct((b, HEADS, VT_ROWS, n), BF16),
                 jax.ShapeDtypeStruct((b, n, 256), BF16),
                 jax.ShapeDtypeStruct((b, n, 256), BF16),
                 jax.ShapeDtypeStruct((b, n, 512), BF16),
                 jax.ShapeDtypeStruct((b, n, 512), BF16)]
    return pl.pallas_call(
        _proj_latent_kernel,
        grid=(n // tm, b),
        in_specs=[tok(d), row, row, pl.BlockSpec((1, d), lambda t, bi: (0, 0)),
                  pl.BlockSpec(w_in.shape, lambda t, bi: (0, 0), pipeline_mode=pl.Buffered(1)),
                  table, table, table],
        out_specs=out_specs,
        out_shape=out_shape,
        compiler_params=_cparams(("arbitrary", "arbitrary")),
        name="proj_latent",
    )(x, sh, sc, g, w_in, *tables)


def _proj_ctx_kernel(x_ref, sh_ref, sc_ref, g_ref, w_ref, kd_ref, vt_ref, kr_ref, vr_ref):
    hb = _modulated(x_ref, sh_ref, sc_ref, g_ref)
    b, nctx = kd_ref.shape[0], kd_ref.shape[1]

    def project(cols):
        return jnp.dot(hb, w_ref[:, cols[0]:cols[1]].astype(BF16), preferred_element_type=F32)

    kd, vd, kr, vr = project(COL_KD), project(COL_VD), project(COL_KR) * QK_SCALE, project(COL_VR)
    for bi in range(b):
        rows = slice(bi * nctx, (bi + 1) * nctx)
        kd_ref[bi] = kd[rows].astype(BF16)
        kr_ref[bi] = kr[rows].astype(BF16)
        vr_ref[bi] = vr[rows].astype(BF16)
        for h in range(HEADS):
            vt_ref[bi, h, 0:V_DIM, :] = vd[rows, h * V_DIM:(h + 1) * V_DIM].T.astype(BF16)
            vt_ref[bi, h, V_DIM:VT_ROWS, :] = jnp.ones((VT_ROWS - V_DIM, nctx), BF16)


def _project_ctx(ctx, sh, sc, g, w_in):
    b, nctx, d = ctx.shape
    full = lambda shape: pl.BlockSpec(shape, lambda i: (0,) * len(shape))
    x2 = ctx.reshape(b * nctx, d)
    shapes = [(b, nctx, 512), (b, HEADS, VT_ROWS, nctx), (b, nctx, 256), (b, nctx, 512)]
    return pl.pallas_call(
        _proj_ctx_kernel,
        grid=(1,),
        in_specs=[full(a.shape) for a in (x2, sh, sc, g, w_in)],
        out_specs=[full(shape) for shape in shapes],
        out_shape=[jax.ShapeDtypeStruct(shape, BF16) for shape in shapes],
        compiler_params=_cparams(("arbitrary",)),
        name="proj_ctx",
    )(x2, sh, sc, g, w_in)


def _attn_kernel(lq1_ref, lk1_ref, lq2_ref, lk2_ref, subg_ref, q_ref, k_ref, vt_ref, kc_ref, vtc_ref,
                 o_ref, sa_scr, sb_scr):
    n = q_ref.shape[0]
    lam = (jnp.exp(jnp.sum(lq1_ref[...] * lk1_ref[...], axis=-1, keepdims=True))
           - jnp.exp(jnp.sum(lq2_ref[...] * lk2_ref[...], axis=-1, keepdims=True)) + LAM_INIT)
    lane = lax.broadcasted_iota(jnp.int32, (1, LANES), 1)
    first_map = lane < HEAD_DIM
    subg = subg_ref[...]
    nq = n // ATTN_QB

    nctx = kc_ref.shape[0]
    gsz = n // ATTN_GROUPS
    groups = [[(k_ref, vt_ref, g * gsz, (g + 1) * gsz, nctx + g * gsz)] for g in range(ATTN_GROUPS)]
    groups[0].insert(0, (kc_ref, vtc_ref, 0, nctx, 0))

    def scores(i, s_scr):
        r0 = pl.multiple_of(i * ATTN_QB, ATTN_QB)
        q = q_ref[pl.ds(r0, ATTN_QB), :]
        zero = jnp.zeros_like(q)
        qms = (jnp.where(first_map, q, zero), jnp.where(first_map, zero, q))
        ms = [None, None]
        for pieces in groups:
            for kr, _, lo, hi, dst in pieces:
                for mp in range(2):
                    s = lax.dot_general(kr[lo:hi, :], qms[mp], (((1,), (1,)), ((), ())),
                                        preferred_element_type=F32)
                    s_scr[mp, dst:dst + hi - lo, :] = s
                    mg = jnp.max(s, axis=0, keepdims=True)
                    ms[mp] = mg if ms[mp] is None else jnp.maximum(ms[mp], mg)
            yield tuple(ms)

    def finish(i, s_scr, ms):
        accs = [None, None]
        for g, pieces in enumerate(groups):
            for _, vr, lo, hi, dst in pieces:
                for mp in range(2):
                    p = jnp.exp2(s_scr[mp, dst:dst + hi - lo, :] - ms[mp]).astype(BF16)
                    part = jnp.dot(vr[:, lo:hi], p, preferred_element_type=F32)
                    accs[mp] = part if accs[mp] is None else accs[mp] + part
            if g == len(groups) - 1:
                ots = [a[0:V_DIM, :] / a[V_DIM:V_DIM + 1, :] for a in accs]
                o = (ots[0] - lam * ots[1]).T
                o = _rmsnorm(o, subg) * (1.0 - LAM_INIT)
                r0 = pl.multiple_of(i * ATTN_QB, ATTN_QB)
                o_ref[pl.ds(r0, ATTN_QB), :] = o.astype(o_ref.dtype)
            yield None

    def run(*stages):
        last = [None] * len(stages)
        for vals in zip(*stages):
            last = list(vals)
        return last

    def step(j, m_a):
        m_b, _ = run(scores(2 * j + 1, sb_scr), finish(2 * j, sa_scr, m_a))
        m_a, _ = run(scores(2 * j + 2, sa_scr), finish(2 * j + 1, sb_scr, m_b))
        return m_a

    (m_a,) = run(scores(0, sa_scr))
    m_a = lax.fori_loop(0, nq // 2 - 1, step, m_a)
    m_b, _ = run(scores(nq - 1, sb_scr), finish(nq - 2, sa_scr, m_a))
    run(finish(nq - 1, sb_scr, m_b))


def _diff_attention(lq1, lk1, lq2, lk2, subg, qd, kd, vt, kd_c, vt_c):
    b, n, _ = qd.shape
    nctx = kd_c.shape[1]
    assert (n // ATTN_QB) % 2 == 0 and n % (ATTN_GROUPS * MXU_DIM) == 0 and nctx % MXU_DIM == 0
    vec = lambda w: pl.BlockSpec((1, w), lambda bi, h: (0, 0))
    head = lambda rows: pl.BlockSpec((None, rows, LANES), lambda bi, h: (bi, 0, h))
    headt = lambda cols: pl.BlockSpec((None, None, VT_ROWS, cols), lambda bi, h: (bi, h, 0, 0))
    return pl.pallas_call(
        _attn_kernel,
        grid=(b, HEADS),
        in_specs=[vec(HEAD_DIM)] * 4 + [vec(V_DIM), head(n), head(n), headt(n), head(nctx), headt(nctx)],
        out_specs=head(n),
        out_shape=jax.ShapeDtypeStruct((b, n, HEADS * V_DIM), BF16),
        scratch_shapes=[pltpu.VMEM((2, n + nctx, ATTN_QB), F32)] * 2,
        compiler_params=_cparams(("arbitrary", "arbitrary")),
        name="diff_attn",
    )(lq1, lk1, lq2, lk2, subg, qd, kd, vt, kd_c, vt_c)


def _ret_kernel(decf_ref, decb_ref, gn_ref, q_ref, k_ref, v_ref, g_ref, kc_ref, vc_ref, o_ref, sb_scr):
    n = q_ref.shape[0]
    c = RET_C
    nc = n // c
    pair = pl.program_id(1)

    def log_gamma(dec):
        z = -dec
        return -(jnp.maximum(z, 0.0) + jnp.log(1.0 + jnp.exp(-jnp.abs(z))))

    lgf_all = log_gamma(decf_ref[...])
    lgb_all = log_gamma(decb_ref[...])
    hsel = lax.broadcasted_iota(jnp.int32, (1, HEADS), 1)

    def pick(vec, hh):
        return jnp.sum(jnp.where(hsel == 2 * pair + hh, vec, 0.0), axis=-1, keepdims=True)

    ii = lax.broadcasted_iota(jnp.int32, (c, c), 0).astype(F32)
    jj = lax.broadcasted_iota(jnp.int32, (c, c), 1).astype(F32)
    rel = ii - jj
    pos = lax.broadcasted_iota(jnp.int32, (c, LANES), 0).astype(F32)
    lane = lax.broadcasted_iota(jnp.int32, (1, LANES), 1)

    heads = []
    for hh in range(2):
        lgf, lgb = pick(lgf_all, hh), pick(lgb_all, hh)
        decay = (jnp.where(rel >= 0, jnp.exp(jnp.maximum(rel, 0.0) * lgf), 0.0)
                 + jnp.where(rel <= 0, jnp.exp(jnp.maximum(-rel, 0.0) * lgb), 0.0))
        heads.append(dict(
            decay=decay,
            qdec_f=jnp.exp((pos + 1.0) * lgf),
            qdec_b=jnp.exp((c - pos) * lgb),
            kdec_f=jnp.exp((c - 1.0 - pos) * lgf),
            kdec_b=jnp.exp(pos * lgb),
            cdec_f=jnp.exp(c * lgf),
            cdec_b=jnp.exp(c * lgb),
            qmask=(lane < HEAD_DIM) if hh == 0 else (lane >= HEAD_DIM),
            vlo=hh * V_DIM,
        ))

    def kv_state(k, v, kdec):
        kd = (k.astype(F32) * kdec).astype(BF16)
        return lax.dot_general(kd, v, (((0,), (0,)), ((), ())), preferred_element_type=F32)

    kc = kc_ref[...]
    sf0, sb0 = [], []
    for hd in heads:
        vc = vc_ref[:, hd["vlo"]:hd["vlo"] + V_DIM]
        sf0.append(kv_state(kc, vc, hd["kdec_f"]))
        sb0.append(kv_state(kc, vc, hd["kdec_b"]))

    def bwd(t, sb):
        ci = nc - 1 - t
        r0 = pl.multiple_of(ci * c, c)
        k = k_ref[pl.ds(r0, c), :]
        new = []
        for hh, hd in enumerate(heads):
            sb_scr[ci, hh] = sb[hh].astype(BF16)
            v = v_ref[pl.ds(r0, c), hd["vlo"]:hd["vlo"] + V_DIM]
            new.append(sb[hh] * hd["cdec_b"] + kv_state(k, v, hd["kdec_b"]))
        return tuple(new)

    lax.fori_loop(0, nc, bwd, tuple(sb0), unroll=4)

    gn = gn_ref[...]

    def fwd(ci, sf):
        r0 = pl.multiple_of(ci * c, c)
        q = q_ref[pl.ds(r0, c), :]
        k = k_ref[pl.ds(r0, c), :]
        zero = jnp.zeros_like(q)
        new = []
        for hh, hd in enumerate(heads):
            lo = hd["vlo"]
            v = v_ref[pl.ds(r0, c), lo:lo + V_DIM]
            qm = jnp.where(hd["qmask"], q, zero)
            scores = lax.dot_general(qm, k, (((1,), (1,)), ((), ())), preferred_element_type=F32)
            a = (scores * hd["decay"]).astype(BF16)
            o = jnp.dot(a, v, preferred_element_type=F32)
            o = o + jnp.dot(qm, sf[hh].astype(BF16), preferred_element_type=F32) * hd["qdec_f"]
            o = o + jnp.dot(qm, sb_scr[ci, hh], preferred_element_type=F32) * hd["qdec_b"]
            mu = jnp.mean(o, axis=-1, keepdims=True)
            var = jnp.mean(jnp.square(o - mu), axis=-1, keepdims=True)
            on = (o - mu) * lax.rsqrt(var + EPS) * gn[:, lo:lo + V_DIM]
            gate = g_ref[pl.ds(r0, c), lo:lo + V_DIM].astype(F32)
            o_ref[pl.ds(r0, c), lo:lo + V_DIM] = (on * gate).astype(o_ref.dtype)
            new.append(sf[hh] * hd["cdec_f"] + kv_state(k, v, hd["kdec_f"]))
        return tuple(new)

    lax.fori_loop(0, nc, fwd, tuple(sf0), unroll=4)


def _retention(dec_f, dec_b, gn_g, qr, kr, vr, gr, kr_c, vr_c):
    b, n, _ = qr.shape
    nctx = kr_c.shape[1]
    assert nctx == RET_C and n % (4 * RET_C) == 0
    pairs = HEADS // 2
    small = pl.BlockSpec((1, HEADS), lambda bi, p: (0, 0))
    qk = lambda rows: pl.BlockSpec((None, rows, LANES), lambda bi, p: (bi, 0, p))
    wide = lambda rows: pl.BlockSpec((None, rows, 2 * V_DIM), lambda bi, p: (bi, 0, p))
    return pl.pallas_call(
        _ret_kernel,
        grid=(b, pairs),
        in_specs=[small, small, pl.BlockSpec((1, 2 * V_DIM), lambda bi, p: (0, p)),
                  qk(n), qk(n), wide(n), wide(n), qk(nctx), wide(nctx)],
        out_specs=wide(n),
        out_shape=jax.ShapeDtypeStruct((b, n, HEADS * V_DIM), BF16),
        scratch_shapes=[pltpu.VMEM((n // RET_C, 2, LANES, V_DIM), BF16)],
        compiler_params=_cparams(("arbitrary", "arbitrary")),
        name="retention",
    )(dec_f, dec_b, gn_g, qr, kr, vr, gr, kr_c, vr_c)


def _mlp_kernel(x_ref, od_ref, or_ref, g1_ref, sh2_ref, sc2_ref, g2_ref, n2_ref, fg_ref,
                wo_ref, w1_ref, w2_ref, o_ref):
    half = od_ref.shape[1]
    y = (jnp.dot(od_ref[...], wo_ref[0:half, :].astype(BF16), preferred_element_type=F32)
         + jnp.dot(or_ref[...], wo_ref[half:2 * half, :].astype(BF16), preferred_element_type=F32))
    x1 = x_ref[...] + g1_ref[...] * y
    h = _rmsnorm(x1, n2_ref[...]) * (1.0 + sc2_ref[...]) + sh2_ref[...]
    hb = h.astype(BF16)
    d_ff = w1_ref.shape[1]
    acc = None
    for f0 in range(0, d_ff, FF_CHUNK):
        u = jnp.dot(hb, w1_ref[:, f0:f0 + FF_CHUNK].astype(BF16), preferred_element_type=F32)
        u = jnp.square(jnp.maximum(u, 0.0)).astype(BF16)
        part = jnp.dot(u, w2_ref[f0:f0 + FF_CHUNK, :].astype(BF16), preferred_element_type=F32)
        acc = part if acc is None else acc + part
    x2 = x1 + g2_ref[...] * acc
    o_ref[...] = _rmsnorm(x2, fg_ref[...])


def _out_mlp(x, od, orr, g1, sh2, sc2, g2, n2, fg, wo, w1, w2):
    b, n, d = x.shape
    tm = MLP_TM
    tok = lambda w: pl.BlockSpec((None, tm, w), lambda bi, t: (bi, t, 0))
    row = pl.BlockSpec((None, 1, d), lambda bi, t: (bi, 0, 0))
    gain = pl.BlockSpec((1, d), lambda bi, t: (0, 0))
    resident = lambda a: pl.BlockSpec(a.shape, lambda bi, t: (0, 0), pipeline_mode=pl.Buffered(1))
    return pl.pallas_call(
        _mlp_kernel,
        grid=(b, n // tm),
        in_specs=[tok(d), tok(od.shape[2]), tok(orr.shape[2]), row, row, row, row, gain, gain,
                  resident(wo), resident(w1), resident(w2)],
        out_specs=tok(d),
        out_shape=jax.ShapeDtypeStruct((b, n, d), F32),
        compiler_params=_cparams(("arbitrary", "arbitrary")),
        name="out_mlp",
    )(x, od, orr, g1, sh2, sc2, g2, n2, fg, wo, w1, w2)


def kernel(x, c, ctx, c_ctx, w_ada, b_ada, norm1_g, norm2_g, w_in, lambda_q1, lambda_k1, lambda_q2,
           lambda_k2, diff_subln_g, ret_decay_fwd, ret_decay_bwd, ret_gn_g, w_out, w_mlp1, w_mlp2, final_g):
    assert w_ada.shape[0] == 1, "single-layer block"
    b, n, d = x.shape
    nctx = ctx.shape[1]

    c_rows = jnp.concatenate([c, c_ctx[None], jnp.zeros((8 - b - 1, d), F32)], axis=0)
    mod = _adaln(c_rows, w_ada[0], b_ada).reshape(8, N_MOD, d)
    sh1, sc1, g1, sh2, sc2, g2 = [mod[:b, i][:, None, :] for i in range(N_MOD)]

    qd, kd, vt, qr, kr, vr, gr = _project_latent(x, sh1, sc1, norm1_g, w_in[0], _rope_tables(n))
    kd_c, vt_c, kr_c, vr_c = _project_ctx(ctx, mod[b, 0][None], mod[b, 1][None], norm1_g, w_in[0])

    od = _diff_attention(lambda_q1, lambda_k1, lambda_q2, lambda_k2, diff_subln_g, qd, kd, vt, kd_c, vt_c)
    orr = _retention(ret_decay_fwd, ret_decay_bwd, ret_gn_g, qr, kr, vr, gr, kr_c, vr_c)

    return _out_mlp(x, od, orr, g1, sh2, sc2, g2, norm2_g, final_g[None],
                    w_out[0], w_mlp1[0], w_mlp2[0])
```

```python
import math

import numpy as np
import jax
import jax.numpy as jnp
from jax import lax
from jax.experimental import pallas as pl
from jax.experimental.pallas import tpu as pltpu

F32 = jnp.float32
BF16 = jnp.bfloat16

GRID_W = 64
HEADS = 4
HEAD_DIM = 64
V_DIM = 128
VT_ROWS = V_DIM + 16
ROPE_PAIRS = 16
ROPE_BASE = 10000.0
N_MOD = 6
EPS = 1e-6
LAM_INIT = 0.8 - 0.6 * math.exp(-0.3 * 0)
LOG2E = 1.4426950408889634
QK_SCALE = HEAD_DIM ** -0.5

LANES = 128
MXU_DIM = 256
VMEM_LIMIT_BYTES = 56 * 1024 * 1024

PROJ_TM = 1024
ATTN_QB = 256
ATTN_GROUPS = 8
RET_C = 256
MLP_TM = 512
FF_CHUNK = 1024
ADA_TN = 1536

COL_QD, COL_KD, COL_VD = (0, 512), (512, 1024), (1024, 1536)
COL_QR, COL_KR, COL_VR, COL_GR = (1536, 1792), (1792, 2048), (2048, 2560), (2560, 3072)


def _cparams(sem):
    return pltpu.CompilerParams(dimension_semantics=sem, vmem_limit_bytes=VMEM_LIMIT_BYTES)


def _rmsnorm(xf, g):
    return xf * lax.rsqrt(jnp.mean(xf * xf, axis=-1, keepdims=True) + EPS) * g


def _silu(x):
    return x * (1.0 / (1.0 + jnp.exp(-x)))


def _adaln_kernel(c_ref, w_ref, b_ref, o_ref):
    a = _silu(c_ref[...]).astype(BF16)
    o_ref[...] = jnp.dot(a, w_ref[...].astype(BF16), preferred_element_type=F32) + b_ref[...]


def _adaln(c_rows, w_ada, b_ada):
    r, d = c_rows.shape
    n_out = w_ada.shape[1]
    return pl.pallas_call(
        _adaln_kernel,
        grid=(n_out // ADA_TN,),
        in_specs=[pl.BlockSpec((r, d), lambda j: (0, 0)),
                  pl.BlockSpec((d, ADA_TN), lambda j: (0, j)),
                  pl.BlockSpec((1, ADA_TN), lambda j: (0, j))],
        out_specs=pl.BlockSpec((r, ADA_TN), lambda j: (0, j)),
        out_shape=jax.ShapeDtypeStruct((r, n_out), F32),
        compiler_params=_cparams(("arbitrary",)),
        name="adaln",
    )(c_rows, w_ada, b_ada)


def _rope_tables(n_tokens):
    rows = n_tokens // GRID_W
    row = np.repeat(np.arange(rows), GRID_W).astype(np.float64)
    col = np.tile(np.arange(GRID_W), rows).astype(np.float64)
    inv = (np.float32(ROPE_BASE) ** (-np.arange(ROPE_PAIRS, dtype=np.float32) / ROPE_PAIRS)).astype(np.float64)
    ang_r = row[:, None] * inv
    ang_c = col[:, None] * inv
    zeros = np.zeros_like(ang_r)
    cos64 = np.concatenate([np.cos(ang_r)] * 2 + [np.cos(ang_c)] * 2, axis=1)
    sa64 = np.concatenate([-np.sin(ang_r), zeros, -np.sin(ang_c), zeros], axis=1)
    sb64 = np.concatenate([zeros, np.sin(ang_r), zeros, np.sin(ang_c)], axis=1)
    dup = lambda t: jnp.asarray(np.concatenate([t, t], axis=1), dtype=F32)
    return dup(cos64), dup(sa64), dup(sb64)


def _modulated(x_ref, sh_ref, sc_ref, g_ref):
    h = _rmsnorm(x_ref[...], g_ref[...])
    return (h * (1.0 + sc_ref[...]) + sh_ref[...]).astype(BF16)


def _proj_latent_kernel(x_ref, sh_ref, sc_ref, g_ref, w_ref, c_ref, sa_ref, sb_ref,
                        qd_ref, kd_ref, vt_ref, qr_ref, kr_ref, vr_ref, gr_ref):
    hb = _modulated(x_ref, sh_ref, sc_ref, g_ref)
    c, sa, sb = c_ref[...], sa_ref[...], sb_ref[...]

    def project(cols):
        return jnp.dot(hb, w_ref[:, cols[0]:cols[1]].astype(BF16), preferred_element_type=F32)

    def rotate(y):
        slabs = []
        for s in range(y.shape[1] // LANES):
            ys = y[:, s * LANES:(s + 1) * LANES]
            slabs.append(ys * c + pltpu.roll(ys, LANES - ROPE_PAIRS, 1) * sa
                         + pltpu.roll(ys, ROPE_PAIRS, 1) * sb)
        return jnp.concatenate(slabs, axis=1)

    qd_ref[...] = (rotate(project(COL_QD)) * (QK_SCALE * LOG2E)).astype(BF16)
    kd_ref[...] = rotate(project(COL_KD)).astype(BF16)
    vd = project(COL_VD)
    ones = jnp.ones((VT_ROWS - V_DIM, vd.shape[0]), BF16)
    for h in range(HEADS):
        vt_ref[h, 0:V_DIM, :] = vd[:, h * V_DIM:(h + 1) * V_DIM].T.astype(BF16)
        vt_ref[h, V_DIM:VT_ROWS, :] = ones
    qr_ref[...] = rotate(project(COL_QR)).astype(BF16)
    kr_ref[...] = (rotate(project(COL_KR)) * QK_SCALE).astype(BF16)
    vr_ref[...] = project(COL_VR).astype(BF16)
    gr_ref[...] = _silu(project(COL_GR)).astype(BF16)


def _project_latent(x, sh, sc, g, w_in, tables):
    b, n, d = x.shape
    tm = PROJ_TM
    tok = lambda w: pl.BlockSpec((None, tm, w), lambda t, bi: (bi, t, 0))
    row = pl.BlockSpec((None, 1, d), lambda t, bi: (bi, 0, 0))
    table = pl.BlockSpec((tm, LANES), lambda t, bi: (t, 0))
    out_specs = [tok(512), tok(512),
                 pl.BlockSpec((None, HEADS, VT_ROWS, tm), lambda t, bi: (bi, 0, 0, t)),
                 tok(256), tok(256), tok(512), tok(512)]
    out_shape = [jax.ShapeDtypeStruct((b, n, 512), BF16),
                 jax.ShapeDtypeStruct((b, n, 512), BF16),
                 jax.ShapeDtypeStruct((b, HEADS, VT_ROWS, n), BF16),
                 jax.ShapeDtypeStruct((b, n, 256), BF16),
                 jax.ShapeDtypeStruct((b, n, 256), BF16),
                 jax.ShapeDtypeStruct((b, n, 512), BF16),
                 jax.ShapeDtypeStruct((b, n, 512), BF16)]
    return pl.pallas_call(
        _proj_latent_kernel,
        grid=(n // tm, b),
        in_specs=[tok(d), row, row, pl.BlockSpec((1, d), lambda t, bi: (0, 0)),
                  pl.BlockSpec(w_in.shape, lambda t, bi: (0, 0), pipeline_mode=pl.Buffered(1)),
                  table, table, table],
        out_specs=out_specs,
        out_shape=out_shape,
        compiler_params=_cparams(("arbitrary", "arbitrary")),
        name="proj_latent",
    )(x, sh, sc, g, w_in, *tables)


def _proj_ctx_kernel(x_ref, sh_ref, sc_ref, g_ref, w_ref, kd_ref, vt_ref, kr_ref, vr_ref):
    hb = _modulated(x_ref, sh_ref, sc_ref, g_ref)
    b, nctx = kd_ref.shape[0], kd_ref.shape[1]

    def project(cols):
        return jnp.dot(hb, w_ref[:, cols[0]:cols[1]].astype(BF16), preferred_element_type=F32)

    kd, vd, kr, vr = project(COL_KD), project(COL_VD), project(COL_KR) * QK_SCALE, project(COL_VR)
    for bi in range(b):
        rows = slice(bi * nctx, (bi + 1) * nctx)
        kd_ref[bi] = kd[rows].astype(BF16)
        kr_ref[bi] = kr[rows].astype(BF16)
        vr_ref[bi] = vr[rows].astype(BF16)
        for h in range(HEADS):
            vt_ref[bi, h, 0:V_DIM, :] = vd[rows, h * V_DIM:(h + 1) * V_DIM].T.astype(BF16)
            vt_ref[bi, h, V_DIM:VT_ROWS, :] = jnp.ones((VT_ROWS - V_DIM, nctx), BF16)


def _project_ctx(ctx, sh, sc, g, w_in):
    b, nctx, d = ctx.shape
    full = lambda shape: pl.BlockSpec(shape, lambda i: (0,) * len(shape))
    x2 = ctx.reshape(b * nctx, d)
    shapes = [(b, nctx, 512), (b, HEADS, VT_ROWS, nctx), (b, nctx, 256), (b, nctx, 512)]
    return pl.pallas_call(
        _proj_ctx_kernel,
        grid=(1,),
        in_specs=[full(a.shape) for a in (x2, sh, sc, g, w_in)],
        out_specs=[full(shape) for shape in shapes],
        out_shape=[jax.ShapeDtypeStruct(shape, BF16) for shape in shapes],
        compiler_params=_cparams(("arbitrary",)),
        name="proj_ctx",
    )(x2, sh, sc, g, w_in)


def _mixer_kernel(lq1_ref, lk1_ref, lq2_ref, lk2_ref, subg_ref, q_ref, k_ref, vt_ref, kc_ref, vtc_ref,
                  decf_ref, decb_ref, gn_ref, rq_ref, rk_ref, rv_ref, rg_ref, rkc_ref, rvc_ref,
                  o_ref, ro_ref, sa_scr, sb_scr, rsb_scr, rsf_scr):
    n = q_ref.shape[0]
    head = pl.program_id(1)
    lane = lax.broadcasted_iota(jnp.int32, (1, LANES), 1)

    lam = (jnp.exp(jnp.sum(lq1_ref[...] * lk1_ref[...], axis=-1, keepdims=True))
           - jnp.exp(jnp.sum(lq2_ref[...] * lk2_ref[...], axis=-1, keepdims=True)) + LAM_INIT)
    first_map = lane < HEAD_DIM
    subg = subg_ref[...]
    nq = n // ATTN_QB

    nctx = kc_ref.shape[0]
    gsz = n // ATTN_GROUPS
    groups = [[(k_ref, vt_ref, g * gsz, (g + 1) * gsz, nctx + g * gsz)] for g in range(ATTN_GROUPS)]
    groups[0].insert(0, (kc_ref, vtc_ref, 0, nctx, 0))

    def scores(i, s_scr):
        r0 = pl.multiple_of(i * ATTN_QB, ATTN_QB)
        q = q_ref[pl.ds(r0, ATTN_QB), :]
        zero = jnp.zeros_like(q)
        qms = (jnp.where(first_map, q, zero), jnp.where(first_map, zero, q))
        ms = [None, None]
        for pieces in groups:
            for kr, _, lo, hi, dst in pieces:
                for mp in range(2):
                    s = lax.dot_general(kr[lo:hi, :], qms[mp], (((1,), (1,)), ((), ())),
                                        preferred_element_type=F32)
                    s_scr[mp, dst:dst + hi - lo, :] = s
                    mg = jnp.max(s, axis=0, keepdims=True)
                    ms[mp] = mg if ms[mp] is None else jnp.maximum(ms[mp], mg)
            yield tuple(ms)

    def finish(i, s_scr, ms):
        accs = [None, None]
        for g, pieces in enumerate(groups):
            for _, vr, lo, hi, dst in pieces:
                for mp in range(2):
                    p = jnp.exp2(s_scr[mp, dst:dst + hi - lo, :] - ms[mp]).astype(BF16)
                    part = jnp.dot(vr[:, lo:hi], p, preferred_element_type=F32)
                    accs[mp] = part if accs[mp] is None else accs[mp] + part
            if g == len(groups) - 1:
                ots = [a[0:V_DIM, :] / a[V_DIM:V_DIM + 1, :] for a in accs]
                o = (ots[0] - lam * ots[1]).T
                o = _rmsnorm(o, subg) * (1.0 - LAM_INIT)
                r0 = pl.multiple_of(i * ATTN_QB, ATTN_QB)
                o_ref[pl.ds(r0, ATTN_QB), :] = o.astype(o_ref.dtype)
            yield None

    c = RET_C
    nc = n // c

    def log_gamma(dec):
        z = -dec
        return -(jnp.maximum(z, 0.0) + jnp.log(1.0 + jnp.exp(-jnp.abs(z))))

    hsel = lax.broadcasted_iota(jnp.int32, (1, HEADS), 1) == head

    def pick(vec):
        return jnp.sum(jnp.where(hsel, vec, 0.0), axis=-1, keepdims=True)

    lgf, lgb = pick(log_gamma(decf_ref[...])), pick(log_gamma(decb_ref[...]))
    ii = lax.broadcasted_iota(jnp.int32, (c, c), 0).astype(F32)
    jj = lax.broadcasted_iota(jnp.int32, (c, c), 1).astype(F32)
    rel = ii - jj
    pos = lax.broadcasted_iota(jnp.int32, (c, LANES), 0).astype(F32)
    decay = (jnp.where(rel >= 0, jnp.exp(jnp.maximum(rel, 0.0) * lgf), 0.0)
             + jnp.where(rel <= 0, jnp.exp(jnp.maximum(-rel, 0.0) * lgb), 0.0))
    qdec_f, qdec_b = jnp.exp((pos + 1.0) * lgf), jnp.exp((c - pos) * lgb)
    kdec_f, kdec_b = jnp.exp((c - 1.0 - pos) * lgf), jnp.exp(pos * lgb)
    cdec_f, cdec_b = jnp.exp(c * lgf), jnp.exp(c * lgb)
    rmask = (lane >= HEAD_DIM).astype(jnp.int32) == head % 2
    gn = gn_ref[...]

    def kv_state(k, v, kdec):
        kd = (k.astype(F32) * kdec).astype(BF16)
        return lax.dot_general(kd, v, (((0,), (0,)), ((), ())), preferred_element_type=F32)

    rsf_scr[...] = kv_state(rkc_ref[...], rvc_ref[...], kdec_f)

    def bwd(t, sb):
        ci = nc - 1 - t
        r0 = pl.multiple_of(ci * c, c)
        rsb_scr[ci] = sb.astype(BF16)
        return sb * cdec_b + kv_state(rk_ref[pl.ds(r0, c), :], rv_ref[pl.ds(r0, c), :], kdec_b)

    lax.fori_loop(0, nc, bwd, kv_state(rkc_ref[...], rvc_ref[...], kdec_b), unroll=4)

    def ret_chunk(ci):
        r0 = pl.multiple_of(ci * c, c)
        q = rq_ref[pl.ds(r0, c), :]
        k = rk_ref[pl.ds(r0, c), :]
        v = rv_ref[pl.ds(r0, c), :]
        qm = jnp.where(rmask, q, jnp.zeros_like(q))
        sc = lax.dot_general(qm, k, (((1,), (1,)), ((), ())), preferred_element_type=F32)
        a = (sc * decay).astype(BF16)
        yield None
        sf = rsf_scr[...]
        o = jnp.dot(a, v, preferred_element_type=F32)
        o = o + jnp.dot(qm, sf.astype(BF16), preferred_element_type=F32) * qdec_f
        o = o + jnp.dot(qm, rsb_scr[ci], preferred_element_type=F32) * qdec_b
        yield None
        mu = jnp.mean(o, axis=-1, keepdims=True)
        var = jnp.mean(jnp.square(o - mu), axis=-1, keepdims=True)
        on = (o - mu) * lax.rsqrt(var + EPS) * gn
        gate = rg_ref[pl.ds(r0, c), :].astype(F32)
        ro_ref[pl.ds(r0, c), :] = (on * gate).astype(ro_ref.dtype)
        yield None
        rsf_scr[...] = sf * cdec_f + kv_state(k, v, kdec_f)
        yield None

    def run(*stages, side=None):
        last = [None] * len(stages)
        for t, vals in enumerate(zip(*stages)):
            last = list(vals)
            if side is not None and t % 2 == 0:
                next(side)
        return last

    def step(j, m_a):
        m_b, _ = run(scores(2 * j + 1, sb_scr), finish(2 * j, sa_scr, m_a), side=ret_chunk(2 * j))
        m_a, _ = run(scores(2 * j + 2, sa_scr), finish(2 * j + 1, sb_scr, m_b), side=ret_chunk(2 * j + 1))
        return m_a

    (m_a,) = run(scores(0, sa_scr))
    m_a = lax.fori_loop(0, nq // 2 - 1, step, m_a)
    m_b, _ = run(scores(nq - 1, sb_scr), finish(nq - 2, sa_scr, m_a), side=ret_chunk(nq - 2))
    run(finish(nq - 1, sb_scr, m_b), side=ret_chunk(nq - 1))


def _token_mixers(lq1, lk1, lq2, lk2, subg, qd, kd, vt, kd_c, vt_c,
                  dec_f, dec_b, gn_g, qr, kr, vr, gr, kr_c, vr_c):
    b, n, _ = qd.shape
    nctx = kd_c.shape[1]
    assert (n // ATTN_QB) % 2 == 0 and n % (ATTN_GROUPS * MXU_DIM) == 0 and nctx % MXU_DIM == 0
    assert ATTN_GROUPS % 8 == 0 and nctx == RET_C and n // RET_C == n // ATTN_QB
    vec = lambda w: pl.BlockSpec((1, w), lambda bi, h: (0, 0))
    head = lambda rows: pl.BlockSpec((None, rows, LANES), lambda bi, h: (bi, 0, h))
    pair = lambda rows: pl.BlockSpec((None, rows, LANES), lambda bi, h: (bi, 0, h // 2))
    headt = lambda cols: pl.BlockSpec((None, None, VT_ROWS, cols), lambda bi, h: (bi, h, 0, 0))
    out = jax.ShapeDtypeStruct((b, n, HEADS * V_DIM), BF16)
    return pl.pallas_call(
        _mixer_kernel,
        grid=(b, HEADS),
        in_specs=[vec(HEAD_DIM)] * 4 + [vec(V_DIM), head(n), head(n), headt(n), head(nctx), headt(nctx),
                  vec(HEADS), vec(HEADS), pl.BlockSpec((1, V_DIM), lambda bi, h: (0, h)),
                  pair(n), pair(n), head(n), head(n), pair(nctx), head(nctx)],
        out_specs=[head(n), head(n)],
        out_shape=[out, out],
        scratch_shapes=[pltpu.VMEM((2, n + nctx, ATTN_QB), F32)] * 2
        + [pltpu.VMEM((n // RET_C, LANES, V_DIM), BF16), pltpu.VMEM((LANES, V_DIM), F32)],
        compiler_params=_cparams(("arbitrary", "arbitrary")),
        name="token_mixers",
    )(lq1, lk1, lq2, lk2, subg, qd, kd, vt, kd_c, vt_c, dec_f, dec_b, gn_g, qr, kr, vr, gr, kr_c, vr_c)


def _mlp_kernel(x_ref, od_ref, or_ref, g1_ref, sh2_ref, sc2_ref, g2_ref, n2_ref, fg_ref,
                wo_ref, w1_ref, w2_ref, o_ref):
    half = od_ref.shape[1]
    y = (jnp.dot(od_ref[...], wo_ref[0:half, :].astype(BF16), preferred_element_type=F32)
         + jnp.dot(or_ref[...], wo_ref[half:2 * half, :].astype(BF16), preferred_element_type=F32))
    x1 = x_ref[...] + g1_ref[...] * y
    h = _rmsnorm(x1, n2_ref[...]) * (1.0 + sc2_ref[...]) + sh2_ref[...]
    hb = h.astype(BF16)
    d_ff = w1_ref.shape[1]
    acc = None
    for f0 in range(0, d_ff, FF_CHUNK):
        u = jnp.dot(hb, w1_ref[:, f0:f0 + FF_CHUNK].astype(BF16), preferred_element_type=F32)
        u = jnp.square(jnp.maximum(u, 0.0)).astype(BF16)
        part = jnp.dot(u, w2_ref[f0:f0 + FF_CHUNK, :].astype(BF16), preferred_element_type=F32)
        acc = part if acc is None else acc + part
    x2 = x1 + g2_ref[...] * acc
    o_ref[...] = _rmsnorm(x2, fg_ref[...])


def _out_mlp(x, od, orr, g1, sh2, sc2, g2, n2, fg, wo, w1, w2):
    b, n, d = x.shape
    tm = MLP_TM
    tok = lambda w: pl.BlockSpec((None, tm, w), lambda bi, t: (bi, t, 0))
    row = pl.BlockSpec((None, 1, d), lambda bi, t: (bi, 0, 0))
    gain = pl.BlockSpec((1, d), lambda bi, t: (0, 0))
    resident = lambda a: pl.BlockSpec(a.shape, lambda bi, t: (0, 0), pipeline_mode=pl.Buffered(1))
    return pl.pallas_call(
        _mlp_kernel,
        grid=(b, n // tm),
        in_specs=[tok(d), tok(od.shape[2]), tok(orr.shape[2]), row, row, row, row, gain, gain,
                  resident(wo), resident(w1), resident(w2)],
        out_specs=tok(d),
        out_shape=jax.ShapeDtypeStruct((b, n, d), F32),
        compiler_params=_cparams(("arbitrary", "arbitrary")),
        name="out_mlp",
    )(x, od, orr, g1, sh2, sc2, g2, n2, fg, wo, w1, w2)


def kernel(x, c, ctx, c_ctx, w_ada, b_ada, norm1_g, norm2_g, w_in, lambda_q1, lambda_k1, lambda_q2,
           lambda_k2, diff_subln_g, ret_decay_fwd, ret_decay_bwd, ret_gn_g, w_out, w_mlp1, w_mlp2, final_g):
    assert w_ada.shape[0] == 1, "single-layer block"
    b, n, d = x.shape

    c_rows = jnp.concatenate([c, c_ctx[None], jnp.zeros((8 - b - 1, d), F32)], axis=0)
    mod = _adaln(c_rows, w_ada[0], b_ada).reshape(8, N_MOD, d)
    sh1, sc1, g1, sh2, sc2, g2 = [mod[:b, i][:, None, :] for i in range(N_MOD)]

    qd, kd, vt, qr, kr, vr, gr = _project_latent(x, sh1, sc1, norm1_g, w_in[0], _rope_tables(n))
    kd_c, vt_c, kr_c, vr_c = _project_ctx(ctx, mod[b, 0][None], mod[b, 1][None], norm1_g, w_in[0])

    od, orr = _token_mixers(lambda_q1, lambda_k1, lambda_q2, lambda_k2, diff_subln_g, qd, kd, vt, kd_c, vt_c,
                            ret_decay_fwd, ret_decay_bwd, ret_gn_g, qr, kr, vr, gr, kr_c, vr_c)

    return _out_mlp(x, od, orr, g1, sh2, sc2, g2, norm2_g, final_g[None],
                    w_out[0], w_mlp1[0], w_mlp2[0])
```

```python
import math

import numpy as np
import jax
import jax.numpy as jnp
from jax import lax
from jax.experimental import pallas as pl
from jax.experimental.pallas import tpu as pltpu

F32 = jnp.float32
BF16 = jnp.bfloat16

GRID_W = 64
HEADS = 4
HEAD_DIM = 64
V_DIM = 128
VT_ROWS = V_DIM + 16
ROPE_PAIRS = 16
ROPE_BASE = 10000.0
N_MOD = 6
EPS = 1e-6
LAM_INIT = 0.8 - 0.6 * math.exp(-0.3 * 0)
LOG2E = 1.4426950408889634
QK_SCALE = HEAD_DIM ** -0.5

LANES = 128
MXU_DIM = 256
VMEM_LIMIT_BYTES = 56 * 1024 * 1024

PROJ_TM = 1024
ATTN_QB = 256
ATTN_GROUPS = 8
RET_C = 256
MLP_TM = 512
FF_CHUNK = 1024
ADA_TN = 1536

COL_QD, COL_KD, COL_VD = (0, 512), (512, 1024), (1024, 1536)
COL_QR, COL_KR, COL_VR, COL_GR = (1536, 1792), (1792, 2048), (2048, 2560), (2560, 3072)


def _cparams(sem):
    return pltpu.CompilerParams(dimension_semantics=sem, vmem_limit_bytes=VMEM_LIMIT_BYTES)


def _rmsnorm(xf, g):
    return xf * lax.rsqrt(jnp.mean(xf * xf, axis=-1, keepdims=True) + EPS) * g


def _silu(x):
    return x * (1.0 / (1.0 + jnp.exp(-x)))


def _adaln_kernel(c_ref, w_ref, b_ref, o_ref):
    a = _silu(c_ref[...]).astype(BF16)
    o_ref[...] = jnp.dot(a, w_ref[...].astype(BF16), preferred_element_type=F32) + b_ref[...]


def _adaln(c_rows, w_ada, b_ada):
    r, d = c_rows.shape
    n_out = w_ada.shape[1]
    return pl.pallas_call(
        _adaln_kernel,
        grid=(n_out // ADA_TN,),
        in_specs=[pl.BlockSpec((r, d), lambda j: (0, 0)),
                  pl.BlockSpec((d, ADA_TN), lambda j: (0, j)),
                  pl.BlockSpec((1, ADA_TN), lambda j: (0, j))],
        out_specs=pl.BlockSpec((r, ADA_TN), lambda j: (0, j)),
        out_shape=jax.ShapeDtypeStruct((r, n_out), F32),
        compiler_params=_cparams(("arbitrary",)),
        name="adaln",
    )(c_rows, w_ada, b_ada)


def _rope_tables(n_tokens):
    rows = n_tokens // GRID_W
    row = np.repeat(np.arange(rows), GRID_W).astype(np.float64)
    col = np.tile(np.arange(GRID_W), rows).astype(np.float64)
    inv = (np.float32(ROPE_BASE) ** (-np.arange(ROPE_PAIRS, dtype=np.float32) / ROPE_PAIRS)).astype(np.float64)
    ang_r = row[:, None] * inv
    ang_c = col[:, None] * inv
    zeros = np.zeros_like(ang_r)
    cos64 = np.concatenate([np.cos(ang_r)] * 2 + [np.cos(ang_c)] * 2, axis=1)
    sa64 = np.concatenate([-np.sin(ang_r), zeros, -np.sin(ang_c), zeros], axis=1)
    sb64 = np.concatenate([zeros, np.sin(ang_r), zeros, np.sin(ang_c)], axis=1)
    dup = lambda t: jnp.asarray(np.concatenate([t, t], axis=1), dtype=F32)
    return dup(cos64), dup(sa64), dup(sb64)


def _modulated(x_ref, sh_ref, sc_ref, g_ref):
    h = _rmsnorm(x_ref[...], g_ref[...])
    return (h * (1.0 + sc_ref[...]) + sh_ref[...]).astype(BF16)


def _proj_latent_kernel(x_ref, sh_ref, sc_ref, g_ref, w_ref, c_ref, sa_ref, sb_ref,
                        qd_ref, kd_ref, vt_ref, qr_ref, kr_ref, vr_ref, gr_ref):
    hb = _modulated(x_ref, sh_ref, sc_ref, g_ref)
    c, sa, sb = c_ref[...], sa_ref[...], sb_ref[...]

    def project(cols):
        return jnp.dot(hb, w_ref[:, cols[0]:cols[1]].astype(BF16), preferred_element_type=F32)

    def rotate(y):
        slabs = []
        for s in range(y.shape[1] // LANES):
            ys = y[:, s * LANES:(s + 1) * LANES]
            slabs.append(ys * c + pltpu.roll(ys, LANES - ROPE_PAIRS, 1) * sa
                         + pltpu.roll(ys, ROPE_PAIRS, 1) * sb)
        return jnp.concatenate(slabs, axis=1)

    qd_ref[...] = (rotate(project(COL_QD)) * (QK_SCALE * LOG2E)).astype(BF16)
    kd_ref[...] = rotate(project(COL_KD)).astype(BF16)
    vd = project(COL_VD)
    ones = jnp.ones((VT_ROWS - V_DIM, vd.shape[0]), BF16)
    for h in range(HEADS):
        vt_ref[h, 0:V_DIM, :] = vd[:, h * V_DIM:(h + 1) * V_DIM].T.astype(BF16)
        vt_ref[h, V_DIM:VT_ROWS, :] = ones
    qr_ref[...] = rotate(project(COL_QR)).astype(BF16)
    kr_ref[...] = (rotate(project(COL_KR)) * QK_SCALE).astype(BF16)
    vr_ref[...] = project(COL_VR).astype(BF16)
    gr_ref[...] = _silu(project(COL_GR)).astype(BF16)


def _project_latent(x, sh, sc, g, w_in, tables):
    b, n, d = x.shape
    tm = PROJ_TM
    tok = lambda w: pl.BlockSpec((None, tm, w), lambda t, bi: (bi, t, 0))
    row = pl.BlockSpec((None, 1, d), lambda t, bi: (bi, 0, 0))
    table = pl.BlockSpec((tm, LANES), lambda t, bi: (t, 0))
    out_specs = [tok(512), tok(512),
                 pl.BlockSpec((None, HEADS, VT_ROWS, tm), lambda t, bi: (bi, 0, 0, t)),
                 tok(256), tok(256), tok(512), tok(512)]
    out_shape = [jax.ShapeDtypeStruct((b, n, 512), BF16),
                 jax.ShapeDtypeStruct((b, n, 512), BF16),
                 jax.ShapeDtypeStruct((b, HEADS, VT_ROWS, n), BF16),
                 jax.ShapeDtypeStruct((b, n, 256), BF16),
                 jax.ShapeDtypeStruct((b, n, 256), BF16),
                 jax.ShapeDtypeStruct((b, n, 512), BF16),
                 jax.ShapeDtypeStruct((b, n, 512), BF16)]
    return pl.pallas_call(
        _proj_latent_kernel,
        grid=(n // tm, b),
        in_specs=[tok(d), row, row, pl.BlockSpec((1, d), lambda t, bi: (0, 0)),
                  pl.BlockSpec(w_in.shape, lambda t, bi: (0, 0), pipeline_mode=pl.Buffered(1)),
                  table, table, table],
        out_specs=out_specs,
        out_shape=out_shape,
        compiler_params=_cparams(("arbitrary", "arbitrary")),
        name="proj_latent",
    )(x, sh, sc, g, w_in, *tables)


def _proj_ctx_kernel(x_ref, sh_ref, sc_ref, g_ref, w_ref, kd_ref, vt_ref, kr_ref, vr_ref):
    hb = _modulated(x_ref, sh_ref, sc_ref, g_ref)
    b, nctx = kd_ref.shape[0], kd_ref.shape[1]

    def project(cols):
        return jnp.dot(hb, w_ref[:, cols[0]:cols[1]].astype(BF16), preferred_element_type=F32)

    kd, vd, kr, vr = project(COL_KD), project(COL_VD), project(COL_KR) * QK_SCALE, project(COL_VR)
    for bi in range(b):
        rows = slice(bi * nctx, (bi + 1) * nctx)
        kd_ref[bi] = kd[rows].astype(BF16)
        kr_ref[bi] = kr[rows].astype(BF16)
        vr_ref[bi] = vr[rows].astype(BF16)
        for h in range(HEADS):
            vt_ref[bi, h, 0:V_DIM, :] = vd[rows, h * V_DIM:(h + 1) * V_DIM].T.astype(BF16)
            vt_ref[bi, h, V_DIM:VT_ROWS, :] = jnp.ones((VT_ROWS - V_DIM, nctx), BF16)


def _project_ctx(ctx, sh, sc, g, w_in):
    b, nctx, d = ctx.shape
    full = lambda shape: pl.BlockSpec(shape, lambda i: (0,) * len(shape))
    x2 = ctx.reshape(b * nctx, d)
    shapes = [(b, nctx, 512), (b, HEADS, VT_ROWS, nctx), (b, nctx, 256), (b, nctx, 512)]
    return pl.pallas_call(
        _proj_ctx_kernel,
        grid=(1,),
        in_specs=[full(a.shape) for a in (x2, sh, sc, g, w_in)],
        out_specs=[full(shape) for shape in shapes],
        out_shape=[jax.ShapeDtypeStruct(shape, BF16) for shape in shapes],
        compiler_params=_cparams(("arbitrary",)),
        name="proj_ctx",
    )(x2, sh, sc, g, w_in)


def _mixer_kernel(lq1_ref, lk1_ref, lq2_ref, lk2_ref, subg_ref, q_ref, k_ref, vt_ref, kc_ref, vtc_ref,
                  decf_ref, decb_ref, gn_ref, rq_ref, rk_ref, rv_ref, rg_ref, rkc_ref, rvc_ref,
                  o_ref, ro_ref, sa_scr, sb_scr, rsb_scr, rsf_scr):
    n = q_ref.shape[0]
    head = pl.program_id(1)
    lane = lax.broadcasted_iota(jnp.int32, (1, LANES), 1)

    lam = (jnp.exp(jnp.sum(lq1_ref[...] * lk1_ref[...], axis=-1, keepdims=True))
           - jnp.exp(jnp.sum(lq2_ref[...] * lk2_ref[...], axis=-1, keepdims=True)) + LAM_INIT)
    first_map = lane < HEAD_DIM
    subg = subg_ref[...]
    nq = n // ATTN_QB

    nctx = kc_ref.shape[0]
    gsz = n // ATTN_GROUPS
    groups = [[(k_ref, vt_ref, g * gsz, (g + 1) * gsz, nctx + g * gsz)] for g in range(ATTN_GROUPS)]
    groups[0].insert(0, (kc_ref, vtc_ref, 0, nctx, 0))

    def scores(i, s_scr):
        r0 = pl.multiple_of(i * ATTN_QB, ATTN_QB)
        q = q_ref[pl.ds(r0, ATTN_QB), :]
        zero = jnp.zeros_like(q)
        qms = (jnp.where(first_map, q, zero), jnp.where(first_map, zero, q))
        ms = [None, None]
        for pieces in groups:
            for kr, _, lo, hi, dst in pieces:
                for mp in range(2):
                    s = lax.dot_general(kr[lo:hi, :], qms[mp], (((1,), (1,)), ((), ())),
                                        preferred_element_type=F32)
                    s_scr[mp, dst:dst + hi - lo, :] = s
                    mg = jnp.max(s, axis=0, keepdims=True)
                    ms[mp] = mg if ms[mp] is None else jnp.maximum(ms[mp], mg)
            yield tuple(ms)

    def finish(i, s_scr, ms):
        accs = [None, None]
        for g, pieces in enumerate(groups):
            for _, vr, lo, hi, dst in pieces:
                for mp in range(2):
                    p = jnp.exp2(s_scr[mp, dst:dst + hi - lo, :] - ms[mp]).astype(BF16)
                    part = jnp.dot(vr[:, lo:hi], p, preferred_element_type=F32)
                    accs[mp] = part if accs[mp] is None else accs[mp] + part
            if g == len(groups) - 1:
                ots = [a[0:V_DIM, :] / a[V_DIM:V_DIM + 1, :] for a in accs]
                o = (ots[0] - lam * ots[1]).T
                o = _rmsnorm(o, subg) * (1.0 - LAM_INIT)
                r0 = pl.multiple_of(i * ATTN_QB, ATTN_QB)
                o_ref[pl.ds(r0, ATTN_QB), :] = o.astype(o_ref.dtype)
            yield None

    c = RET_C
    nc = n // c

    def log_gamma(dec):
        z = -dec
        return -(jnp.maximum(z, 0.0) + jnp.log(1.0 + jnp.exp(-jnp.abs(z))))

    hsel = lax.broadcasted_iota(jnp.int32, (1, HEADS), 1) == head

    def pick(vec):
        return jnp.sum(jnp.where(hsel, vec, 0.0), axis=-1, keepdims=True)

    lgf, lgb = pick(log_gamma(decf_ref[...])), pick(log_gamma(decb_ref[...]))
    ii = lax.broadcasted_iota(jnp.int32, (c, c), 0).astype(F32)
    jj = lax.broadcasted_iota(jnp.int32, (c, c), 1).astype(F32)
    rel = ii - jj
    pos = lax.broadcasted_iota(jnp.int32, (c, LANES), 0).astype(F32)
    decay = (jnp.where(rel >= 0, jnp.exp(jnp.maximum(rel, 0.0) * lgf), 0.0)
             + jnp.where(rel <= 0, jnp.exp(jnp.maximum(-rel, 0.0) * lgb), 0.0))
    qdec_f, qdec_b = jnp.exp((pos + 1.0) * lgf), jnp.exp((c - pos) * lgb)
    kdec_f, kdec_b = jnp.exp((c - 1.0 - pos) * lgf), jnp.exp(pos * lgb)
    cdec_f, cdec_b = jnp.exp(c * lgf), jnp.exp(c * lgb)
    rmask = (lane >= HEAD_DIM).astype(jnp.int32) == head % 2
    gn = gn_ref[...]

    def kv_state(k, v, kdec):
        kd = (k.astype(F32) * kdec).astype(BF16)
        return lax.dot_general(kd, v, (((0,), (0,)), ((), ())), preferred_element_type=F32)

    rsf_scr[...] = kv_state(rkc_ref[...], rvc_ref[...], kdec_f)

    def backward_states():
        sb = kv_state(rkc_ref[...], rvc_ref[...], kdec_b)
        for ci in reversed(range(nc)):
            rsb_scr[ci] = sb.astype(BF16)
            if ci > 0:
                sb = sb * cdec_b + kv_state(rk_ref[ci * c:(ci + 1) * c, :], rv_ref[ci * c:(ci + 1) * c, :], kdec_b)
            yield None

    def ret_chunk(ci):
        r0 = pl.multiple_of(ci * c, c)
        q = rq_ref[pl.ds(r0, c), :]
        k = rk_ref[pl.ds(r0, c), :]
        v = rv_ref[pl.ds(r0, c), :]
        qm = jnp.where(rmask, q, jnp.zeros_like(q))
        sc = lax.dot_general(qm, k, (((1,), (1,)), ((), ())), preferred_element_type=F32)
        a = (sc * decay).astype(BF16)
        yield None
        sf = rsf_scr[...]
        o = jnp.dot(a, v, preferred_element_type=F32)
        o = o + jnp.dot(qm, sf.astype(BF16), preferred_element_type=F32) * qdec_f
        o = o + jnp.dot(qm, rsb_scr[ci], preferred_element_type=F32) * qdec_b
        yield None
        mu = jnp.mean(o, axis=-1, keepdims=True)
        var = jnp.mean(jnp.square(o - mu), axis=-1, keepdims=True)
        on = (o - mu) * lax.rsqrt(var + EPS) * gn
        gate = rg_ref[pl.ds(r0, c), :].astype(F32)
        ro_ref[pl.ds(r0, c), :] = (on * gate).astype(ro_ref.dtype)
        yield None
        rsf_scr[...] = sf * cdec_f + kv_state(k, v, kdec_f)
        yield None

    def run(*stages, side=None, side_steps=0.5):
        last = [None] * len(stages)
        done = 0
        for t, vals in enumerate(zip(*stages)):
            last = list(vals)
            while side is not None and done < (t + 1) * side_steps:
                next(side)
                done += 1
        return last

    def step(j, m_a):
        m_b, _ = run(scores(2 * j + 1, sb_scr), finish(2 * j, sa_scr, m_a), side=ret_chunk(2 * j))
        m_a, _ = run(scores(2 * j + 2, sa_scr), finish(2 * j + 1, sb_scr, m_b), side=ret_chunk(2 * j + 1))
        return m_a

    (m_a,) = run(scores(0, sa_scr), side=backward_states(), side_steps=nc / ATTN_GROUPS)
    m_a = lax.fori_loop(0, nq // 2 - 1, step, m_a)
    m_b, _ = run(scores(nq - 1, sb_scr), finish(nq - 2, sa_scr, m_a), side=ret_chunk(nq - 2))
    run(finish(nq - 1, sb_scr, m_b), side=ret_chunk(nq - 1))


def _token_mixers(lq1, lk1, lq2, lk2, subg, qd, kd, vt, kd_c, vt_c,
                  dec_f, dec_b, gn_g, qr, kr, vr, gr, kr_c, vr_c):
    b, n, _ = qd.shape
    nctx = kd_c.shape[1]
    assert (n // ATTN_QB) % 2 == 0 and n % (ATTN_GROUPS * MXU_DIM) == 0 and nctx % MXU_DIM == 0
    assert ATTN_GROUPS % 8 == 0 and nctx == RET_C and n // RET_C == n // ATTN_QB
    vec = lambda w: pl.BlockSpec((1, w), lambda bi, h: (0, 0))
    head = lambda rows: pl.BlockSpec((None, rows, LANES), lambda bi, h: (bi, 0, h))
    pair = lambda rows: pl.BlockSpec((None, rows, LANES), lambda bi, h: (bi, 0, h // 2))
    headt = lambda cols: pl.BlockSpec((None, None, VT_ROWS, cols), lambda bi, h: (bi, h, 0, 0))
    out = jax.ShapeDtypeStruct((b, n, HEADS * V_DIM), BF16)
    return pl.pallas_call(
        _mixer_kernel,
        grid=(b, HEADS),
        in_specs=[vec(HEAD_DIM)] * 4 + [vec(V_DIM), head(n), head(n), headt(n), head(nctx), headt(nctx),
                  vec(HEADS), vec(HEADS), pl.BlockSpec((1, V_DIM), lambda bi, h: (0, h)),
                  pair(n), pair(n), head(n), head(n), pair(nctx), head(nctx)],
        out_specs=[head(n), head(n)],
        out_shape=[out, out],
        scratch_shapes=[pltpu.VMEM((2, n + nctx, ATTN_QB), F32)] * 2
        + [pltpu.VMEM((n // RET_C, LANES, V_DIM), BF16), pltpu.VMEM((LANES, V_DIM), F32)],
        compiler_params=_cparams(("arbitrary", "arbitrary")),
        name="token_mixers",
    )(lq1, lk1, lq2, lk2, subg, qd, kd, vt, kd_c, vt_c, dec_f, dec_b, gn_g, qr, kr, vr, gr, kr_c, vr_c)


def _mlp_kernel(x_ref, od_ref, or_ref, g1_ref, sh2_ref, sc2_ref, g2_ref, n2_ref, fg_ref,
                wo_ref, w1_ref, w2_ref, o_ref):
    half = od_ref.shape[1]
    y = (jnp.dot(od_ref[...], wo_ref[0:half, :].astype(BF16), preferred_element_type=F32)
         + jnp.dot(or_ref[...], wo_ref[half:2 * half, :].astype(BF16), preferred_element_type=F32))
    x1 = x_ref[...] + g1_ref[...] * y
    h = _rmsnorm(x1, n2_ref[...]) * (1.0 + sc2_ref[...]) + sh2_ref[...]
    hb = h.astype(BF16)
    d_ff = w1_ref.shape[1]
    acc = None
    for f0 in range(0, d_ff, FF_CHUNK):
        u = jnp.dot(hb, w1_ref[:, f0:f0 + FF_CHUNK].astype(BF16), preferred_element_type=F32)
        u = jnp.square(jnp.maximum(u, 0.0)).astype(BF16)
        part = jnp.dot(u, w2_ref[f0:f0 + FF_CHUNK, :].astype(BF16), preferred_element_type=F32)
        acc = part if acc is None else acc + part
    x2 = x1 + g2_ref[...] * acc
    o_ref[...] = _rmsnorm(x2, fg_ref[...])


def _out_mlp(x, od, orr, g1, sh2, sc2, g2, n2, fg, wo, w1, w2):
    b, n, d = x.shape
    tm = MLP_TM
    tok = lambda w: pl.BlockSpec((None, tm, w), lambda bi, t: (bi, t, 0))
    row = pl.BlockSpec((None, 1, d), lambda bi, t: (bi, 0, 0))
    gain = pl.BlockSpec((1, d), lambda bi, t: (0, 0))
    resident = lambda a: pl.BlockSpec(a.shape, lambda bi, t: (0, 0), pipeline_mode=pl.Buffered(1))
    return pl.pallas_call(
        _mlp_kernel,
        grid=(b, n // tm),
        in_specs=[tok(d), tok(od.shape[2]), tok(orr.shape[2]), row, row, row, row, gain, gain,
                  resident(wo), resident(w1), resident(w2)],
        out_specs=tok(d),
        out_shape=jax.ShapeDtypeStruct((b, n, d), F32),
        compiler_params=_cparams(("arbitrary", "arbitrary")),
        name="out_mlp",
    )(x, od, orr, g1, sh2, sc2, g2, n2, fg, wo, w1, w2)


def kernel(x, c, ctx, c_ctx, w_ada, b_ada, norm1_g, norm2_g, w_in, lambda_q1, lambda_k1, lambda_q2,
           lambda_k2, diff_subln_g, ret_decay_fwd, ret_decay_bwd, ret_gn_g, w_out, w_mlp1, w_mlp2, final_g):
    assert w_ada.shape[0] == 1, "single-layer block"
    b, n, d = x.shape

    c_rows = jnp.concatenate([c, c_ctx[None], jnp.zeros((8 - b - 1, d), F32)], axis=0)
    mod = _adaln(c_rows, w_ada[0], b_ada).reshape(8, N_MOD, d)
    sh1, sc1, g1, sh2, sc2, g2 = [mod[:b, i][:, None, :] for i in range(N_MOD)]

    qd, kd, vt, qr, kr, vr, gr = _project_latent(x, sh1, sc1, norm1_g, w_in[0], _rope_tables(n))
    kd_c, vt_c, kr_c, vr_c = _project_ctx(ctx, mod[b, 0][None], mod[b, 1][None], norm1_g, w_in[0])

    od, orr = _token_mixers(lambda_q1, lambda_k1, lambda_q2, lambda_k2, diff_subln_g, qd, kd, vt, kd_c, vt_c,
                            ret_decay_fwd, ret_decay_bwd, ret_gn_g, qr, kr, vr, gr, kr_c, vr_c)

    return _out_mlp(x, od, orr, g1, sh2, sc2, g2, norm2_g, final_g[None],
                    w_out[0], w_mlp1[0], w_mlp2[0])
```

```python
import math

import numpy as np
import jax
import jax.numpy as jnp
from jax import lax
from jax.experimental import pallas as pl
from jax.experimental.pallas import tpu as pltpu

F32 = jnp.float32
BF16 = jnp.bfloat16

GRID_W = 64
HEADS = 4
HEAD_DIM = 64
V_DIM = 128
VT_ROWS = V_DIM + 16
ROPE_PAIRS = 16
ROPE_BASE = 10000.0
N_MOD = 6
EPS = 1e-6
LAM_INIT = 0.8 - 0.6 * math.exp(-0.3 * 0)
LOG2E = 1.4426950408889634
QK_SCALE = HEAD_DIM ** -0.5

LANES = 128
MXU_DIM = 256
VMEM_LIMIT_BYTES = 56 * 1024 * 1024

PROJ_TM = 1024
ATTN_QB = 256
ATTN_GROUPS = 8
RET_C = 256
MLP_TM = 512
FF_CHUNK = 1024
ADA_TN = 1536

COL_QD, COL_KD, COL_VD = (0, 512), (512, 1024), (1024, 1536)
COL_QR, COL_KR, COL_VR, COL_GR = (1536, 1792), (1792, 2048), (2048, 2560), (2560, 3072)


def _cparams(sem):
    return pltpu.CompilerParams(dimension_semantics=sem, vmem_limit_bytes=VMEM_LIMIT_BYTES)


def _rmsnorm(xf, g):
    return xf * lax.rsqrt(jnp.mean(xf * xf, axis=-1, keepdims=True) + EPS) * g


def _silu(x):
    return x * (1.0 / (1.0 + jnp.exp(-x)))


def _adaln_kernel(c_ref, w_ref, b_ref, o_ref):
    a = _silu(c_ref[...]).astype(BF16)
    o_ref[...] = jnp.dot(a, w_ref[...].astype(BF16), preferred_element_type=F32) + b_ref[...]


def _adaln(c_rows, w_ada, b_ada):
    r, d = c_rows.shape
    n_out = w_ada.shape[1]
    return pl.pallas_call(
        _adaln_kernel,
        grid=(n_out // ADA_TN,),
        in_specs=[pl.BlockSpec((r, d), lambda j: (0, 0)),
                  pl.BlockSpec((d, ADA_TN), lambda j: (0, j)),
                  pl.BlockSpec((1, ADA_TN), lambda j: (0, j))],
        out_specs=pl.BlockSpec((r, ADA_TN), lambda j: (0, j)),
        out_shape=jax.ShapeDtypeStruct((r, n_out), F32),
        compiler_params=_cparams(("arbitrary",)),
        name="adaln",
    )(c_rows, w_ada, b_ada)


def _rope_tables(n_tokens):
    rows = n_tokens // GRID_W
    row = np.repeat(np.arange(rows), GRID_W).astype(np.float64)
    col = np.tile(np.arange(GRID_W), rows).astype(np.float64)
    inv = (np.float32(ROPE_BASE) ** (-np.arange(ROPE_PAIRS, dtype=np.float32) / ROPE_PAIRS)).astype(np.float64)
    ang_r = row[:, None] * inv
    ang_c = col[:, None] * inv
    zeros = np.zeros_like(ang_r)
    cos64 = np.concatenate([np.cos(ang_r)] * 2 + [np.cos(ang_c)] * 2, axis=1)
    sa64 = np.concatenate([-np.sin(ang_r), zeros, -np.sin(ang_c), zeros], axis=1)
    sb64 = np.concatenate([zeros, np.sin(ang_r), zeros, np.sin(ang_c)], axis=1)
    dup = lambda t: jnp.asarray(np.concatenate([t, t], axis=1), dtype=F32)
    return dup(cos64), dup(sa64), dup(sb64)


def _modulated(x_ref, sh_ref, sc_ref, g_ref):
    h = _rmsnorm(x_ref[...], g_ref[...])
    return (h * (1.0 + sc_ref[...]) + sh_ref[...]).astype(BF16)


def _proj_latent_kernel(x_ref, sh_ref, sc_ref, g_ref, w_ref, c_ref, sa_ref, sb_ref,
                        qd_ref, kd_ref, vt_ref, qr_ref, kr_ref, vr_ref, gr_ref):
    hb = _modulated(x_ref, sh_ref, sc_ref, g_ref)
    c, sa, sb = c_ref[...], sa_ref[...], sb_ref[...]

    def project(cols):
        return jnp.dot(hb, w_ref[:, cols[0]:cols[1]].astype(BF16), preferred_element_type=F32)

    def rotate(y):
        slabs = []
        for s in range(y.shape[1] // LANES):
            ys = y[:, s * LANES:(s + 1) * LANES]
            slabs.append(ys * c + pltpu.roll(ys, LANES - ROPE_PAIRS, 1) * sa
                         + pltpu.roll(ys, ROPE_PAIRS, 1) * sb)
        return jnp.concatenate(slabs, axis=1)

    qd_ref[...] = (rotate(project(COL_QD)) * (QK_SCALE * LOG2E)).astype(BF16)
    kd_ref[...] = rotate(project(COL_KD)).astype(BF16)
    vd = project(COL_VD)
    ones = jnp.ones((VT_ROWS - V_DIM, vd.shape[0]), BF16)
    for h in range(HEADS):
        vt_ref[h, 0:V_DIM, :] = vd[:, h * V_DIM:(h + 1) * V_DIM].T.astype(BF16)
        vt_ref[h, V_DIM:VT_ROWS, :] = ones
    gr_ref[...] = _silu(project(COL_GR)).astype(BF16)
    qr_ref[...] = rotate(project(COL_QR)).astype(BF16)
    kr_ref[...] = (rotate(project(COL_KR)) * QK_SCALE).astype(BF16)
    vr_ref[...] = project(COL_VR).astype(BF16)


def _project_latent(x, sh, sc, g, w_in, tables):
    b, n, d = x.shape
    tm = PROJ_TM
    tok = lambda w: pl.BlockSpec((None, tm, w), lambda t, bi: (bi, t, 0))
    row = pl.BlockSpec((None, 1, d), lambda t, bi: (bi, 0, 0))
    table = pl.BlockSpec((tm, LANES), lambda t, bi: (t, 0))
    out_specs = [tok(512), tok(512),
                 pl.BlockSpec((None, HEADS, VT_ROWS, tm), lambda t, bi: (bi, 0, 0, t)),
                 tok(256), tok(256), tok(512), tok(512)]
    out_shape = [jax.ShapeDtypeStruct((b, n, 512), BF16),
                 jax.ShapeDtypeStruct((b, n, 512), BF16),
                 jax.ShapeDtypeStruct((b, HEADS, VT_ROWS, n), BF16),
                 jax.ShapeDtypeStruct((b, n, 256), BF16),
                 jax.ShapeDtypeStruct((b, n, 256), BF16),
                 jax.ShapeDtypeStruct((b, n, 512), BF16),
                 jax.ShapeDtypeStruct((b, n, 512), BF16)]
    return pl.pallas_call(
        _proj_latent_kernel,
        grid=(n // tm, b),
        in_specs=[tok(d), row, row, pl.BlockSpec((1, d), lambda t, bi: (0, 0)),
                  pl.BlockSpec(w_in.shape, lambda t, bi: (0, 0), pipeline_mode=pl.Buffered(1)),
                  table, table, table],
        out_specs=out_specs,
        out_shape=out_shape,
        compiler_params=_cparams(("arbitrary", "arbitrary")),
        name="proj_latent",
    )(x, sh, sc, g, w_in, *tables)


def _proj_ctx_kernel(x_ref, sh_ref, sc_ref, g_ref, w_ref, kd_ref, vt_ref, kr_ref, vr_ref):
    hb = _modulated(x_ref, sh_ref, sc_ref, g_ref)
    b, nctx = kd_ref.shape[0], kd_ref.shape[1]

    def project(cols):
        return jnp.dot(hb, w_ref[:, cols[0]:cols[1]].astype(BF16), preferred_element_type=F32)

    kd, vd, kr, vr = project(COL_KD), project(COL_VD), project(COL_KR) * QK_SCALE, project(COL_VR)
    for bi in range(b):
        rows = slice(bi * nctx, (bi + 1) * nctx)
        kd_ref[bi] = kd[rows].astype(BF16)
        kr_ref[bi] = kr[rows].astype(BF16)
        vr_ref[bi] = vr[rows].astype(BF16)
        for h in range(HEADS):
            vt_ref[bi, h, 0:V_DIM, :] = vd[rows, h * V_DIM:(h + 1) * V_DIM].T.astype(BF16)
            vt_ref[bi, h, V_DIM:VT_ROWS, :] = jnp.ones((VT_ROWS - V_DIM, nctx), BF16)


def _project_ctx(ctx, sh, sc, g, w_in):
    b, nctx, d = ctx.shape
    full = lambda shape: pl.BlockSpec(shape, lambda i: (0,) * len(shape))
    x2 = ctx.reshape(b * nctx, d)
    shapes = [(b, nctx, 512), (b, HEADS, VT_ROWS, nctx), (b, nctx, 256), (b, nctx, 512)]
    return pl.pallas_call(
        _proj_ctx_kernel,
        grid=(1,),
        in_specs=[full(a.shape) for a in (x2, sh, sc, g, w_in)],
        out_specs=[full(shape) for shape in shapes],
        out_shape=[jax.ShapeDtypeStruct(shape, BF16) for shape in shapes],
        compiler_params=_cparams(("arbitrary",)),
        name="proj_ctx",
    )(x2, sh, sc, g, w_in)


def _mixer_kernel(lq1_ref, lk1_ref, lq2_ref, lk2_ref, subg_ref, q_ref, k_ref, vt_ref, kc_ref, vtc_ref,
                  decf_ref, decb_ref, gn_ref, rq_ref, rk_ref, rv_ref, rg_ref, rkc_ref, rvc_ref,
                  o_ref, ro_ref, sa_scr, sb_scr, rsb_scr, rsf_scr):
    n = q_ref.shape[0]
    head = pl.program_id(1)
    lane = lax.broadcasted_iota(jnp.int32, (1, LANES), 1)

    lam = (jnp.exp(jnp.sum(lq1_ref[...] * lk1_ref[...], axis=-1, keepdims=True))
           - jnp.exp(jnp.sum(lq2_ref[...] * lk2_ref[...], axis=-1, keepdims=True)) + LAM_INIT)
    first_map = lane < HEAD_DIM
    subg = subg_ref[...]
    nq = n // ATTN_QB

    nctx = kc_ref.shape[0]
    gsz = n // ATTN_GROUPS
    groups = [[(k_ref, vt_ref, g * gsz, (g + 1) * gsz, nctx + g * gsz)] for g in range(ATTN_GROUPS)]
    groups[0].insert(0, (kc_ref, vtc_ref, 0, nctx, 0))

    def scores(i, s_scr):
        r0 = pl.multiple_of(i * ATTN_QB, ATTN_QB)
        q = q_ref[pl.ds(r0, ATTN_QB), :]
        zero = jnp.zeros_like(q)
        qms = (jnp.where(first_map, q, zero), jnp.where(first_map, zero, q))
        ms = [None, None]
        for pieces in groups:
            for kr, _, lo, hi, dst in pieces:
                for mp in range(2):
                    s = lax.dot_general(kr[lo:hi, :], qms[mp], (((1,), (1,)), ((), ())),
                                        preferred_element_type=F32)
                    s_scr[mp, dst:dst + hi - lo, :] = s
                    mg = jnp.max(s, axis=0, keepdims=True)
                    ms[mp] = mg if ms[mp] is None else jnp.maximum(ms[mp], mg)
            yield tuple(ms)

    def finish(i, s_scr, ms):
        accs = [None, None]
        for g, pieces in enumerate(groups):
            for _, vr, lo, hi, dst in pieces:
                for mp in range(2):
                    p = jnp.exp2(s_scr[mp, dst:dst + hi - lo, :] - ms[mp]).astype(BF16)
                    part = jnp.dot(vr[:, lo:hi], p, preferred_element_type=F32)
                    accs[mp] = part if accs[mp] is None else accs[mp] + part
            if g == len(groups) - 1:
                ots = [a[0:V_DIM, :] / a[V_DIM:V_DIM + 1, :] for a in accs]
                o = (ots[0] - lam * ots[1]).T
                o = _rmsnorm(o, subg) * (1.0 - LAM_INIT)
                r0 = pl.multiple_of(i * ATTN_QB, ATTN_QB)
                o_ref[pl.ds(r0, ATTN_QB), :] = o.astype(o_ref.dtype)
            yield None

    c = RET_C
    nc = n // c

    def log_gamma(dec):
        z = -dec
        return -(jnp.maximum(z, 0.0) + jnp.log(1.0 + jnp.exp(-jnp.abs(z))))

    hsel = lax.broadcasted_iota(jnp.int32, (1, HEADS), 1) == head

    def pick(vec):
        return jnp.sum(jnp.where(hsel, vec, 0.0), axis=-1, keepdims=True)

    lgf, lgb = pick(log_gamma(decf_ref[...])), pick(log_gamma(decb_ref[...]))
    ii = lax.broadcasted_iota(jnp.int32, (c, c), 0).astype(F32)
    jj = lax.broadcasted_iota(jnp.int32, (c, c), 1).astype(F32)
    rel = ii - jj
    pos = lax.broadcasted_iota(jnp.int32, (c, LANES), 0).astype(F32)
    decay = (jnp.where(rel >= 0, jnp.exp(jnp.maximum(rel, 0.0) * lgf), 0.0)
             + jnp.where(rel <= 0, jnp.exp(jnp.maximum(-rel, 0.0) * lgb), 0.0))
    qdec_f, qdec_b = jnp.exp((pos + 1.0) * lgf), jnp.exp((c - pos) * lgb)
    kdec_f, kdec_b = jnp.exp((c - 1.0 - pos) * lgf), jnp.exp(pos * lgb)
    cdec_f, cdec_b = jnp.exp(c * lgf), jnp.exp(c * lgb)
    rmask = (lane >= HEAD_DIM).astype(jnp.int32) == head % 2
    gn = gn_ref[...]

    def kv_state(k, v, kdec):
        kd = (k.astype(F32) * kdec).astype(BF16)
        return lax.dot_general(kd, v, (((0,), (0,)), ((), ())), preferred_element_type=F32)

    rsf_scr[...] = kv_state(rkc_ref[...], rvc_ref[...], kdec_f)

    def backward_states():
        sb = kv_state(rkc_ref[...], rvc_ref[...], kdec_b)
        for ci in reversed(range(nc)):
            rsb_scr[ci] = sb.astype(BF16)
            if ci > 0:
                sb = sb * cdec_b + kv_state(rk_ref[ci * c:(ci + 1) * c, :], rv_ref[ci * c:(ci + 1) * c, :], kdec_b)
            yield None

    def ret_chunk(ci):
        r0 = pl.multiple_of(ci * c, c)
        q = rq_ref[pl.ds(r0, c), :]
        k = rk_ref[pl.ds(r0, c), :]
        v = rv_ref[pl.ds(r0, c), :]
        qm = jnp.where(rmask, q, jnp.zeros_like(q))
        sc = lax.dot_general(qm, k, (((1,), (1,)), ((), ())), preferred_element_type=F32)
        a = (sc * decay).astype(BF16)
        yield None
        sf = rsf_scr[...]
        o = jnp.dot(a, v, preferred_element_type=F32)
        o = o + jnp.dot(qm, sf.astype(BF16), preferred_element_type=F32) * qdec_f
        o = o + jnp.dot(qm, rsb_scr[ci], preferred_element_type=F32) * qdec_b
        yield None
        mu = jnp.mean(o, axis=-1, keepdims=True)
        var = jnp.mean(jnp.square(o - mu), axis=-1, keepdims=True)
        on = (o - mu) * lax.rsqrt(var + EPS) * gn
        gate = rg_ref[pl.ds(r0, c), :].astype(F32)
        ro_ref[pl.ds(r0, c), :] = (on * gate).astype(ro_ref.dtype)
        yield None
        rsf_scr[...] = sf * cdec_f + kv_state(k, v, kdec_f)
        yield None

    def run(*stages, side=None, side_steps=0.5):
        last = [None] * len(stages)
        done = 0
        for t, vals in enumerate(zip(*stages)):
            last = list(vals)
            while side is not None and done < (t + 1) * side_steps:
                next(side)
                done += 1
        return last

    def step(j, m_a):
        m_b, _ = run(scores(2 * j + 1, sb_scr), finish(2 * j, sa_scr, m_a), side=ret_chunk(2 * j))
        m_a, _ = run(scores(2 * j + 2, sa_scr), finish(2 * j + 1, sb_scr, m_b), side=ret_chunk(2 * j + 1))
        return m_a

    (m_a,) = run(scores(0, sa_scr), side=backward_states(), side_steps=nc / ATTN_GROUPS)
    m_a = lax.fori_loop(0, nq // 2 - 1, step, m_a)
    m_b, _ = run(scores(nq - 1, sb_scr), finish(nq - 2, sa_scr, m_a), side=ret_chunk(nq - 2))
    run(finish(nq - 1, sb_scr, m_b), side=ret_chunk(nq - 1))


def _token_mixers(lq1, lk1, lq2, lk2, subg, qd, kd, vt, kd_c, vt_c,
                  dec_f, dec_b, gn_g, qr, kr, vr, gr, kr_c, vr_c):
    b, n, _ = qd.shape
    nctx = kd_c.shape[1]
    assert (n // ATTN_QB) % 2 == 0 and n % (ATTN_GROUPS * MXU_DIM) == 0 and nctx % MXU_DIM == 0
    assert ATTN_GROUPS % 8 == 0 and nctx == RET_C and n // RET_C == n // ATTN_QB
    vec = lambda w: pl.BlockSpec((1, w), lambda bi, h: (0, 0))
    head = lambda rows: pl.BlockSpec((None, rows, LANES), lambda bi, h: (bi, 0, h))
    pair = lambda rows: pl.BlockSpec((None, rows, LANES), lambda bi, h: (bi, 0, h // 2))
    headt = lambda cols: pl.BlockSpec((None, None, VT_ROWS, cols), lambda bi, h: (bi, h, 0, 0))
    out = jax.ShapeDtypeStruct((b, n, HEADS * V_DIM), BF16)
    return pl.pallas_call(
        _mixer_kernel,
        grid=(b, HEADS),
        in_specs=[vec(HEAD_DIM)] * 4 + [vec(V_DIM), head(n), head(n), headt(n), head(nctx), headt(nctx),
                  vec(HEADS), vec(HEADS), pl.BlockSpec((1, V_DIM), lambda bi, h: (0, h)),
                  pair(n), pair(n), head(n), head(n), pair(nctx), head(nctx)],
        out_specs=[head(n), head(n)],
        out_shape=[out, out],
        scratch_shapes=[pltpu.VMEM((2, n + nctx, ATTN_QB), F32)] * 2
        + [pltpu.VMEM((n // RET_C, LANES, V_DIM), BF16), pltpu.VMEM((LANES, V_DIM), F32)],
        compiler_params=_cparams(("arbitrary", "arbitrary")),
        name="token_mixers",
    )(lq1, lk1, lq2, lk2, subg, qd, kd, vt, kd_c, vt_c, dec_f, dec_b, gn_g, qr, kr, vr, gr, kr_c, vr_c)


def _mlp_kernel(x_ref, od_ref, or_ref, g1_ref, sh2_ref, sc2_ref, g2_ref, n2_ref, fg_ref,
                wo_ref, w1_ref, w2_ref, o_ref):
    half = od_ref.shape[1]
    y = (jnp.dot(od_ref[...], wo_ref[0:half, :].astype(BF16), preferred_element_type=F32)
         + jnp.dot(or_ref[...], wo_ref[half:2 * half, :].astype(BF16), preferred_element_type=F32))
    x1 = x_ref[...] + g1_ref[...] * y
    h = _rmsnorm(x1, n2_ref[...]) * (1.0 + sc2_ref[...]) + sh2_ref[...]
    hb = h.astype(BF16)
    d_ff = w1_ref.shape[1]
    acc = None
    for f0 in range(0, d_ff, FF_CHUNK):
        u = jnp.dot(hb, w1_ref[:, f0:f0 + FF_CHUNK].astype(BF16), preferred_element_type=F32)
        u = jnp.square(jnp.maximum(u, 0.0)).astype(BF16)
        part = jnp.dot(u, w2_ref[f0:f0 + FF_CHUNK, :].astype(BF16), preferred_element_type=F32)
        acc = part if acc is None else acc + part
    x2 = x1 + g2_ref[...] * acc
    o_ref[...] = _rmsnorm(x2, fg_ref[...])


def _out_mlp(x, od, orr, g1, sh2, sc2, g2, n2, fg, wo, w1, w2):
    b, n, d = x.shape
    tm = MLP_TM
    tok = lambda w: pl.BlockSpec((None, tm, w), lambda bi, t: (bi, t, 0))
    row = pl.BlockSpec((None, 1, d), lambda bi, t: (bi, 0, 0))
    gain = pl.BlockSpec((1, d), lambda bi, t: (0, 0))
    resident = lambda a: pl.BlockSpec(a.shape, lambda bi, t: (0, 0), pipeline_mode=pl.Buffered(1))
    return pl.pallas_call(
        _mlp_kernel,
        grid=(b, n // tm),
        in_specs=[tok(d), tok(od.shape[2]), tok(orr.shape[2]), row, row, row, row, gain, gain,
                  resident(wo), resident(w1), resident(w2)],
        out_specs=tok(d),
        out_shape=jax.ShapeDtypeStruct((b, n, d), F32),
        compiler_params=_cparams(("arbitrary", "arbitrary")),
        name="out_mlp",
    )(x, od, orr, g1, sh2, sc2, g2, n2, fg, wo, w1, w2)


def kernel(x, c, ctx, c_ctx, w_ada, b_ada, norm1_g, norm2_g, w_in, lambda_q1, lambda_k1, lambda_q2,
           lambda_k2, diff_subln_g, ret_decay_fwd, ret_decay_bwd, ret_gn_g, w_out, w_mlp1, w_mlp2, final_g):
    assert w_ada.shape[0] == 1, "single-layer block"
    b, n, d = x.shape

    c_rows = jnp.concatenate([c, c_ctx[None], jnp.zeros((8 - b - 1, d), F32)], axis=0)
    mod = _adaln(c_rows, w_ada[0], b_ada).reshape(8, N_MOD, d)
    sh1, sc1, g1, sh2, sc2, g2 = [mod[:b, i][:, None, :] for i in range(N_MOD)]

    qd, kd, vt, qr, kr, vr, gr = _project_latent(x, sh1, sc1, norm1_g, w_in[0], _rope_tables(n))
    kd_c, vt_c, kr_c, vr_c = _project_ctx(ctx, mod[b, 0][None], mod[b, 1][None], norm1_g, w_in[0])

    od, orr = _token_mixers(lambda_q1, lambda_k1, lambda_q2, lambda_k2, diff_subln_g, qd, kd, vt, kd_c, vt_c,
                            ret_decay_fwd, ret_decay_bwd, ret_gn_g, qr, kr, vr, gr, kr_c, vr_c)

    return _out_mlp(x, od, orr, g1, sh2, sc2, g2, norm2_g, final_g[None],
                    w_out[0], w_mlp1[0], w_mlp2[0])
```

```python
import math

import numpy as np
import jax
import jax.numpy as jnp
from jax import lax
from jax.experimental import pallas as pl
from jax.experimental.pallas import tpu as pltpu

F32 = jnp.float32
BF16 = jnp.bfloat16

GRID_W = 64
HEADS = 4
HEAD_DIM = 64
V_DIM = 128
VT_ROWS = V_DIM + 16
ROPE_PAIRS = 16
ROPE_BASE = 10000.0
N_MOD = 6
EPS = 1e-6
LAM_INIT = 0.8 - 0.6 * math.exp(-0.3 * 0)
LOG2E = 1.4426950408889634
QK_SCALE = HEAD_DIM ** -0.5

LANES = 128
MXU_DIM = 256
VMEM_LIMIT_BYTES = 56 * 1024 * 1024

PROJ_TM = 1024
ATTN_QB = 256
ATTN_GROUPS = 8
RET_C = 256
MLP_TM = 512
FF_CHUNK = 1024
ADA_TN = 1536

W_DQK = HEADS * 2 * HEAD_DIM
W_RQK = HEADS * HEAD_DIM
W_V = HEADS * V_DIM


def _col_ranges(*widths):
    edges = [sum(widths[:i]) for i in range(len(widths) + 1)]
    return list(zip(edges[:-1], edges[1:]))


COL_QD, COL_KD, COL_VD, COL_QR, COL_KR, COL_VR, COL_GR = _col_ranges(W_DQK, W_DQK, W_V, W_RQK, W_RQK, W_V, W_V)


def _cparams(sem):
    return pltpu.CompilerParams(dimension_semantics=sem, vmem_limit_bytes=VMEM_LIMIT_BYTES)


def _rmsnorm(xf, g):
    return xf * lax.rsqrt(jnp.mean(xf * xf, axis=-1, keepdims=True) + EPS) * g


def _silu(x):
    return x * (1.0 / (1.0 + jnp.exp(-x)))


def _adaln_kernel(c_ref, w_ref, b_ref, o_ref):
    a = _silu(c_ref[...]).astype(BF16)
    o_ref[...] = jnp.dot(a, w_ref[...].astype(BF16), preferred_element_type=F32) + b_ref[...]


def _adaln(c_rows, w_ada, b_ada):
    r, d = c_rows.shape
    n_out = w_ada.shape[1]
    return pl.pallas_call(
        _adaln_kernel,
        grid=(n_out // ADA_TN,),
        in_specs=[pl.BlockSpec((r, d), lambda j: (0, 0)),
                  pl.BlockSpec((d, ADA_TN), lambda j: (0, j)),
                  pl.BlockSpec((1, ADA_TN), lambda j: (0, j))],
        out_specs=pl.BlockSpec((r, ADA_TN), lambda j: (0, j)),
        out_shape=jax.ShapeDtypeStruct((r, n_out), F32),
        compiler_params=_cparams(("arbitrary",)),
        name="adaln",
    )(c_rows, w_ada, b_ada)


def _rope_tables(n_tokens):
    rows = n_tokens // GRID_W
    row = np.repeat(np.arange(rows), GRID_W).astype(np.float64)
    col = np.tile(np.arange(GRID_W), rows).astype(np.float64)
    inv = (np.float32(ROPE_BASE) ** (-np.arange(ROPE_PAIRS, dtype=np.float32) / ROPE_PAIRS)).astype(np.float64)
    ang_r = row[:, None] * inv
    ang_c = col[:, None] * inv
    zeros = np.zeros_like(ang_r)
    cos64 = np.concatenate([np.cos(ang_r)] * 2 + [np.cos(ang_c)] * 2, axis=1)
    sa64 = np.concatenate([-np.sin(ang_r), zeros, -np.sin(ang_c), zeros], axis=1)
    sb64 = np.concatenate([zeros, np.sin(ang_r), zeros, np.sin(ang_c)], axis=1)
    dup = lambda t: jnp.asarray(np.concatenate([t, t], axis=1), dtype=F32)
    return dup(cos64), dup(sa64), dup(sb64)


def _modulated(x_ref, sh_ref, sc_ref, g_ref):
    h = _rmsnorm(x_ref[...], g_ref[...])
    return (h * (1.0 + sc_ref[...]) + sh_ref[...]).astype(BF16)


def _proj_latent_kernel(x_ref, sh_ref, sc_ref, g_ref, w_ref, c_ref, sa_ref, sb_ref,
                        qd_ref, kd_ref, vt_ref, qr_ref, kr_ref, vr_ref, gr_ref):
    hb = _modulated(x_ref, sh_ref, sc_ref, g_ref)
    c, sa, sb = c_ref[...], sa_ref[...], sb_ref[...]

    def project(cols):
        return jnp.dot(hb, w_ref[:, cols[0]:cols[1]].astype(BF16), preferred_element_type=F32)

    def rotate(y):
        slabs = []
        for s in range(y.shape[1] // LANES):
            ys = y[:, s * LANES:(s + 1) * LANES]
            slabs.append(ys * c + pltpu.roll(ys, LANES - ROPE_PAIRS, 1) * sa
                         + pltpu.roll(ys, ROPE_PAIRS, 1) * sb)
        return jnp.concatenate(slabs, axis=1)

    qd_ref[...] = (rotate(project(COL_QD)) * (QK_SCALE * LOG2E)).astype(BF16)
    kd_ref[...] = rotate(project(COL_KD)).astype(BF16)
    vd = project(COL_VD)
    ones = jnp.ones((VT_ROWS - V_DIM, vd.shape[0]), BF16)
    for h in range(HEADS):
        vt_ref[h, 0:V_DIM, :] = vd[:, h * V_DIM:(h + 1) * V_DIM].T.astype(BF16)
        vt_ref[h, V_DIM:VT_ROWS, :] = ones
    gr_ref[...] = _silu(project(COL_GR)).astype(BF16)
    qr_ref[...] = rotate(project(COL_QR)).astype(BF16)
    kr_ref[...] = (rotate(project(COL_KR)) * QK_SCALE).astype(BF16)
    vr_ref[...] = project(COL_VR).astype(BF16)


def _project_latent(x, sh, sc, g, w_in, tables):
    b, n, d = x.shape
    tm = PROJ_TM
    tok = lambda w: pl.BlockSpec((None, tm, w), lambda t, bi: (bi, t, 0))
    row = pl.BlockSpec((None, 1, d), lambda t, bi: (bi, 0, 0))
    table = pl.BlockSpec((tm, LANES), lambda t, bi: (t, 0))
    out_specs = [tok(W_DQK), tok(W_DQK),
                 pl.BlockSpec((None, HEADS, VT_ROWS, tm), lambda t, bi: (bi, 0, 0, t)),
                 tok(W_RQK), tok(W_RQK), tok(W_V), tok(W_V)]
    out_shape = [jax.ShapeDtypeStruct((b, n, W_DQK), BF16),
                 jax.ShapeDtypeStruct((b, n, W_DQK), BF16),
                 jax.ShapeDtypeStruct((b, HEADS, VT_ROWS, n), BF16),
                 jax.ShapeDtypeStruct((b, n, W_RQK), BF16),
                 jax.ShapeDtypeStruct((b, n, W_RQK), BF16),
                 jax.ShapeDtypeStruct((b, n, W_V), BF16),
                 jax.ShapeDtypeStruct((b, n, W_V), BF16)]
    return pl.pallas_call(
        _proj_latent_kernel,
        grid=(n // tm, b),
        in_specs=[tok(d), row, row, pl.BlockSpec((1, d), lambda t, bi: (0, 0)),
                  pl.BlockSpec(w_in.shape, lambda t, bi: (0, 0), pipeline_mode=pl.Buffered(1)),
                  table, table, table],
        out_specs=out_specs,
        out_shape=out_shape,
        compiler_params=_cparams(("arbitrary", "arbitrary")),
        name="proj_latent",
    )(x, sh, sc, g, w_in, *tables)


def _proj_ctx_kernel(x_ref, sh_ref, sc_ref, g_ref, w_ref, kd_ref, vt_ref, kr_ref, vr_ref):
    hb = _modulated(x_ref, sh_ref, sc_ref, g_ref)
    b, nctx = kd_ref.shape[0], kd_ref.shape[1]

    def project(cols):
        return jnp.dot(hb, w_ref[:, cols[0]:cols[1]].astype(BF16), preferred_element_type=F32)

    kd, vd, kr, vr = project(COL_KD), project(COL_VD), project(COL_KR) * QK_SCALE, project(COL_VR)
    for bi in range(b):
        rows = slice(bi * nctx, (bi + 1) * nctx)
        kd_ref[bi] = kd[rows].astype(BF16)
        kr_ref[bi] = kr[rows].astype(BF16)
        vr_ref[bi] = vr[rows].astype(BF16)
        for h in range(HEADS):
            vt_ref[bi, h, 0:V_DIM, :] = vd[rows, h * V_DIM:(h + 1) * V_DIM].T.astype(BF16)
            vt_ref[bi, h, V_DIM:VT_ROWS, :] = jnp.ones((VT_ROWS - V_DIM, nctx), BF16)


def _project_ctx(ctx, sh, sc, g, w_in):
    b, nctx, d = ctx.shape
    full = lambda shape: pl.BlockSpec(shape, lambda i: (0,) * len(shape))
    x2 = ctx.reshape(b * nctx, d)
    shapes = [(b, nctx, W_DQK), (b, HEADS, VT_ROWS, nctx), (b, nctx, W_RQK), (b, nctx, W_V)]
    return pl.pallas_call(
        _proj_ctx_kernel,
        grid=(1,),
        in_specs=[full(a.shape) for a in (x2, sh, sc, g, w_in)],
        out_specs=[full(shape) for shape in shapes],
        out_shape=[jax.ShapeDtypeStruct(shape, BF16) for shape in shapes],
        compiler_params=_cparams(("arbitrary",)),
        name="proj_ctx",
    )(x2, sh, sc, g, w_in)


def _mixer_kernel(lq1_ref, lk1_ref, lq2_ref, lk2_ref, subg_ref, q_ref, k_ref, vt_ref, kc_ref, vtc_ref,
                  decf_ref, decb_ref, gn_ref, rq_ref, rk_ref, rv_ref, rg_ref, rkc_ref, rvc_ref,
                  o_ref, ro_ref, sa_scr, sb_scr, rsb_scr, rsf_scr):
    n = q_ref.shape[0]
    head = pl.program_id(1)
    lane = lax.broadcasted_iota(jnp.int32, (1, LANES), 1)

    lam = (jnp.exp(jnp.sum(lq1_ref[...] * lk1_ref[...], axis=-1, keepdims=True))
           - jnp.exp(jnp.sum(lq2_ref[...] * lk2_ref[...], axis=-1, keepdims=True)) + LAM_INIT)
    first_map = lane < HEAD_DIM
    subg = subg_ref[...]
    nq = n // ATTN_QB

    nctx = kc_ref.shape[0]
    gsz = n // ATTN_GROUPS
    groups = [[(k_ref, vt_ref, g * gsz, (g + 1) * gsz, nctx + g * gsz)] for g in range(ATTN_GROUPS)]
    groups[0].insert(0, (kc_ref, vtc_ref, 0, nctx, 0))

    def scores(i, s_scr):
        r0 = pl.multiple_of(i * ATTN_QB, ATTN_QB)
        q = q_ref[pl.ds(r0, ATTN_QB), :]
        zero = jnp.zeros_like(q)
        qms = (jnp.where(first_map, q, zero), jnp.where(first_map, zero, q))
        ms = [None, None]
        for pieces in groups:
            for kr, _, lo, hi, dst in pieces:
                for mp in range(2):
                    s = lax.dot_general(kr[lo:hi, :], qms[mp], (((1,), (1,)), ((), ())),
                                        preferred_element_type=F32)
                    s_scr[mp, dst:dst + hi - lo, :] = s
                    mg = jnp.max(s, axis=0, keepdims=True)
                    ms[mp] = mg if ms[mp] is None else jnp.maximum(ms[mp], mg)
            yield tuple(ms)

    def finish(i, s_scr, ms):
        accs = [None, None]
        for g, pieces in enumerate(groups):
            for _, vr, lo, hi, dst in pieces:
                for mp in range(2):
                    p = jnp.exp2(s_scr[mp, dst:dst + hi - lo, :] - ms[mp]).astype(BF16)
                    part = jnp.dot(vr[:, lo:hi], p, preferred_element_type=F32)
                    accs[mp] = part if accs[mp] is None else accs[mp] + part
            if g == len(groups) - 1:
                ots = [a[0:V_DIM, :] / a[V_DIM:V_DIM + 1, :] for a in accs]
                o = (ots[0] - lam * ots[1]).T
                o = _rmsnorm(o, subg) * (1.0 - LAM_INIT)
                r0 = pl.multiple_of(i * ATTN_QB, ATTN_QB)
                o_ref[pl.ds(r0, ATTN_QB), :] = o.astype(o_ref.dtype)
            yield None

    c = RET_C
    nc = n // c

    def log_gamma(dec):
        z = -dec
        return -(jnp.maximum(z, 0.0) + jnp.log(1.0 + jnp.exp(-jnp.abs(z))))

    hsel = lax.broadcasted_iota(jnp.int32, (1, HEADS), 1) == head

    def pick(vec):
        return jnp.sum(jnp.where(hsel, vec, 0.0), axis=-1, keepdims=True)

    lgf, lgb = pick(log_gamma(decf_ref[...])), pick(log_gamma(decb_ref[...]))
    ii = lax.broadcasted_iota(jnp.int32, (c, c), 0).astype(F32)
    jj = lax.broadcasted_iota(jnp.int32, (c, c), 1).astype(F32)
    rel = ii - jj
    pos = lax.broadcasted_iota(jnp.int32, (c, LANES), 0).astype(F32)
    decay = (jnp.where(rel >= 0, jnp.exp(jnp.maximum(rel, 0.0) * lgf), 0.0)
             + jnp.where(rel <= 0, jnp.exp(jnp.maximum(-rel, 0.0) * lgb), 0.0))
    qdec_f, qdec_b = jnp.exp((pos + 1.0) * lgf), jnp.exp((c - pos) * lgb)
    kdec_f, kdec_b = jnp.exp((c - 1.0 - pos) * lgf), jnp.exp(pos * lgb)
    cdec_f, cdec_b = jnp.exp(c * lgf), jnp.exp(c * lgb)
    rmask = (lane >= HEAD_DIM).astype(jnp.int32) == head % 2
    gn = gn_ref[...]

    def kv_state(k, v, kdec):
        kd = (k.astype(F32) * kdec).astype(BF16)
        return lax.dot_general(kd, v, (((0,), (0,)), ((), ())), preferred_element_type=F32)

    rsf_scr[...] = kv_state(rkc_ref[...], rvc_ref[...], kdec_f)

    def backward_states():
        sb = kv_state(rkc_ref[...], rvc_ref[...], kdec_b)
        for ci in reversed(range(nc)):
            rsb_scr[ci] = sb.astype(BF16)
            if ci > 0:
                sb = sb * cdec_b + kv_state(rk_ref[ci * c:(ci + 1) * c, :], rv_ref[ci * c:(ci + 1) * c, :], kdec_b)
            yield None

    def ret_chunk(ci):
        r0 = pl.multiple_of(ci * c, c)
        q = rq_ref[pl.ds(r0, c), :]
        k = rk_ref[pl.ds(r0, c), :]
        v = rv_ref[pl.ds(r0, c), :]
        qm = jnp.where(rmask, q, jnp.zeros_like(q))
        sc = lax.dot_general(qm, k, (((1,), (1,)), ((), ())), preferred_element_type=F32)
        a = (sc * decay).astype(BF16)
        yield None
        sf = rsf_scr[...]
        o = jnp.dot(a, v, preferred_element_type=F32)
        o = o + jnp.dot(qm, sf.astype(BF16), preferred_element_type=F32) * qdec_f
        o = o + jnp.dot(qm, rsb_scr[ci], preferred_element_type=F32) * qdec_b
        yield None
        mu = jnp.mean(o, axis=-1, keepdims=True)
        var = jnp.mean(jnp.square(o - mu), axis=-1, keepdims=True)
        on = (o - mu) * lax.rsqrt(var + EPS) * gn
        gate = rg_ref[pl.ds(r0, c), :].astype(F32)
        ro_ref[pl.ds(r0, c), :] = (on * gate).astype(ro_ref.dtype)
        yield None
        rsf_scr[...] = sf * cdec_f + kv_state(k, v, kdec_f)
        yield None

    def run(*stages, side=None, side_steps=4 / ATTN_GROUPS):
        last = [None] * len(stages)
        done = 0
        for t, vals in enumerate(zip(*stages)):
            last = list(vals)
            while side is not None and done < (t + 1) * side_steps:
                next(side)
                done += 1
        return last

    def step(j, m_a):
        m_b, _ = run(scores(2 * j + 1, sb_scr), finish(2 * j, sa_scr, m_a), side=ret_chunk(2 * j))
        m_a, _ = run(scores(2 * j + 2, sa_scr), finish(2 * j + 1, sb_scr, m_b), side=ret_chunk(2 * j + 1))
        return m_a

    (m_a,) = run(scores(0, sa_scr), side=backward_states(), side_steps=nc / ATTN_GROUPS)
    m_a = lax.fori_loop(0, nq // 2 - 1, step, m_a)
    m_b, _ = run(scores(nq - 1, sb_scr), finish(nq - 2, sa_scr, m_a), side=ret_chunk(nq - 2))
    run(finish(nq - 1, sb_scr, m_b), side=ret_chunk(nq - 1))


def _token_mixers(lq1, lk1, lq2, lk2, subg, qd, kd, vt, kd_c, vt_c,
                  dec_f, dec_b, gn_g, qr, kr, vr, gr, kr_c, vr_c):
    b, n, _ = qd.shape
    nctx = kd_c.shape[1]
    assert (n // ATTN_QB) % 2 == 0 and n % (ATTN_GROUPS * MXU_DIM) == 0 and nctx % MXU_DIM == 0
    assert ATTN_GROUPS >= 4 and nctx == RET_C and RET_C == ATTN_QB
    vec = lambda w: pl.BlockSpec((1, w), lambda bi, h: (0, 0))
    head = lambda rows: pl.BlockSpec((None, rows, LANES), lambda bi, h: (bi, 0, h))
    pair = lambda rows: pl.BlockSpec((None, rows, LANES), lambda bi, h: (bi, 0, h // 2))
    headt = lambda cols: pl.BlockSpec((None, None, VT_ROWS, cols), lambda bi, h: (bi, h, 0, 0))
    out = jax.ShapeDtypeStruct((b, n, HEADS * V_DIM), BF16)
    return pl.pallas_call(
        _mixer_kernel,
        grid=(b, HEADS),
        in_specs=[vec(HEAD_DIM)] * 4 + [vec(V_DIM), head(n), head(n), headt(n), head(nctx), headt(nctx),
                  vec(HEADS), vec(HEADS), pl.BlockSpec((1, V_DIM), lambda bi, h: (0, h)),
                  pair(n), pair(n), head(n), head(n), pair(nctx), head(nctx)],
        out_specs=[head(n), head(n)],
        out_shape=[out, out],
        scratch_shapes=[pltpu.VMEM((2, n + nctx, ATTN_QB), F32)] * 2
        + [pltpu.VMEM((n // RET_C, LANES, V_DIM), BF16), pltpu.VMEM((LANES, V_DIM), F32)],
        compiler_params=_cparams(("arbitrary", "arbitrary")),
        name="token_mixers",
    )(lq1, lk1, lq2, lk2, subg, qd, kd, vt, kd_c, vt_c, dec_f, dec_b, gn_g, qr, kr, vr, gr, kr_c, vr_c)


def _mlp_kernel(x_ref, od_ref, or_ref, g1_ref, sh2_ref, sc2_ref, g2_ref, n2_ref, fg_ref,
                wo_ref, w1_ref, w2_ref, o_ref):
    half = od_ref.shape[1]
    y = (jnp.dot(od_ref[...], wo_ref[0:half, :].astype(BF16), preferred_element_type=F32)
         + jnp.dot(or_ref[...], wo_ref[half:2 * half, :].astype(BF16), preferred_element_type=F32))
    x1 = x_ref[...] + g1_ref[...] * y
    h = _rmsnorm(x1, n2_ref[...]) * (1.0 + sc2_ref[...]) + sh2_ref[...]
    hb = h.astype(BF16)
    d_ff = w1_ref.shape[1]
    acc = None
    for f0 in range(0, d_ff, FF_CHUNK):
        u = jnp.dot(hb, w1_ref[:, f0:f0 + FF_CHUNK].astype(BF16), preferred_element_type=F32)
        u = jnp.square(jnp.maximum(u, 0.0)).astype(BF16)
        part = jnp.dot(u, w2_ref[f0:f0 + FF_CHUNK, :].astype(BF16), preferred_element_type=F32)
        acc = part if acc is None else acc + part
    x2 = x1 + g2_ref[...] * acc
    o_ref[...] = _rmsnorm(x2, fg_ref[...])


def _out_mlp(x, od, orr, g1, sh2, sc2, g2, n2, fg, wo, w1, w2):
    b, n, d = x.shape
    tm = MLP_TM
    tok = lambda w: pl.BlockSpec((None, tm, w), lambda bi, t: (bi, t, 0))
    row = pl.BlockSpec((None, 1, d), lambda bi, t: (bi, 0, 0))
    gain = pl.BlockSpec((1, d), lambda bi, t: (0, 0))
    resident = lambda a: pl.BlockSpec(a.shape, lambda bi, t: (0, 0), pipeline_mode=pl.Buffered(1))
    return pl.pallas_call(
        _mlp_kernel,
        grid=(b, n // tm),
        in_specs=[tok(d), tok(od.shape[2]), tok(orr.shape[2]), row, row, row, row, gain, gain,
                  resident(wo), resident(w1), resident(w2)],
        out_specs=tok(d),
        out_shape=jax.ShapeDtypeStruct((b, n, d), F32),
        compiler_params=_cparams(("arbitrary", "arbitrary")),
        name="out_mlp",
    )(x, od, orr, g1, sh2, sc2, g2, n2, fg, wo, w1, w2)


def kernel(x, c, ctx, c_ctx, w_ada, b_ada, norm1_g, norm2_g, w_in, lambda_q1, lambda_k1, lambda_q2,
           lambda_k2, diff_subln_g, ret_decay_fwd, ret_decay_bwd, ret_gn_g, w_out, w_mlp1, w_mlp2, final_g):
    assert w_ada.shape[0] == 1, "single-layer block"
    b, n, d = x.shape

    c_rows = jnp.concatenate([c, c_ctx[None], jnp.zeros((8 - b - 1, d), F32)], axis=0)
    mod = _adaln(c_rows, w_ada[0], b_ada).reshape(8, N_MOD, d)
    sh1, sc1, g1, sh2, sc2, g2 = [mod[:b, i][:, None, :] for i in range(N_MOD)]

    qd, kd, vt, qr, kr, vr, gr = _project_latent(x, sh1, sc1, norm1_g, w_in[0], _rope_tables(n))
    kd_c, vt_c, kr_c, vr_c = _project_ctx(ctx, mod[b, 0][None], mod[b, 1][None], norm1_g, w_in[0])

    od, orr = _token_mixers(lambda_q1, lambda_k1, lambda_q2, lambda_k2, diff_subln_g, qd, kd, vt, kd_c, vt_c,
                            ret_decay_fwd, ret_decay_bwd, ret_gn_g, qr, kr, vr, gr, kr_c, vr_c)

    return _out_mlp(x, od, orr, g1, sh2, sc2, g2, norm2_g, final_g[None],
                    w_out[0], w_mlp1[0], w_mlp2[0])
```

```python
import math

import numpy as np
import jax
import jax.numpy as jnp
from jax import lax
from jax.experimental import pallas as pl
from jax.experimental.pallas import tpu as pltpu

F32 = jnp.float32
BF16 = jnp.bfloat16

GRID_W = 64
HEADS = 4
HEAD_DIM = 64
V_DIM = 128
VT_ROWS = V_DIM + 16
ROPE_PAIRS = 16
ROPE_BASE = 10000.0
N_MOD = 6
EPS = 1e-6
LAM_INIT = 0.8 - 0.6 * math.exp(-0.3 * 0)
LOG2E = 1.4426950408889634
QK_SCALE = HEAD_DIM ** -0.5

LANES = 128
MXU_DIM = 256
VMEM_LIMIT_BYTES = 56 * 1024 * 1024

PROJ_TM = 1024
ATTN_QB = 256
ATTN_GROUPS = 8
RET_C = 256
MLP_TM = 512
FF_CHUNK = 1024
ADA_TN = 1536

W_DQK = HEADS * 2 * HEAD_DIM
W_RQK = HEADS * HEAD_DIM
W_V = HEADS * V_DIM


def _col_ranges(*widths):
    edges = [sum(widths[:i]) for i in range(len(widths) + 1)]
    return list(zip(edges[:-1], edges[1:]))


COL_QD, COL_KD, COL_VD, COL_QR, COL_KR, COL_VR, COL_GR = _col_ranges(W_DQK, W_DQK, W_V, W_RQK, W_RQK, W_V, W_V)


def _cparams(sem):
    return pltpu.CompilerParams(dimension_semantics=sem, vmem_limit_bytes=VMEM_LIMIT_BYTES)


def _rmsnorm(xf, g):
    return xf * lax.rsqrt(jnp.mean(xf * xf, axis=-1, keepdims=True) + EPS) * g


def _silu(x):
    return x * (1.0 / (1.0 + jnp.exp(-x)))


def _adaln_kernel(c_ref, w_ref, b_ref, o_ref):
    a = _silu(c_ref[...]).astype(BF16)
    o_ref[...] = jnp.dot(a, w_ref[...].astype(BF16), preferred_element_type=F32) + b_ref[...]


def _adaln(c_rows, w_ada, b_ada):
    r, d = c_rows.shape
    n_out = w_ada.shape[1]
    return pl.pallas_call(
        _adaln_kernel,
        grid=(n_out // ADA_TN,),
        in_specs=[pl.BlockSpec((r, d), lambda j: (0, 0)),
                  pl.BlockSpec((d, ADA_TN), lambda j: (0, j)),
                  pl.BlockSpec((1, ADA_TN), lambda j: (0, j))],
        out_specs=pl.BlockSpec((r, ADA_TN), lambda j: (0, j)),
        out_shape=jax.ShapeDtypeStruct((r, n_out), F32),
        compiler_params=_cparams(("arbitrary",)),
        name="adaln",
    )(c_rows, w_ada, b_ada)


def _rope_tables(n_tokens):
    rows = n_tokens // GRID_W
    row = np.repeat(np.arange(rows), GRID_W).astype(np.float64)
    col = np.tile(np.arange(GRID_W), rows).astype(np.float64)
    inv = (np.float32(ROPE_BASE) ** (-np.arange(ROPE_PAIRS, dtype=np.float32) / ROPE_PAIRS)).astype(np.float64)
    ang_r = row[:, None] * inv
    ang_c = col[:, None] * inv
    zeros = np.zeros_like(ang_r)
    cos64 = np.concatenate([np.cos(ang_r)] * 2 + [np.cos(ang_c)] * 2, axis=1)
    sa64 = np.concatenate([-np.sin(ang_r), zeros, -np.sin(ang_c), zeros], axis=1)
    sb64 = np.concatenate([zeros, np.sin(ang_r), zeros, np.sin(ang_c)], axis=1)
    dup = lambda t: jnp.asarray(np.concatenate([t, t], axis=1), dtype=F32)
    return dup(cos64), dup(sa64), dup(sb64)


def _modulated(x_ref, sh_ref, sc_ref, g_ref):
    h = _rmsnorm(x_ref[...], g_ref[...])
    return (h * (1.0 + sc_ref[...]) + sh_ref[...]).astype(BF16)


def _proj_latent_kernel(x_ref, sh_ref, sc_ref, g_ref, w_ref, c_ref, sa_ref, sb_ref,
                        qd_ref, kd_ref, vt_ref, qr_ref, kr_ref, vr_ref, gr_ref):
    hb = _modulated(x_ref, sh_ref, sc_ref, g_ref)
    c, sa, sb = c_ref[...], sa_ref[...], sb_ref[...]

    def project(cols):
        return jnp.dot(hb, w_ref[:, cols[0]:cols[1]].astype(BF16), preferred_element_type=F32)

    def rotate(y):
        slabs = []
        for s in range(y.shape[1] // LANES):
            ys = y[:, s * LANES:(s + 1) * LANES]
            slabs.append(ys * c + pltpu.roll(ys, LANES - ROPE_PAIRS, 1) * sa
                         + pltpu.roll(ys, ROPE_PAIRS, 1) * sb)
        return jnp.concatenate(slabs, axis=1)

    qd_ref[...] = (rotate(project(COL_QD)) * (QK_SCALE * LOG2E)).astype(BF16)
    kd_ref[...] = rotate(project(COL_KD)).astype(BF16)
    vd = project(COL_VD)
    ones = jnp.ones((VT_ROWS - V_DIM, vd.shape[0]), BF16)
    for h in range(HEADS):
        vt_ref[h, 0:V_DIM, :] = vd[:, h * V_DIM:(h + 1) * V_DIM].T.astype(BF16)
        vt_ref[h, V_DIM:VT_ROWS, :] = ones
    gr_ref[...] = _silu(project(COL_GR)).astype(BF16)
    qr_ref[...] = rotate(project(COL_QR)).astype(BF16)
    kr_ref[...] = (rotate(project(COL_KR)) * QK_SCALE).astype(BF16)
    vr_ref[...] = project(COL_VR).astype(BF16)


def _project_latent(x, sh, sc, g, w_in, tables):
    b, n, d = x.shape
    tm = PROJ_TM
    tok = lambda w: pl.BlockSpec((None, tm, w), lambda t, bi: (bi, t, 0))
    row = pl.BlockSpec((None, 1, d), lambda t, bi: (bi, 0, 0))
    table = pl.BlockSpec((tm, LANES), lambda t, bi: (t, 0))
    out_specs = [tok(W_DQK), tok(W_DQK),
                 pl.BlockSpec((None, HEADS, VT_ROWS, tm), lambda t, bi: (bi, 0, 0, t)),
                 tok(W_RQK), tok(W_RQK), tok(W_V), tok(W_V)]
    out_shape = [jax.ShapeDtypeStruct((b, n, W_DQK), BF16),
                 jax.ShapeDtypeStruct((b, n, W_DQK), BF16),
                 jax.ShapeDtypeStruct((b, HEADS, VT_ROWS, n), BF16),
                 jax.ShapeDtypeStruct((b, n, W_RQK), BF16),
                 jax.ShapeDtypeStruct((b, n, W_RQK), BF16),
                 jax.ShapeDtypeStruct((b, n, W_V), BF16),
                 jax.ShapeDtypeStruct((b, n, W_V), BF16)]
    return pl.pallas_call(
        _proj_latent_kernel,
        grid=(n // tm, b),
        in_specs=[tok(d), row, row, pl.BlockSpec((1, d), lambda t, bi: (0, 0)),
                  pl.BlockSpec(w_in.shape, lambda t, bi: (0, 0), pipeline_mode=pl.Buffered(1)),
                  table, table, table],
        out_specs=out_specs,
        out_shape=out_shape,
        compiler_params=_cparams(("arbitrary", "arbitrary")),
        name="proj_latent",
    )(x, sh, sc, g, w_in, *tables)


def _proj_ctx_kernel(x_ref, sh_ref, sc_ref, g_ref, w_ref, kd_ref, vt_ref, kr_ref, vr_ref):
    hb = _modulated(x_ref, sh_ref, sc_ref, g_ref)
    b, nctx = kd_ref.shape[0], kd_ref.shape[1]

    def project(cols):
        return jnp.dot(hb, w_ref[:, cols[0]:cols[1]].astype(BF16), preferred_element_type=F32)

    kd, vd, kr, vr = project(COL_KD), project(COL_VD), project(COL_KR) * QK_SCALE, project(COL_VR)
    for bi in range(b):
        rows = slice(bi * nctx, (bi + 1) * nctx)
        kd_ref[bi] = kd[rows].astype(BF16)
        kr_ref[bi] = kr[rows].astype(BF16)
        vr_ref[bi] = vr[rows].astype(BF16)
        for h in range(HEADS):
            vt_ref[bi, h, 0:V_DIM, :] = vd[rows, h * V_DIM:(h + 1) * V_DIM].T.astype(BF16)
            vt_ref[bi, h, V_DIM:VT_ROWS, :] = jnp.ones((VT_ROWS - V_DIM, nctx), BF16)


def _project_ctx(ctx, sh, sc, g, w_in):
    b, nctx, d = ctx.shape
    full = lambda shape: pl.BlockSpec(shape, lambda i: (0,) * len(shape))
    x2 = ctx.reshape(b * nctx, d)
    shapes = [(b, nctx, W_DQK), (b, HEADS, VT_ROWS, nctx), (b, nctx, W_RQK), (b, nctx, W_V)]
    return pl.pallas_call(
        _proj_ctx_kernel,
        grid=(1,),
        in_specs=[full(a.shape) for a in (x2, sh, sc, g, w_in)],
        out_specs=[full(shape) for shape in shapes],
        out_shape=[jax.ShapeDtypeStruct(shape, BF16) for shape in shapes],
        compiler_params=_cparams(("arbitrary",)),
        name="proj_ctx",
    )(x2, sh, sc, g, w_in)


def _mixer_kernel(lq1_ref, lk1_ref, lq2_ref, lk2_ref, subg_ref, q_ref, k_ref, vt_ref, kc_ref, vtc_ref,
                  decf_ref, decb_ref, gn_ref, rq_ref, rk_ref, rv_ref, rg_ref, rkc_ref, rvc_ref,
                  o_ref, ro_ref, sa_scr, sb_scr, rsb_scr, rsf_scr, acc_scr):
    n = q_ref.shape[0]
    head = pl.program_id(1)
    lane = lax.broadcasted_iota(jnp.int32, (1, LANES), 1)

    lam = (jnp.exp(jnp.sum(lq1_ref[...] * lk1_ref[...], axis=-1, keepdims=True))
           - jnp.exp(jnp.sum(lq2_ref[...] * lk2_ref[...], axis=-1, keepdims=True)) + LAM_INIT)
    first_map = lane < HEAD_DIM
    subg = subg_ref[...]
    nq = n // ATTN_QB

    nctx = kc_ref.shape[0]
    gsz = n // ATTN_GROUPS
    groups = [[(k_ref, vt_ref, g * gsz, (g + 1) * gsz, nctx + g * gsz)] for g in range(ATTN_GROUPS)]
    groups[0].insert(0, (kc_ref, vtc_ref, 0, nctx, 0))

    def scores(i, s_scr):
        r0 = pl.multiple_of(i * ATTN_QB, ATTN_QB)
        q = q_ref[pl.ds(r0, ATTN_QB), :]
        zero = jnp.zeros_like(q)
        qms = (jnp.where(first_map, q, zero), jnp.where(first_map, zero, q))
        ms = [None, None]
        for pieces in groups:
            for kr, _, lo, hi, dst in pieces:
                for mp in range(2):
                    s = lax.dot_general(kr[lo:hi, :], qms[mp], (((1,), (1,)), ((), ())),
                                        preferred_element_type=F32)
                    s_scr[mp, dst:dst + hi - lo, :] = s
                    mg = jnp.max(s, axis=0, keepdims=True)
                    ms[mp] = mg if ms[mp] is None else jnp.maximum(ms[mp], mg)
            yield tuple(ms)

    def store_block(i, accs):
        ots = [a[0:V_DIM, :] / a[V_DIM:V_DIM + 1, :] for a in accs]
        o = (ots[0] - lam * ots[1]).T
        o = _rmsnorm(o, subg) * (1.0 - LAM_INIT)
        r0 = pl.multiple_of(i * ATTN_QB, ATTN_QB)
        o_ref[pl.ds(r0, ATTN_QB), :] = o.astype(o_ref.dtype)

    def finish(i, s_scr, ms, defer=False):
        accs = [None, None]
        for g, pieces in enumerate(groups):
            for _, vr, lo, hi, dst in pieces:
                for mp in range(2):
                    p = jnp.exp2(s_scr[mp, dst:dst + hi - lo, :] - ms[mp]).astype(BF16)
                    part = jnp.dot(vr[:, lo:hi], p, preferred_element_type=F32)
                    accs[mp] = part if accs[mp] is None else accs[mp] + part
            if g == len(groups) - 1:
                if defer:
                    acc_scr[0], acc_scr[1] = accs
                else:
                    store_block(i, accs)
            yield None

    c = RET_C
    nc = n // c

    def log_gamma(dec):
        z = -dec
        return -(jnp.maximum(z, 0.0) + jnp.log(1.0 + jnp.exp(-jnp.abs(z))))

    hsel = lax.broadcasted_iota(jnp.int32, (1, HEADS), 1) == head

    def pick(vec):
        return jnp.sum(jnp.where(hsel, vec, 0.0), axis=-1, keepdims=True)

    lgf, lgb = pick(log_gamma(decf_ref[...])), pick(log_gamma(decb_ref[...]))
    ii = lax.broadcasted_iota(jnp.int32, (c, c), 0).astype(F32)
    jj = lax.broadcasted_iota(jnp.int32, (c, c), 1).astype(F32)
    rel = ii - jj
    pos = lax.broadcasted_iota(jnp.int32, (c, LANES), 0).astype(F32)
    decay = (jnp.where(rel >= 0, jnp.exp(jnp.maximum(rel, 0.0) * lgf), 0.0)
             + jnp.where(rel <= 0, jnp.exp(jnp.maximum(-rel, 0.0) * lgb), 0.0))
    qdec_f, qdec_b = jnp.exp((pos + 1.0) * lgf), jnp.exp((c - pos) * lgb)
    kdec_f, kdec_b = jnp.exp((c - 1.0 - pos) * lgf), jnp.exp(pos * lgb)
    cdec_f, cdec_b = jnp.exp(c * lgf), jnp.exp(c * lgb)
    rmask = (lane >= HEAD_DIM).astype(jnp.int32) == head % 2
    gn = gn_ref[...]

    def kv_state(k, v, kdec):
        kd = (k.astype(F32) * kdec).astype(BF16)
        return lax.dot_general(kd, v, (((0,), (0,)), ((), ())), preferred_element_type=F32)

    rsf_scr[...] = kv_state(rkc_ref[...], rvc_ref[...], kdec_f)

    def backward_states():
        sb = kv_state(rkc_ref[...], rvc_ref[...], kdec_b)
        for ci in reversed(range(nc)):
            rsb_scr[ci] = sb.astype(BF16)
            if ci > 0:
                sb = sb * cdec_b + kv_state(rk_ref[ci * c:(ci + 1) * c, :], rv_ref[ci * c:(ci + 1) * c, :], kdec_b)
            yield None

    def ret_chunk(ci):
        r0 = pl.multiple_of(ci * c, c)
        q = rq_ref[pl.ds(r0, c), :]
        k = rk_ref[pl.ds(r0, c), :]
        v = rv_ref[pl.ds(r0, c), :]
        qm = jnp.where(rmask, q, jnp.zeros_like(q))
        sc = lax.dot_general(qm, k, (((1,), (1,)), ((), ())), preferred_element_type=F32)
        a = (sc * decay).astype(BF16)
        yield None
        sf = rsf_scr[...]
        o = jnp.dot(a, v, preferred_element_type=F32)
        o = o + jnp.dot(qm, sf.astype(BF16), preferred_element_type=F32) * qdec_f
        o = o + jnp.dot(qm, rsb_scr[ci], preferred_element_type=F32) * qdec_b
        yield None
        mu = jnp.mean(o, axis=-1, keepdims=True)
        var = jnp.mean(jnp.square(o - mu), axis=-1, keepdims=True)
        on = (o - mu) * lax.rsqrt(var + EPS) * gn
        gate = rg_ref[pl.ds(r0, c), :].astype(F32)
        ro_ref[pl.ds(r0, c), :] = (on * gate).astype(ro_ref.dtype)
        yield None
        rsf_scr[...] = sf * cdec_f + kv_state(k, v, kdec_f)
        yield None

    def run(*stages, side=None, side_steps=4 / ATTN_GROUPS):
        last = [None] * len(stages)
        done = 0
        for t, vals in enumerate(zip(*stages)):
            last = list(vals)
            while side is not None and done < (t + 1) * side_steps:
                next(side)
                done += 1
        return last

    def step(j, m_a):
        store_block(jnp.maximum(2 * j - 1, 0), (acc_scr[0], acc_scr[1]))
        m_b, _ = run(scores(2 * j + 1, sb_scr), finish(2 * j, sa_scr, m_a), side=ret_chunk(2 * j))
        m_a, _ = run(scores(2 * j + 2, sa_scr), finish(2 * j + 1, sb_scr, m_b, defer=True),
                     side=ret_chunk(2 * j + 1))
        return m_a

    acc_scr[...] = jnp.ones(acc_scr.shape, F32)
    (m_a,) = run(scores(0, sa_scr), side=backward_states(), side_steps=nc / ATTN_GROUPS)
    m_a = lax.fori_loop(0, nq // 2 - 1, step, m_a)
    store_block(nq - 3, (acc_scr[0], acc_scr[1]))
    m_b, _ = run(scores(nq - 1, sb_scr), finish(nq - 2, sa_scr, m_a), side=ret_chunk(nq - 2))
    run(finish(nq - 1, sb_scr, m_b), side=ret_chunk(nq - 1))


def _token_mixers(lq1, lk1, lq2, lk2, subg, qd, kd, vt, kd_c, vt_c,
                  dec_f, dec_b, gn_g, qr, kr, vr, gr, kr_c, vr_c):
    b, n, _ = qd.shape
    nctx = kd_c.shape[1]
    assert (n // ATTN_QB) % 2 == 0 and n % (ATTN_GROUPS * MXU_DIM) == 0 and nctx % MXU_DIM == 0
    assert ATTN_GROUPS >= 4 and nctx == RET_C and RET_C == ATTN_QB
    vec = lambda w: pl.BlockSpec((1, w), lambda bi, h: (0, 0))
    head = lambda rows: pl.BlockSpec((None, rows, LANES), lambda bi, h: (bi, 0, h))
    pair = lambda rows: pl.BlockSpec((None, rows, LANES), lambda bi, h: (bi, 0, h // 2))
    headt = lambda cols: pl.BlockSpec((None, None, VT_ROWS, cols), lambda bi, h: (bi, h, 0, 0))
    out = jax.ShapeDtypeStruct((b, n, HEADS * V_DIM), BF16)
    return pl.pallas_call(
        _mixer_kernel,
        grid=(b, HEADS),
        in_specs=[vec(HEAD_DIM)] * 4 + [vec(V_DIM), head(n), head(n), headt(n), head(nctx), headt(nctx),
                  vec(HEADS), vec(HEADS), pl.BlockSpec((1, V_DIM), lambda bi, h: (0, h)),
                  pair(n), pair(n), head(n), head(n), pair(nctx), head(nctx)],
        out_specs=[head(n), head(n)],
        out_shape=[out, out],
        scratch_shapes=[pltpu.VMEM((2, n + nctx, ATTN_QB), F32)] * 2
        + [pltpu.VMEM((n // RET_C, LANES, V_DIM), BF16), pltpu.VMEM((LANES, V_DIM), F32),
           pltpu.VMEM((2, VT_ROWS, ATTN_QB), F32)],
        compiler_params=_cparams(("arbitrary", "arbitrary")),
        name="token_mixers",
    )(lq1, lk1, lq2, lk2, subg, qd, kd, vt, kd_c, vt_c, dec_f, dec_b, gn_g, qr, kr, vr, gr, kr_c, vr_c)


def _mlp_kernel(x_ref, od_ref, or_ref, g1_ref, sh2_ref, sc2_ref, g2_ref, n2_ref, fg_ref,
                wo_ref, w1_ref, w2_ref, o_ref):
    half = od_ref.shape[1]
    y = (jnp.dot(od_ref[...], wo_ref[0:half, :].astype(BF16), preferred_element_type=F32)
         + jnp.dot(or_ref[...], wo_ref[half:2 * half, :].astype(BF16), preferred_element_type=F32))
    x1 = x_ref[...] + g1_ref[...] * y
    h = _rmsnorm(x1, n2_ref[...]) * (1.0 + sc2_ref[...]) + sh2_ref[...]
    hb = h.astype(BF16)
    d_ff = w1_ref.shape[1]
    acc = None
    for f0 in range(0, d_ff, FF_CHUNK):
        u = jnp.dot(hb, w1_ref[:, f0:f0 + FF_CHUNK].astype(BF16), preferred_element_type=F32)
        u = jnp.square(jnp.maximum(u, 0.0)).astype(BF16)
        part = jnp.dot(u, w2_ref[f0:f0 + FF_CHUNK, :].astype(BF16), preferred_element_type=F32)
        acc = part if acc is None else acc + part
    x2 = x1 + g2_ref[...] * acc
    o_ref[...] = _rmsnorm(x2, fg_ref[...])


def _out_mlp(x, od, orr, g1, sh2, sc2, g2, n2, fg, wo, w1, w2):
    b, n, d = x.shape
    tm = MLP_TM
    tok = lambda w: pl.BlockSpec((None, tm, w), lambda bi, t: (bi, t, 0))
    row = pl.BlockSpec((None, 1, d), lambda bi, t: (bi, 0, 0))
    gain = pl.BlockSpec((1, d), lambda bi, t: (0, 0))
    resident = lambda a: pl.BlockSpec(a.shape, lambda bi, t: (0, 0), pipeline_mode=pl.Buffered(1))
    return pl.pallas_call(
        _mlp_kernel,
        grid=(b, n // tm),
        in_specs=[tok(d), tok(od.shape[2]), tok(orr.shape[2]), row, row, row, row, gain, gain,
                  resident(wo), resident(w1), resident(w2)],
        out_specs=tok(d),
        out_shape=jax.ShapeDtypeStruct((b, n, d), F32),
        compiler_params=_cparams(("arbitrary", "arbitrary")),
        name="out_mlp",
    )(x, od, orr, g1, sh2, sc2, g2, n2, fg, wo, w1, w2)


def kernel(x, c, ctx, c_ctx, w_ada, b_ada, norm1_g, norm2_g, w_in, lambda_q1, lambda_k1, lambda_q2,
           lambda_k2, diff_subln_g, ret_decay_fwd, ret_decay_bwd, ret_gn_g, w_out, w_mlp1, w_mlp2, final_g):
    assert w_ada.shape[0] == 1, "single-layer block"
    b, n, d = x.shape

    c_rows = jnp.concatenate([c, c_ctx[None], jnp.zeros((8 - b - 1, d), F32)], axis=0)
    mod = _adaln(c_rows, w_ada[0], b_ada).reshape(8, N_MOD, d)
    sh1, sc1, g1, sh2, sc2, g2 = [mod[:b, i][:, None, :] for i in range(N_MOD)]

    qd, kd, vt, qr, kr, vr, gr = _project_latent(x, sh1, sc1, norm1_g, w_in[0], _rope_tables(n))
    kd_c, vt_c, kr_c, vr_c = _project_ctx(ctx, mod[b, 0][None], mod[b, 1][None], norm1_g, w_in[0])

    od, orr = _token_mixers(lambda_q1, lambda_k1, lambda_q2, lambda_k2, diff_subln_g, qd, kd, vt, kd_c, vt_c,
                            ret_decay_fwd, ret_decay_bwd, ret_gn_g, qr, kr, vr, gr, kr_c, vr_c)

    return _out_mlp(x, od, orr, g1, sh2, sc2, g2, norm2_g, final_g[None],
                    w_out[0], w_mlp1[0], w_mlp2[0])
```

```python
import math

import numpy as np
import jax
import jax.numpy as jnp
from jax import lax
from jax.experimental import pallas as pl
from jax.experimental.pallas import tpu as pltpu

F32 = jnp.float32
BF16 = jnp.bfloat16

GRID_W = 64
HEADS = 4
HEAD_DIM = 64
V_DIM = 128
VT_ROWS = V_DIM + 16
ROPE_PAIRS = 16
ROPE_BASE = 10000.0
N_MOD = 6
EPS = 1e-6
LAM_INIT = 0.8 - 0.6 * math.exp(-0.3 * 0)
LOG2E = 1.4426950408889634
QK_SCALE = HEAD_DIM ** -0.5

LANES = 128
MXU_DIM = 256
VMEM_LIMIT_BYTES = 56 * 1024 * 1024

PROJ_TM = 1024
ATTN_QB = 256
ATTN_GROUPS = 8
RET_C = 256
MLP_TM = 512
FF_CHUNK = 1024
ADA_TN = 1536

W_DQK = HEADS * 2 * HEAD_DIM
W_RQK = HEADS * HEAD_DIM
W_V = HEADS * V_DIM


def _col_ranges(*widths):
    edges = [sum(widths[:i]) for i in range(len(widths) + 1)]
    return list(zip(edges[:-1], edges[1:]))


COL_QD, COL_KD, COL_VD, COL_QR, COL_KR, COL_VR, COL_GR = _col_ranges(W_DQK, W_DQK, W_V, W_RQK, W_RQK, W_V, W_V)


def _cparams(sem):
    return pltpu.CompilerParams(dimension_semantics=sem, vmem_limit_bytes=VMEM_LIMIT_BYTES)


def _rmsnorm(xf, g):
    return xf * lax.rsqrt(jnp.mean(xf * xf, axis=-1, keepdims=True) + EPS) * g


def _silu(x):
    return x * (1.0 / (1.0 + jnp.exp(-x)))


def _adaln_kernel(c_ref, w_ref, b_ref, o_ref):
    a = _silu(c_ref[...]).astype(BF16)
    o_ref[...] = jnp.dot(a, w_ref[...].astype(BF16), preferred_element_type=F32) + b_ref[...]


def _adaln(c_rows, w_ada, b_ada):
    r, d = c_rows.shape
    n_out = w_ada.shape[1]
    return pl.pallas_call(
        _adaln_kernel,
        grid=(n_out // ADA_TN,),
        in_specs=[pl.BlockSpec((r, d), lambda j: (0, 0)),
                  pl.BlockSpec((d, ADA_TN), lambda j: (0, j)),
                  pl.BlockSpec((1, ADA_TN), lambda j: (0, j))],
        out_specs=pl.BlockSpec((r, ADA_TN), lambda j: (0, j)),
        out_shape=jax.ShapeDtypeStruct((r, n_out), F32),
        compiler_params=_cparams(("arbitrary",)),
        name="adaln",
    )(c_rows, w_ada, b_ada)


def _rope_tables(n_tokens):
    rows = n_tokens // GRID_W
    row = np.repeat(np.arange(rows), GRID_W).astype(np.float64)
    col = np.tile(np.arange(GRID_W), rows).astype(np.float64)
    inv = (np.float32(ROPE_BASE) ** (-np.arange(ROPE_PAIRS, dtype=np.float32) / ROPE_PAIRS)).astype(np.float64)
    ang_r = row[:, None] * inv
    ang_c = col[:, None] * inv
    zeros = np.zeros_like(ang_r)
    cos64 = np.concatenate([np.cos(ang_r)] * 2 + [np.cos(ang_c)] * 2, axis=1)
    sa64 = np.concatenate([-np.sin(ang_r), zeros, -np.sin(ang_c), zeros], axis=1)
    sb64 = np.concatenate([zeros, np.sin(ang_r), zeros, np.sin(ang_c)], axis=1)
    dup = lambda t: jnp.asarray(np.concatenate([t, t], axis=1), dtype=F32)
    return dup(cos64), dup(sa64), dup(sb64)


def _modulated(x_ref, sh_ref, sc_ref, g_ref):
    h = _rmsnorm(x_ref[...], g_ref[...])
    return (h * (1.0 + sc_ref[...]) + sh_ref[...]).astype(BF16)


def _proj_latent_kernel(x_ref, sh_ref, sc_ref, g_ref, w_ref, c_ref, sa_ref, sb_ref,
                        qd_ref, kd_ref, vt_ref, qr_ref, kr_ref, vr_ref, gr_ref):
    hb = _modulated(x_ref, sh_ref, sc_ref, g_ref)
    c, sa, sb = c_ref[...], sa_ref[...], sb_ref[...]

    def project(cols):
        return jnp.dot(hb, w_ref[:, cols[0]:cols[1]].astype(BF16), preferred_element_type=F32)

    def rotate(y):
        slabs = []
        for s in range(y.shape[1] // LANES):
            ys = y[:, s * LANES:(s + 1) * LANES]
            slabs.append(ys * c + pltpu.roll(ys, LANES - ROPE_PAIRS, 1) * sa
                         + pltpu.roll(ys, ROPE_PAIRS, 1) * sb)
        return jnp.concatenate(slabs, axis=1)

    qd_ref[...] = (rotate(project(COL_QD)) * (QK_SCALE * LOG2E)).astype(BF16)
    kd_ref[...] = rotate(project(COL_KD)).astype(BF16)
    vd = project(COL_VD)
    ones = jnp.ones((VT_ROWS - V_DIM, vd.shape[0]), BF16)
    for h in range(HEADS):
        vt_ref[h, 0:V_DIM, :] = vd[:, h * V_DIM:(h + 1) * V_DIM].T.astype(BF16)
        vt_ref[h, V_DIM:VT_ROWS, :] = ones
    gr_ref[...] = _silu(project(COL_GR)).astype(BF16)
    qr_ref[...] = rotate(project(COL_QR)).astype(BF16)
    kr_ref[...] = (rotate(project(COL_KR)) * QK_SCALE).astype(BF16)
    vr_ref[...] = project(COL_VR).astype(BF16)


def _project_latent(x, sh, sc, g, w_in, tables):
    b, n, d = x.shape
    tm = PROJ_TM
    tok = lambda w: pl.BlockSpec((None, tm, w), lambda t, bi: (bi, t, 0))
    row = pl.BlockSpec((None, 1, d), lambda t, bi: (bi, 0, 0))
    table = pl.BlockSpec((tm, LANES), lambda t, bi: (t, 0))
    out_specs = [tok(W_DQK), tok(W_DQK),
                 pl.BlockSpec((None, HEADS, VT_ROWS, tm), lambda t, bi: (bi, 0, 0, t)),
                 tok(W_RQK), tok(W_RQK), tok(W_V), tok(W_V)]
    out_shape = [jax.ShapeDtypeStruct((b, n, W_DQK), BF16),
                 jax.ShapeDtypeStruct((b, n, W_DQK), BF16),
                 jax.ShapeDtypeStruct((b, HEADS, VT_ROWS, n), BF16),
                 jax.ShapeDtypeStruct((b, n, W_RQK), BF16),
                 jax.ShapeDtypeStruct((b, n, W_RQK), BF16),
                 jax.ShapeDtypeStruct((b, n, W_V), BF16),
                 jax.ShapeDtypeStruct((b, n, W_V), BF16)]
    return pl.pallas_call(
        _proj_latent_kernel,
        grid=(n // tm, b),
        in_specs=[tok(d), row, row, pl.BlockSpec((1, d), lambda t, bi: (0, 0)),
                  pl.BlockSpec(w_in.shape, lambda t, bi: (0, 0), pipeline_mode=pl.Buffered(1)),
                  table, table, table],
        out_specs=out_specs,
        out_shape=out_shape,
        compiler_params=_cparams(("arbitrary", "arbitrary")),
        name="proj_latent",
    )(x, sh, sc, g, w_in, *tables)


def _proj_ctx_kernel(x_ref, sh_ref, sc_ref, g_ref, w_ref, kd_ref, vt_ref, kr_ref, vr_ref):
    hb = _modulated(x_ref, sh_ref, sc_ref, g_ref)
    b, nctx = kd_ref.shape[0], kd_ref.shape[1]

    def project(cols):
        return jnp.dot(hb, w_ref[:, cols[0]:cols[1]].astype(BF16), preferred_element_type=F32)

    kd, vd, kr, vr = project(COL_KD), project(COL_VD), project(COL_KR) * QK_SCALE, project(COL_VR)
    for bi in range(b):
        rows = slice(bi * nctx, (bi + 1) * nctx)
        kd_ref[bi] = kd[rows].astype(BF16)
        kr_ref[bi] = kr[rows].astype(BF16)
        vr_ref[bi] = vr[rows].astype(BF16)
        for h in range(HEADS):
            vt_ref[bi, h, 0:V_DIM, :] = vd[rows, h * V_DIM:(h + 1) * V_DIM].T.astype(BF16)
            vt_ref[bi, h, V_DIM:VT_ROWS, :] = jnp.ones((VT_ROWS - V_DIM, nctx), BF16)


def _project_ctx(ctx, sh, sc, g, w_in):
    b, nctx, d = ctx.shape
    full = lambda shape: pl.BlockSpec(shape, lambda i: (0,) * len(shape))
    x2 = ctx.reshape(b * nctx, d)
    shapes = [(b, nctx, W_DQK), (b, HEADS, VT_ROWS, nctx), (b, nctx, W_RQK), (b, nctx, W_V)]
    return pl.pallas_call(
        _proj_ctx_kernel,
        grid=(1,),
        in_specs=[full(a.shape) for a in (x2, sh, sc, g, w_in)],
        out_specs=[full(shape) for shape in shapes],
        out_shape=[jax.ShapeDtypeStruct(shape, BF16) for shape in shapes],
        compiler_params=_cparams(("arbitrary",)),
        name="proj_ctx",
    )(x2, sh, sc, g, w_in)


def _mixer_kernel(lq1_ref, lk1_ref, lq2_ref, lk2_ref, subg_ref, q_ref, k_ref, vt_ref, kc_ref, vtc_ref,
                  decf_ref, decb_ref, gn_ref, rq_ref, rk_ref, rv_ref, rg_ref, rkc_ref, rvc_ref, *rest):
    n_w = (len(rest) - 7) // 2
    w_refs, (o_ref, ro_ref), wb_refs = rest[:n_w], rest[n_w:n_w + 2], rest[n_w + 2:2 * n_w + 2]
    sa_scr, sb_scr, rsb_scr, rsf_scr, acc_scr = rest[2 * n_w + 2:]
    n = q_ref.shape[0]
    head = pl.program_id(1)
    lane = lax.broadcasted_iota(jnp.int32, (1, LANES), 1)

    for w_ref, wb_ref in zip(w_refs, wb_refs):
        wb_ref[...] = w_ref[...].astype(BF16)

    lam = (jnp.exp(jnp.sum(lq1_ref[...] * lk1_ref[...], axis=-1, keepdims=True))
           - jnp.exp(jnp.sum(lq2_ref[...] * lk2_ref[...], axis=-1, keepdims=True)) + LAM_INIT)
    first_map = lane < HEAD_DIM
    subg = subg_ref[...]
    nq = n // ATTN_QB

    nctx = kc_ref.shape[0]
    gsz = n // ATTN_GROUPS
    groups = [[(k_ref, vt_ref, g * gsz, (g + 1) * gsz, nctx + g * gsz)] for g in range(ATTN_GROUPS)]
    groups[0].insert(0, (kc_ref, vtc_ref, 0, nctx, 0))

    def scores(i, s_scr):
        r0 = pl.multiple_of(i * ATTN_QB, ATTN_QB)
        q = q_ref[pl.ds(r0, ATTN_QB), :]
        zero = jnp.zeros_like(q)
        qms = (jnp.where(first_map, q, zero), jnp.where(first_map, zero, q))
        ms = [None, None]
        for pieces in groups:
            for kr, _, lo, hi, dst in pieces:
                for mp in range(2):
                    s = lax.dot_general(kr[lo:hi, :], qms[mp], (((1,), (1,)), ((), ())),
                                        preferred_element_type=F32)
                    s_scr[mp, dst:dst + hi - lo, :] = s
                    mg = jnp.max(s, axis=0, keepdims=True)
                    ms[mp] = mg if ms[mp] is None else jnp.maximum(ms[mp], mg)
            yield tuple(ms)

    def store_block(i, accs):
        ots = [a[0:V_DIM, :] / a[V_DIM:V_DIM + 1, :] for a in accs]
        o = (ots[0] - lam * ots[1]).T
        o = _rmsnorm(o, subg) * (1.0 - LAM_INIT)
        r0 = pl.multiple_of(i * ATTN_QB, ATTN_QB)
        o_ref[pl.ds(r0, ATTN_QB), :] = o.astype(o_ref.dtype)

    def finish(i, s_scr, ms, defer=False):
        accs = [None, None]
        for g, pieces in enumerate(groups):
            for _, vr, lo, hi, dst in pieces:
                for mp in range(2):
                    p = jnp.exp2(s_scr[mp, dst:dst + hi - lo, :] - ms[mp]).astype(BF16)
                    part = jnp.dot(vr[:, lo:hi], p, preferred_element_type=F32)
                    accs[mp] = part if accs[mp] is None else accs[mp] + part
            if g == len(groups) - 1:
                if defer:
                    acc_scr[0], acc_scr[1] = accs
                else:
                    store_block(i, accs)
            yield None

    c = RET_C
    nc = n // c

    def log_gamma(dec):
        z = -dec
        return -(jnp.maximum(z, 0.0) + jnp.log(1.0 + jnp.exp(-jnp.abs(z))))

    hsel = lax.broadcasted_iota(jnp.int32, (1, HEADS), 1) == head

    def pick(vec):
        return jnp.sum(jnp.where(hsel, vec, 0.0), axis=-1, keepdims=True)

    rmask = (lane >= HEAD_DIM).astype(jnp.int32) == head % 2
    gn = gn_ref[...]
    rt = {}

    def kv_state(k, v, kdec):
        kd = (k.astype(F32) * kdec).astype(BF16)
        return lax.dot_general(kd, v, (((0,), (0,)), ((), ())), preferred_element_type=F32)

    def backward_states():
        lgf, lgb = pick(log_gamma(decf_ref[...])), pick(log_gamma(decb_ref[...]))
        ii = lax.broadcasted_iota(jnp.int32, (c, c), 0).astype(F32)
        jj = lax.broadcasted_iota(jnp.int32, (c, c), 1).astype(F32)
        rel = ii - jj
        pos = lax.broadcasted_iota(jnp.int32, (c, LANES), 0).astype(F32)
        rt["decay"] = (jnp.where(rel >= 0, jnp.exp(jnp.maximum(rel, 0.0) * lgf), 0.0)
                       + jnp.where(rel <= 0, jnp.exp(jnp.maximum(-rel, 0.0) * lgb), 0.0))
        rt["qdec_f"], rt["qdec_b"] = jnp.exp((pos + 1.0) * lgf), jnp.exp((c - pos) * lgb)
        rt["kdec_f"], kdec_b = jnp.exp((c - 1.0 - pos) * lgf), jnp.exp(pos * lgb)
        rt["cdec_f"], cdec_b = jnp.exp(c * lgf), jnp.exp(c * lgb)
        rsf_scr[...] = kv_state(rkc_ref[...], rvc_ref[...], rt["kdec_f"])
        sb = kv_state(rkc_ref[...], rvc_ref[...], kdec_b)
        for ci in reversed(range(nc)):
            rsb_scr[ci] = sb.astype(BF16)
            if ci > 0:
                sb = sb * cdec_b + kv_state(rk_ref[ci * c:(ci + 1) * c, :], rv_ref[ci * c:(ci + 1) * c, :], kdec_b)
            yield None

    def ret_chunk(ci):
        r0 = pl.multiple_of(ci * c, c)
        q = rq_ref[pl.ds(r0, c), :]
        k = rk_ref[pl.ds(r0, c), :]
        v = rv_ref[pl.ds(r0, c), :]
        qm = jnp.where(rmask, q, jnp.zeros_like(q))
        sc = lax.dot_general(qm, k, (((1,), (1,)), ((), ())), preferred_element_type=F32)
        a = (sc * rt["decay"]).astype(BF16)
        yield None
        sf = rsf_scr[...]
        o = jnp.dot(a, v, preferred_element_type=F32)
        o = o + jnp.dot(qm, sf.astype(BF16), preferred_element_type=F32) * rt["qdec_f"]
        o = o + jnp.dot(qm, rsb_scr[ci], preferred_element_type=F32) * rt["qdec_b"]
        yield None
        mu = jnp.mean(o, axis=-1, keepdims=True)
        var = jnp.mean(jnp.square(o - mu), axis=-1, keepdims=True)
        on = (o - mu) * lax.rsqrt(var + EPS) * gn
        gate = rg_ref[pl.ds(r0, c), :].astype(F32)
        ro_ref[pl.ds(r0, c), :] = (on * gate).astype(ro_ref.dtype)
        yield None
        rsf_scr[...] = sf * rt["cdec_f"] + kv_state(k, v, rt["kdec_f"])
        yield None

    def run(*stages, side=None, side_steps=4 / ATTN_GROUPS):
        last = [None] * len(stages)
        done = 0
        for t, vals in enumerate(zip(*stages)):
            last = list(vals)
            while side is not None and done < (t + 1) * side_steps:
                next(side)
                done += 1
        return last

    def step(j, m_a):
        store_block(jnp.maximum(2 * j - 1, 0), (acc_scr[0], acc_scr[1]))
        m_b, _ = run(scores(2 * j + 1, sb_scr), finish(2 * j, sa_scr, m_a), side=ret_chunk(2 * j))
        m_a, _ = run(scores(2 * j + 2, sa_scr), finish(2 * j + 1, sb_scr, m_b, defer=True),
                     side=ret_chunk(2 * j + 1))
        return m_a

    acc_scr[...] = jnp.ones(acc_scr.shape, F32)
    (m_a,) = run(scores(0, sa_scr), side=backward_states(), side_steps=nc / ATTN_GROUPS)
    m_a = lax.fori_loop(0, nq // 2 - 1, step, m_a)
    store_block(nq - 3, (acc_scr[0], acc_scr[1]))
    m_b, _ = run(scores(nq - 1, sb_scr), finish(nq - 2, sa_scr, m_a), side=ret_chunk(nq - 2))
    run(finish(nq - 1, sb_scr, m_b), side=ret_chunk(nq - 1))


def _token_mixers(lq1, lk1, lq2, lk2, subg, qd, kd, vt, kd_c, vt_c,
                  dec_f, dec_b, gn_g, qr, kr, vr, gr, kr_c, vr_c, next_weights):
    b, n, _ = qd.shape
    nctx = kd_c.shape[1]
    assert (n // ATTN_QB) % 2 == 0 and n % (ATTN_GROUPS * MXU_DIM) == 0 and nctx % MXU_DIM == 0
    assert ATTN_GROUPS >= 4 and nctx == RET_C and RET_C == ATTN_QB
    vec = lambda w: pl.BlockSpec((1, w), lambda bi, h: (0, 0))
    head = lambda rows: pl.BlockSpec((None, rows, LANES), lambda bi, h: (bi, 0, h))
    pair = lambda rows: pl.BlockSpec((None, rows, LANES), lambda bi, h: (bi, 0, h // 2))
    headt = lambda cols: pl.BlockSpec((None, None, VT_ROWS, cols), lambda bi, h: (bi, h, 0, 0))
    out = jax.ShapeDtypeStruct((b, n, HEADS * V_DIM), BF16)
    steps = b * HEADS
    assert all(w.shape[0] % (16 * steps) == 0 for w in next_weights)
    w_specs = [pl.BlockSpec((w.shape[0] // steps, w.shape[1]), lambda bi, h: (bi * HEADS + h, 0))
               for w in next_weights]
    results = pl.pallas_call(
        _mixer_kernel,
        grid=(b, HEADS),
        in_specs=[vec(HEAD_DIM)] * 4 + [vec(V_DIM), head(n), head(n), headt(n), head(nctx), headt(nctx),
                  vec(HEADS), vec(HEADS), pl.BlockSpec((1, V_DIM), lambda bi, h: (0, h)),
                  pair(n), pair(n), head(n), head(n), pair(nctx), head(nctx)] + w_specs,
        out_specs=[head(n), head(n)] + w_specs,
        out_shape=[out, out] + [jax.ShapeDtypeStruct(w.shape, BF16) for w in next_weights],
        scratch_shapes=[pltpu.VMEM((2, n + nctx, ATTN_QB), F32)] * 2
        + [pltpu.VMEM((n // RET_C, LANES, V_DIM), BF16), pltpu.VMEM((LANES, V_DIM), F32),
           pltpu.VMEM((2, VT_ROWS, ATTN_QB), F32)],
        compiler_params=_cparams(("arbitrary", "arbitrary")),
        name="token_mixers",
    )(lq1, lk1, lq2, lk2, subg, qd, kd, vt, kd_c, vt_c, dec_f, dec_b, gn_g, qr, kr, vr, gr, kr_c, vr_c,
      *next_weights)
    return results[0], results[1], results[2:]


def _mlp_kernel(x_ref, od_ref, or_ref, g1_ref, sh2_ref, sc2_ref, g2_ref, n2_ref, fg_ref,
                wo_ref, w1_ref, w2_ref, o_ref):
    half = od_ref.shape[1]
    y = (jnp.dot(od_ref[...], wo_ref[0:half, :], preferred_element_type=F32)
         + jnp.dot(or_ref[...], wo_ref[half:2 * half, :], preferred_element_type=F32))
    x1 = x_ref[...] + g1_ref[...] * y
    h = _rmsnorm(x1, n2_ref[...]) * (1.0 + sc2_ref[...]) + sh2_ref[...]
    hb = h.astype(BF16)
    d_ff = w1_ref.shape[1]
    acc = None
    for f0 in range(0, d_ff, FF_CHUNK):
        u = jnp.dot(hb, w1_ref[:, f0:f0 + FF_CHUNK], preferred_element_type=F32)
        u = jnp.square(jnp.maximum(u, 0.0)).astype(BF16)
        part = jnp.dot(u, w2_ref[f0:f0 + FF_CHUNK, :], preferred_element_type=F32)
        acc = part if acc is None else acc + part
    x2 = x1 + g2_ref[...] * acc
    o_ref[...] = _rmsnorm(x2, fg_ref[...])


def _out_mlp(x, od, orr, g1, sh2, sc2, g2, n2, fg, wo, w1, w2):
    b, n, d = x.shape
    tm = MLP_TM
    tok = lambda w: pl.BlockSpec((None, tm, w), lambda bi, t: (bi, t, 0))
    row = pl.BlockSpec((None, 1, d), lambda bi, t: (bi, 0, 0))
    gain = pl.BlockSpec((1, d), lambda bi, t: (0, 0))
    resident = lambda a: pl.BlockSpec(a.shape, lambda bi, t: (0, 0), pipeline_mode=pl.Buffered(1))
    return pl.pallas_call(
        _mlp_kernel,
        grid=(b, n // tm),
        in_specs=[tok(d), tok(od.shape[2]), tok(orr.shape[2]), row, row, row, row, gain, gain,
                  resident(wo), resident(w1), resident(w2)],
        out_specs=tok(d),
        out_shape=jax.ShapeDtypeStruct((b, n, d), F32),
        compiler_params=_cparams(("arbitrary", "arbitrary")),
        name="out_mlp",
    )(x, od, orr, g1, sh2, sc2, g2, n2, fg, wo, w1, w2)


def kernel(x, c, ctx, c_ctx, w_ada, b_ada, norm1_g, norm2_g, w_in, lambda_q1, lambda_k1, lambda_q2,
           lambda_k2, diff_subln_g, ret_decay_fwd, ret_decay_bwd, ret_gn_g, w_out, w_mlp1, w_mlp2, final_g):
    assert w_ada.shape[0] == 1, "single-layer block"
    b, n, d = x.shape

    c_rows = jnp.concatenate([c, c_ctx[None], jnp.zeros((8 - b - 1, d), F32)], axis=0)
    mod = _adaln(c_rows, w_ada[0], b_ada).reshape(8, N_MOD, d)
    sh1, sc1, g1, sh2, sc2, g2 = [mod[:b, i][:, None, :] for i in range(N_MOD)]

    qd, kd, vt, qr, kr, vr, gr = _project_latent(x, sh1, sc1, norm1_g, w_in[0], _rope_tables(n))
    kd_c, vt_c, kr_c, vr_c = _project_ctx(ctx, mod[b, 0][None], mod[b, 1][None], norm1_g, w_in[0])

    od, orr, (wo_b, w1_b, w2_b) = _token_mixers(
        lambda_q1, lambda_k1, lambda_q2, lambda_k2, diff_subln_g, qd, kd, vt, kd_c, vt_c,
        ret_decay_fwd, ret_decay_bwd, ret_gn_g, qr, kr, vr, gr, kr_c, vr_c,
        next_weights=(w_out[0], w_mlp1[0], w_mlp2[0]))

    return _out_mlp(x, od, orr, g1, sh2, sc2, g2, norm2_g, final_g[None], wo_b, w1_b, w2_b)
```

```python
import math

import numpy as np
import jax
import jax.numpy as jnp
from jax import lax
from jax.experimental import pallas as pl
from jax.experimental.pallas import tpu as pltpu

F32 = jnp.float32
BF16 = jnp.bfloat16

GRID_W = 64
HEADS = 4
HEAD_DIM = 64
V_DIM = 128
VT_ROWS = V_DIM + 16
ROPE_PAIRS = 16
ROPE_BASE = 10000.0
N_MOD = 6
EPS = 1e-6
LAM_INIT = 0.8 - 0.6 * math.exp(-0.3 * 0)
LOG2E = 1.4426950408889634
QK_SCALE = HEAD_DIM ** -0.5

LANES = 128
MXU_DIM = 256
VMEM_LIMIT_BYTES = 56 * 1024 * 1024

PROJ_TM = 1024
ATTN_QB = 256
ATTN_GROUPS = 8
RET_C = 256
MLP_TM = 512
FF_CHUNK = 1024
ADA_TN = 1536

W_DQK = HEADS * 2 * HEAD_DIM
W_RQK = HEADS * HEAD_DIM
W_V = HEADS * V_DIM


def _col_ranges(*widths):
    edges = [sum(widths[:i]) for i in range(len(widths) + 1)]
    return list(zip(edges[:-1], edges[1:]))


COL_QD, COL_KD, COL_VD, COL_QR, COL_KR, COL_VR, COL_GR = _col_ranges(W_DQK, W_DQK, W_V, W_RQK, W_RQK, W_V, W_V)


def _cparams(sem):
    return pltpu.CompilerParams(dimension_semantics=sem, vmem_limit_bytes=VMEM_LIMIT_BYTES)


def _rmsnorm(xf, g):
    return xf * lax.rsqrt(jnp.mean(xf * xf, axis=-1, keepdims=True) + EPS) * g


def _silu(x):
    return x * (1.0 / (1.0 + jnp.exp(-x)))


def _adaln_kernel(c_ref, w_ref, b_ref, o_ref):
    a = _silu(c_ref[...]).astype(BF16)
    o_ref[...] = jnp.dot(a, w_ref[...].astype(BF16), preferred_element_type=F32) + b_ref[...]


def _adaln(c_rows, w_ada, b_ada):
    r, d = c_rows.shape
    n_out = w_ada.shape[1]
    return pl.pallas_call(
        _adaln_kernel,
        grid=(n_out // ADA_TN,),
        in_specs=[pl.BlockSpec((r, d), lambda j: (0, 0)),
                  pl.BlockSpec((d, ADA_TN), lambda j: (0, j)),
                  pl.BlockSpec((1, ADA_TN), lambda j: (0, j))],
        out_specs=pl.BlockSpec((r, ADA_TN), lambda j: (0, j)),
        out_shape=jax.ShapeDtypeStruct((r, n_out), F32),
        compiler_params=_cparams(("arbitrary",)),
        name="adaln",
    )(c_rows, w_ada, b_ada)


def _rope_tables(n_tokens):
    rows = n_tokens // GRID_W
    row = np.repeat(np.arange(rows), GRID_W).astype(np.float64)
    col = np.tile(np.arange(GRID_W), rows).astype(np.float64)
    inv = (np.float32(ROPE_BASE) ** (-np.arange(ROPE_PAIRS, dtype=np.float32) / ROPE_PAIRS)).astype(np.float64)
    ang_r = row[:, None] * inv
    ang_c = col[:, None] * inv
    zeros = np.zeros_like(ang_r)
    cos64 = np.concatenate([np.cos(ang_r)] * 2 + [np.cos(ang_c)] * 2, axis=1)
    sa64 = np.concatenate([-np.sin(ang_r), zeros, -np.sin(ang_c), zeros], axis=1)
    sb64 = np.concatenate([zeros, np.sin(ang_r), zeros, np.sin(ang_c)], axis=1)
    dup = lambda t: jnp.asarray(np.concatenate([t, t], axis=1), dtype=F32)
    return dup(cos64), dup(sa64), dup(sb64)


def _modulated(x_ref, sh_ref, sc_ref, g_ref):
    h = _rmsnorm(x_ref[...], g_ref[...])
    return (h * (1.0 + sc_ref[...]) + sh_ref[...]).astype(BF16)


def _proj_kernel(x_ref, sh_ref, sc_ref, g_ref, w_ref, c_ref, sa_ref, sb_ref, xc_ref, csh_ref, csc_ref,
                 qd_ref, kd_ref, vt_ref, qr_ref, kr_ref, vr_ref, gr_ref,
                 kdc_ref, vtc_ref, krc_ref, vrc_ref):
    def project(hb, cols):
        return jnp.dot(hb, w_ref[:, cols[0]:cols[1]].astype(BF16), preferred_element_type=F32)

    def put_vt(ref, vd):
        ones = jnp.ones((VT_ROWS - V_DIM, vd.shape[0]), BF16)
        for h in range(HEADS):
            ref[h, 0:V_DIM, :] = vd[:, h * V_DIM:(h + 1) * V_DIM].T.astype(BF16)
            ref[h, V_DIM:VT_ROWS, :] = ones

    hb = _modulated(x_ref, sh_ref, sc_ref, g_ref)
    c, sa, sb = c_ref[...], sa_ref[...], sb_ref[...]

    def rotate(y):
        slabs = []
        for s in range(y.shape[1] // LANES):
            ys = y[:, s * LANES:(s + 1) * LANES]
            slabs.append(ys * c + pltpu.roll(ys, LANES - ROPE_PAIRS, 1) * sa
                         + pltpu.roll(ys, ROPE_PAIRS, 1) * sb)
        return jnp.concatenate(slabs, axis=1)

    qd_ref[...] = (rotate(project(hb, COL_QD)) * (QK_SCALE * LOG2E)).astype(BF16)
    kd_ref[...] = rotate(project(hb, COL_KD)).astype(BF16)
    put_vt(vt_ref, project(hb, COL_VD))
    gr_ref[...] = _silu(project(hb, COL_GR)).astype(BF16)
    qr_ref[...] = rotate(project(hb, COL_QR)).astype(BF16)
    kr_ref[...] = (rotate(project(hb, COL_KR)) * QK_SCALE).astype(BF16)
    vr_ref[...] = project(hb, COL_VR).astype(BF16)

    @pl.when(pl.program_id(1) == 0)
    def _():
        hc = _modulated(xc_ref, csh_ref, csc_ref, g_ref)
        kdc_ref[...] = project(hc, COL_KD).astype(BF16)
        put_vt(vtc_ref, project(hc, COL_VD))
        krc_ref[...] = (project(hc, COL_KR) * QK_SCALE).astype(BF16)
        vrc_ref[...] = project(hc, COL_VR).astype(BF16)


def _project(x, ctx, sh, sc, csh, csc, g, w_in, tables):
    b, n, d = x.shape
    nctx = ctx.shape[1]
    tm = PROJ_TM
    tok = lambda w: pl.BlockSpec((None, tm, w), lambda bi, t: (bi, t, 0))
    ctok = lambda w: pl.BlockSpec((None, nctx, w), lambda bi, t: (bi, 0, 0))
    row = pl.BlockSpec((None, 1, d), lambda bi, t: (bi, 0, 0))
    shared = pl.BlockSpec((1, d), lambda bi, t: (0, 0))
    table = pl.BlockSpec((tm, LANES), lambda bi, t: (t, 0))
    out_specs = [tok(W_DQK), tok(W_DQK),
                 pl.BlockSpec((None, HEADS, VT_ROWS, tm), lambda bi, t: (bi, 0, 0, t)),
                 tok(W_RQK), tok(W_RQK), tok(W_V), tok(W_V),
                 ctok(W_DQK), pl.BlockSpec((None, HEADS, VT_ROWS, nctx), lambda bi, t: (bi, 0, 0, 0)),
                 ctok(W_RQK), ctok(W_V)]
    out_shape = [jax.ShapeDtypeStruct((b, n, W_DQK), BF16),
                 jax.ShapeDtypeStruct((b, n, W_DQK), BF16),
                 jax.ShapeDtypeStruct((b, HEADS, VT_ROWS, n), BF16),
                 jax.ShapeDtypeStruct((b, n, W_RQK), BF16),
                 jax.ShapeDtypeStruct((b, n, W_RQK), BF16),
                 jax.ShapeDtypeStruct((b, n, W_V), BF16),
                 jax.ShapeDtypeStruct((b, n, W_V), BF16),
                 jax.ShapeDtypeStruct((b, nctx, W_DQK), BF16),
                 jax.ShapeDtypeStruct((b, HEADS, VT_ROWS, nctx), BF16),
                 jax.ShapeDtypeStruct((b, nctx, W_RQK), BF16),
                 jax.ShapeDtypeStruct((b, nctx, W_V), BF16)]
    return pl.pallas_call(
        _proj_kernel,
        grid=(b, n // tm),
        in_specs=[tok(d), row, row, shared,
                  pl.BlockSpec(w_in.shape, lambda bi, t: (0, 0), pipeline_mode=pl.Buffered(1)),
                  table, table, table, ctok(d), shared, shared],
        out_specs=out_specs,
        out_shape=out_shape,
        compiler_params=_cparams(("arbitrary", "arbitrary")),
        name="proj",
    )(x, sh, sc, g, w_in, *tables, ctx, csh, csc)


def _mixer_kernel(lq1_ref, lk1_ref, lq2_ref, lk2_ref, subg_ref, q_ref, k_ref, vt_ref, kc_ref, vtc_ref,
                  decf_ref, decb_ref, gn_ref, rq_ref, rk_ref, rv_ref, rg_ref, rkc_ref, rvc_ref,
                  o_ref, ro_ref, sa_scr, sb_scr, rsb_scr, rsf_scr, acc_scr):
    n = q_ref.shape[0]
    head = pl.program_id(1)
    lane = lax.broadcasted_iota(jnp.int32, (1, LANES), 1)

    lam = (jnp.exp(jnp.sum(lq1_ref[...] * lk1_ref[...], axis=-1, keepdims=True))
           - jnp.exp(jnp.sum(lq2_ref[...] * lk2_ref[...], axis=-1, keepdims=True)) + LAM_INIT)
    first_map = lane < HEAD_DIM
    subg = subg_ref[...]
    nq = n // ATTN_QB

    nctx = kc_ref.shape[0]
    gsz = n // ATTN_GROUPS
    groups = [[(k_ref, vt_ref, g * gsz, (g + 1) * gsz, nctx + g * gsz)] for g in range(ATTN_GROUPS)]
    groups[0].insert(0, (kc_ref, vtc_ref, 0, nctx, 0))

    def scores(i, s_scr):
        r0 = pl.multiple_of(i * ATTN_QB, ATTN_QB)
        q = q_ref[pl.ds(r0, ATTN_QB), :]
        zero = jnp.zeros_like(q)
        qms = (jnp.where(first_map, q, zero), jnp.where(first_map, zero, q))
        ms = [None, None]
        for pieces in groups:
            for kr, _, lo, hi, dst in pieces:
                for mp in range(2):
                    s = lax.dot_general(kr[lo:hi, :], qms[mp], (((1,), (1,)), ((), ())),
                                        preferred_element_type=F32)
                    s_scr[mp, dst:dst + hi - lo, :] = s
                    mg = jnp.max(s, axis=0, keepdims=True)
                    ms[mp] = mg if ms[mp] is None else jnp.maximum(ms[mp], mg)
            yield tuple(ms)

    def store_block(i, accs):
        ots = [a[0:V_DIM, :] / a[V_DIM:V_DIM + 1, :] for a in accs]
        ot = ots[0] - lam * ots[1]
        ot = ot * lax.rsqrt(jnp.mean(ot * ot, axis=0, keepdims=True) + EPS)
        o = ot.T * (subg * (1.0 - LAM_INIT))
        r0 = pl.multiple_of(i * ATTN_QB, ATTN_QB)
        o_ref[pl.ds(r0, ATTN_QB), :] = o.astype(o_ref.dtype)

    def finish(i, s_scr, ms, defer=False):
        accs = [None, None]
        for g, pieces in enumerate(groups):
            for _, vr, lo, hi, dst in pieces:
                for mp in range(2):
                    p = jnp.exp2(s_scr[mp, dst:dst + hi - lo, :] - ms[mp]).astype(BF16)
                    part = jnp.dot(vr[:, lo:hi], p, preferred_element_type=F32)
                    accs[mp] = part if accs[mp] is None else accs[mp] + part
            if g == len(groups) - 1:
                if defer:
                    acc_scr[0], acc_scr[1] = accs
                else:
                    store_block(i, accs)
            yield None

    c = RET_C
    nc = n // c

    def log_gamma(dec):
        z = -dec
        return -(jnp.maximum(z, 0.0) + jnp.log(1.0 + jnp.exp(-jnp.abs(z))))

    hsel = lax.broadcasted_iota(jnp.int32, (1, HEADS), 1) == head

    def pick(vec):
        return jnp.sum(jnp.where(hsel, vec, 0.0), axis=-1, keepdims=True)

    rmask = (lane >= HEAD_DIM).astype(jnp.int32) == head % 2
    gn = gn_ref[...]
    rt = {}

    def kv_state(k, v, kdec):
        kd = (k.astype(F32) * kdec).astype(BF16)
        return lax.dot_general(kd, v, (((0,), (0,)), ((), ())), preferred_element_type=F32)

    def backward_states():
        lgf, lgb = pick(log_gamma(decf_ref[...])), pick(log_gamma(decb_ref[...]))
        ii = lax.broadcasted_iota(jnp.int32, (c, c), 0).astype(F32)
        jj = lax.broadcasted_iota(jnp.int32, (c, c), 1).astype(F32)
        rel = ii - jj
        pos = lax.broadcasted_iota(jnp.int32, (c, LANES), 0).astype(F32)
        rt["decay"] = (jnp.where(rel >= 0, jnp.exp(jnp.maximum(rel, 0.0) * lgf), 0.0)
                       + jnp.where(rel <= 0, jnp.exp(jnp.maximum(-rel, 0.0) * lgb), 0.0))
        rt["qdec_f"], rt["qdec_b"] = jnp.exp((pos + 1.0) * lgf), jnp.exp((c - pos) * lgb)
        rt["kdec_f"], kdec_b = jnp.exp((c - 1.0 - pos) * lgf), jnp.exp(pos * lgb)
        rt["cdec_f"], cdec_b = jnp.exp(c * lgf), jnp.exp(c * lgb)
        rsf_scr[...] = kv_state(rkc_ref[...], rvc_ref[...], rt["kdec_f"])
        sb = kv_state(rkc_ref[...], rvc_ref[...], kdec_b)
        for ci in reversed(range(nc)):
            rsb_scr[ci] = sb.astype(BF16)
            if ci > 0:
                sb = sb * cdec_b + kv_state(rk_ref[ci * c:(ci + 1) * c, :], rv_ref[ci * c:(ci + 1) * c, :], kdec_b)
            yield None

    def ret_chunk(ci):
        r0 = pl.multiple_of(ci * c, c)
        q = rq_ref[pl.ds(r0, c), :]
        k = rk_ref[pl.ds(r0, c), :]
        v = rv_ref[pl.ds(r0, c), :]
        qm = jnp.where(rmask, q, jnp.zeros_like(q))
        sc = lax.dot_general(qm, k, (((1,), (1,)), ((), ())), preferred_element_type=F32)
        a = (sc * rt["decay"]).astype(BF16)
        yield None
        sf = rsf_scr[...]
        o = jnp.dot(a, v, preferred_element_type=F32)
        o = o + jnp.dot(qm, sf.astype(BF16), preferred_element_type=F32) * rt["qdec_f"]
        o = o + jnp.dot(qm, rsb_scr[ci], preferred_element_type=F32) * rt["qdec_b"]
        yield None
        mu = jnp.mean(o, axis=-1, keepdims=True)
        var = jnp.mean(jnp.square(o - mu), axis=-1, keepdims=True)
        on = (o - mu) * lax.rsqrt(var + EPS) * gn
        gate = rg_ref[pl.ds(r0, c), :].astype(F32)
        ro_ref[pl.ds(r0, c), :] = (on * gate).astype(ro_ref.dtype)
        yield None
        rsf_scr[...] = sf * rt["cdec_f"] + kv_state(k, v, rt["kdec_f"])
        yield None

    def run(*stages, side=None, side_steps=4 / ATTN_GROUPS):
        last = [None] * len(stages)
        done = 0
        for t, vals in enumerate(zip(*stages)):
            last = list(vals)
            while side is not None and done < (t + 1) * side_steps:
                next(side)
                done += 1
        return last

    def step(j, m_a):
        store_block(jnp.maximum(2 * j - 1, 0), (acc_scr[0], acc_scr[1]))
        m_b, _ = run(scores(2 * j + 1, sb_scr), finish(2 * j, sa_scr, m_a), side=ret_chunk(2 * j))
        m_a, _ = run(scores(2 * j + 2, sa_scr), finish(2 * j + 1, sb_scr, m_b, defer=True),
                     side=ret_chunk(2 * j + 1))
        return m_a

    acc_scr[...] = jnp.ones(acc_scr.shape, F32)
    (m_a,) = run(scores(0, sa_scr), side=backward_states(), side_steps=nc / ATTN_GROUPS)
    m_a = lax.fori_loop(0, nq // 2 - 1, step, m_a)
    store_block(nq - 3, (acc_scr[0], acc_scr[1]))
    m_b, _ = run(scores(nq - 1, sb_scr), finish(nq - 2, sa_scr, m_a), side=ret_chunk(nq - 2))
    run(finish(nq - 1, sb_scr, m_b), side=ret_chunk(nq - 1))


def _token_mixers(lq1, lk1, lq2, lk2, subg, qd, kd, vt, kd_c, vt_c,
                  dec_f, dec_b, gn_g, qr, kr, vr, gr, kr_c, vr_c):
    b, n, _ = qd.shape
    nctx = kd_c.shape[1]
    assert (n // ATTN_QB) % 2 == 0 and n % (ATTN_GROUPS * MXU_DIM) == 0 and nctx % MXU_DIM == 0
    assert ATTN_GROUPS >= 4 and nctx == RET_C and RET_C == ATTN_QB
    vec = lambda w: pl.BlockSpec((1, w), lambda bi, h: (0, 0))
    head = lambda rows: pl.BlockSpec((None, rows, LANES), lambda bi, h: (bi, 0, h))
    pair = lambda rows: pl.BlockSpec((None, rows, LANES), lambda bi, h: (bi, 0, h // 2))
    headt = lambda cols: pl.BlockSpec((None, None, VT_ROWS, cols), lambda bi, h: (bi, h, 0, 0))
    out = jax.ShapeDtypeStruct((b, n, HEADS * V_DIM), BF16)
    return pl.pallas_call(
        _mixer_kernel,
        grid=(b, HEADS),
        in_specs=[vec(HEAD_DIM)] * 4 + [vec(V_DIM), head(n), head(n), headt(n), head(nctx), headt(nctx),
                  vec(HEADS), vec(HEADS), pl.BlockSpec((1, V_DIM), lambda bi, h: (0, h)),
                  pair(n), pair(n), head(n), head(n), pair(nctx), head(nctx)],
        out_specs=[head(n), head(n)],
        out_shape=[out, out],
        scratch_shapes=[pltpu.VMEM((2, n + nctx, ATTN_QB), F32)] * 2
        + [pltpu.VMEM((n // RET_C, LANES, V_DIM), BF16), pltpu.VMEM((LANES, V_DIM), F32),
           pltpu.VMEM((2, VT_ROWS, ATTN_QB), F32)],
        compiler_params=_cparams(("arbitrary", "arbitrary")),
        name="token_mixers",
    )(lq1, lk1, lq2, lk2, subg, qd, kd, vt, kd_c, vt_c, dec_f, dec_b, gn_g, qr, kr, vr, gr, kr_c, vr_c)


def _mlp_kernel(x_ref, od_ref, or_ref, g1_ref, sh2_ref, sc2_ref, g2_ref, n2_ref, fg_ref,
                wo_ref, w1_ref, w2_ref, o_ref):
    half = od_ref.shape[1]
    y = (jnp.dot(od_ref[...], wo_ref[0:half, :].astype(BF16), preferred_element_type=F32)
         + jnp.dot(or_ref[...], wo_ref[half:2 * half, :].astype(BF16), preferred_element_type=F32))
    x1 = x_ref[...] + g1_ref[...] * y
    h = _rmsnorm(x1, n2_ref[...]) * (1.0 + sc2_ref[...]) + sh2_ref[...]
    hb = h.astype(BF16)
    d_ff = w1_ref.shape[1]
    acc = None
    for f0 in range(0, d_ff, FF_CHUNK):
        u = jnp.dot(hb, w1_ref[:, f0:f0 + FF_CHUNK].astype(BF16), preferred_element_type=F32)
        u = jnp.square(jnp.maximum(u, 0.0)).astype(BF16)
        part = jnp.dot(u, w2_ref[f0:f0 + FF_CHUNK, :].astype(BF16), preferred_element_type=F32)
        acc = part if acc is None else acc + part
    x2 = x1 + g2_ref[...] * acc
    o_ref[...] = _rmsnorm(x2, fg_ref[...])


def _out_mlp(x, od, orr, g1, sh2, sc2, g2, n2, fg, wo, w1, w2):
    b, n, d = x.shape
    tm = MLP_TM
    tok = lambda w: pl.BlockSpec((None, tm, w), lambda bi, t: (bi, t, 0))
    row = pl.BlockSpec((None, 1, d), lambda bi, t: (bi, 0, 0))
    gain = pl.BlockSpec((1, d), lambda bi, t: (0, 0))
    resident = lambda a: pl.BlockSpec(a.shape, lambda bi, t: (0, 0), pipeline_mode=pl.Buffered(1))
    return pl.pallas_call(
        _mlp_kernel,
        grid=(b, n // tm),
        in_specs=[tok(d), tok(od.shape[2]), tok(orr.shape[2]), row, row, row, row, gain, gain,
                  resident(wo), resident(w1), resident(w2)],
        out_specs=tok(d),
        out_shape=jax.ShapeDtypeStruct((b, n, d), F32),
        compiler_params=_cparams(("arbitrary", "arbitrary")),
        name="out_mlp",
    )(x, od, orr, g1, sh2, sc2, g2, n2, fg, wo, w1, w2)


def kernel(x, c, ctx, c_ctx, w_ada, b_ada, norm1_g, norm2_g, w_in, lambda_q1, lambda_k1, lambda_q2,
           lambda_k2, diff_subln_g, ret_decay_fwd, ret_decay_bwd, ret_gn_g, w_out, w_mlp1, w_mlp2, final_g):
    assert w_ada.shape[0] == 1, "single-layer block"
    b, n, d = x.shape

    c_rows = jnp.concatenate([c, c_ctx[None], jnp.zeros((8 - b - 1, d), F32)], axis=0)
    mod = _adaln(c_rows, w_ada[0], b_ada).reshape(8, N_MOD, d)
    sh1, sc1, g1, sh2, sc2, g2 = [mod[:b, i][:, None, :] for i in range(N_MOD)]

    qd, kd, vt, qr, kr, vr, gr, kd_c, vt_c, kr_c, vr_c = _project(
        x, ctx, sh1, sc1, mod[b, 0][None], mod[b, 1][None], norm1_g, w_in[0], _rope_tables(n))

    od, orr = _token_mixers(lambda_q1, lambda_k1, lambda_q2, lambda_k2, diff_subln_g, qd, kd, vt, kd_c, vt_c,
                            ret_decay_fwd, ret_decay_bwd, ret_gn_g, qr, kr, vr, gr, kr_c, vr_c)

    return _out_mlp(x, od, orr, g1, sh2, sc2, g2, norm2_g, final_g[None],
                    w_out[0], w_mlp1[0], w_mlp2[0])
```

```python
import math

import numpy as np
import jax
import jax.numpy as jnp
from jax import lax
from jax.experimental import pallas as pl
from jax.experimental.pallas import tpu as pltpu

F32 = jnp.float32
BF16 = jnp.bfloat16

GRID_W = 64
HEADS = 4
HEAD_DIM = 64
V_DIM = 128
VT_ROWS = V_DIM + 16
ROPE_PAIRS = 16
ROPE_BASE = 10000.0
N_MOD = 6
EPS = 1e-6
LAM_INIT = 0.8 - 0.6 * math.exp(-0.3 * 0)
LOG2E = 1.4426950408889634
QK_SCALE = HEAD_DIM ** -0.5

LANES = 128
MXU_DIM = 256
VMEM_LIMIT_BYTES = 56 * 1024 * 1024

PROJ_TM = 1024
ATTN_QB = 256
ATTN_GROUPS = 8
RET_C = 256
MLP_TM = 512
FF_CHUNK = 1024

W_DQK = HEADS * 2 * HEAD_DIM
W_RQK = HEADS * HEAD_DIM
W_V = HEADS * V_DIM


def _col_ranges(*widths):
    edges = [sum(widths[:i]) for i in range(len(widths) + 1)]
    return list(zip(edges[:-1], edges[1:]))


COL_QD, COL_KD, COL_VD, COL_QR, COL_KR, COL_VR, COL_GR = _col_ranges(W_DQK, W_DQK, W_V, W_RQK, W_RQK, W_V, W_V)


def _cparams(sem):
    return pltpu.CompilerParams(dimension_semantics=sem, vmem_limit_bytes=VMEM_LIMIT_BYTES)


def _rmsnorm(xf, g):
    return xf * lax.rsqrt(jnp.mean(xf * xf, axis=-1, keepdims=True) + EPS) * g


def _silu(x):
    return x * (1.0 / (1.0 + jnp.exp(-x)))


def _rope_tables(n_tokens):
    rows = n_tokens // GRID_W
    row = np.repeat(np.arange(rows), GRID_W).astype(np.float64)
    col = np.tile(np.arange(GRID_W), rows).astype(np.float64)
    inv = (np.float32(ROPE_BASE) ** (-np.arange(ROPE_PAIRS, dtype=np.float32) / ROPE_PAIRS)).astype(np.float64)
    ang_r = row[:, None] * inv
    ang_c = col[:, None] * inv
    zeros = np.zeros_like(ang_r)
    cos64 = np.concatenate([np.cos(ang_r)] * 2 + [np.cos(ang_c)] * 2, axis=1)
    sa64 = np.concatenate([-np.sin(ang_r), zeros, -np.sin(ang_c), zeros], axis=1)
    sb64 = np.concatenate([zeros, np.sin(ang_r), zeros, np.sin(ang_c)], axis=1)
    dup = lambda t: jnp.asarray(np.concatenate([t, t], axis=1), dtype=F32)
    return dup(cos64), dup(sa64), dup(sb64)


def _modulated(x_ref, shift, scale, g_ref):
    h = _rmsnorm(x_ref[...], g_ref[...])
    return (h * (1.0 + scale) + shift).astype(BF16)


def _proj_kernel(x_ref, g_ref, w_ref, c_ref, sa_ref, sb_ref, xc_ref, cond_ref, wa1_ref, ba1_ref, wa2_ref, ba2_ref,
                 qd_ref, kd_ref, vt_ref, qr_ref, kr_ref, vr_ref, gr_ref,
                 kdc_ref, vtc_ref, krc_ref, vrc_ref, mod2_ref, mod1_scr):
    d = x_ref.shape[1]
    sample = pl.program_id(0)
    n_samples = pl.num_programs(0)

    cond = _silu(cond_ref[...]).astype(BF16)

    @pl.when((sample == 0) & (pl.program_id(1) == 0))
    def _():
        mod1_scr[...] = jnp.dot(cond, wa1_ref[...].astype(BF16), preferred_element_type=F32) + ba1_ref[...]

    mod2_ref[...] = jnp.dot(cond, wa2_ref[...].astype(BF16), preferred_element_type=F32) + ba2_ref[...]

    def project(hb, cols):
        return jnp.dot(hb, w_ref[:, cols[0]:cols[1]].astype(BF16), preferred_element_type=F32)

    def put_vt(ref, vd):
        ones = jnp.ones((VT_ROWS - V_DIM, vd.shape[0]), BF16)
        for h in range(HEADS):
            ref[h, 0:V_DIM, :] = vd[:, h * V_DIM:(h + 1) * V_DIM].T.astype(BF16)
            ref[h, V_DIM:VT_ROWS, :] = ones

    hb = _modulated(x_ref, mod1_scr[pl.ds(sample, 1), 0:d], mod1_scr[pl.ds(sample, 1), d:2 * d], g_ref)
    c, sa, sb = c_ref[...], sa_ref[...], sb_ref[...]

    def rotate(y):
        slabs = []
        for s in range(y.shape[1] // LANES):
            ys = y[:, s * LANES:(s + 1) * LANES]
            slabs.append(ys * c + pltpu.roll(ys, LANES - ROPE_PAIRS, 1) * sa
                         + pltpu.roll(ys, ROPE_PAIRS, 1) * sb)
        return jnp.concatenate(slabs, axis=1)

    qd_ref[...] = (rotate(project(hb, COL_QD)) * (QK_SCALE * LOG2E)).astype(BF16)
    kd_ref[...] = rotate(project(hb, COL_KD)).astype(BF16)
    put_vt(vt_ref, project(hb, COL_VD))
    gr_ref[...] = _silu(project(hb, COL_GR)).astype(BF16)
    qr_ref[...] = rotate(project(hb, COL_QR)).astype(BF16)
    kr_ref[...] = (rotate(project(hb, COL_KR)) * QK_SCALE).astype(BF16)
    vr_ref[...] = project(hb, COL_VR).astype(BF16)

    @pl.when(pl.program_id(1) == 0)
    def _():
        hc = _modulated(xc_ref, mod1_scr[pl.ds(n_samples, 1), 0:d], mod1_scr[pl.ds(n_samples, 1), d:2 * d], g_ref)
        kdc_ref[...] = project(hc, COL_KD).astype(BF16)
        put_vt(vtc_ref, project(hc, COL_VD))
        krc_ref[...] = (project(hc, COL_KR) * QK_SCALE).astype(BF16)
        vrc_ref[...] = project(hc, COL_VR).astype(BF16)


def _project(x, ctx, cond_rows, w_ada, b_ada, g, w_in, tables):
    b, n, d = x.shape
    nctx = ctx.shape[1]
    tm = PROJ_TM
    tok = lambda w: pl.BlockSpec((None, tm, w), lambda bi, t: (bi, t, 0))
    ctok = lambda w: pl.BlockSpec((None, nctx, w), lambda bi, t: (bi, 0, 0))
    nt = n // tm
    wblk = (N_MOD - 2) * d // (b * nt)
    assert wblk % LANES == 0 and (2 * d) % wblk == 0
    shared = pl.BlockSpec((1, d), lambda bi, t: (0, 0))
    later = lambda rows: pl.BlockSpec((rows, wblk), lambda bi, t: (0, 2 * d // wblk + bi * nt + t))
    table = pl.BlockSpec((tm, LANES), lambda bi, t: (t, 0))
    out_specs = [tok(W_DQK), tok(W_DQK),
                 pl.BlockSpec((None, HEADS, VT_ROWS, tm), lambda bi, t: (bi, 0, 0, t)),
                 tok(W_RQK), tok(W_RQK), tok(W_V), tok(W_V),
                 ctok(W_DQK), pl.BlockSpec((None, HEADS, VT_ROWS, nctx), lambda bi, t: (bi, 0, 0, 0)),
                 ctok(W_RQK), ctok(W_V),
                 pl.BlockSpec((cond_rows.shape[0], wblk), lambda bi, t: (0, bi * nt + t))]
    out_shape = [jax.ShapeDtypeStruct((b, n, W_DQK), BF16),
                 jax.ShapeDtypeStruct((b, n, W_DQK), BF16),
                 jax.ShapeDtypeStruct((b, HEADS, VT_ROWS, n), BF16),
                 jax.ShapeDtypeStruct((b, n, W_RQK), BF16),
                 jax.ShapeDtypeStruct((b, n, W_RQK), BF16),
                 jax.ShapeDtypeStruct((b, n, W_V), BF16),
                 jax.ShapeDtypeStruct((b, n, W_V), BF16),
                 jax.ShapeDtypeStruct((b, nctx, W_DQK), BF16),
                 jax.ShapeDtypeStruct((b, HEADS, VT_ROWS, nctx), BF16),
                 jax.ShapeDtypeStruct((b, nctx, W_RQK), BF16),
                 jax.ShapeDtypeStruct((b, nctx, W_V), BF16),
                 jax.ShapeDtypeStruct((cond_rows.shape[0], (N_MOD - 2) * d), F32)]
    return pl.pallas_call(
        _proj_kernel,
        grid=(b, n // tm),
        in_specs=[tok(d), shared,
                  pl.BlockSpec(w_in.shape, lambda bi, t: (0, 0), pipeline_mode=pl.Buffered(1)),
                  table, table, table, ctok(d),
                  pl.BlockSpec(cond_rows.shape, lambda bi, t: (0, 0)),
                  pl.BlockSpec((d, 2 * d), lambda bi, t: (0, 0), pipeline_mode=pl.Buffered(1)),
                  pl.BlockSpec((1, 2 * d), lambda bi, t: (0, 0)),
                  later(d), later(1)],
        out_specs=out_specs,
        out_shape=out_shape,
        scratch_shapes=[pltpu.VMEM((cond_rows.shape[0], 2 * d), F32)],
        compiler_params=_cparams(("arbitrary", "arbitrary")),
        name="proj",
    )(x, g, w_in, *tables, ctx, cond_rows, w_ada, b_ada, w_ada, b_ada)


def _mixer_kernel(lq1_ref, lk1_ref, lq2_ref, lk2_ref, subg_ref, q_ref, k_ref, vt_ref, kc_ref, vtc_ref,
                  decf_ref, decb_ref, gn_ref, rq_ref, rk_ref, rv_ref, rg_ref, rkc_ref, rvc_ref,
                  o_ref, ro_ref, sa_scr, sb_scr, rsb_scr, rsf_scr, acc_scr):
    n = q_ref.shape[0]
    head = pl.program_id(1)
    lane = lax.broadcasted_iota(jnp.int32, (1, LANES), 1)

    lam = (jnp.exp(jnp.sum(lq1_ref[...] * lk1_ref[...], axis=-1, keepdims=True))
           - jnp.exp(jnp.sum(lq2_ref[...] * lk2_ref[...], axis=-1, keepdims=True)) + LAM_INIT)
    first_map = lane < HEAD_DIM
    subg = subg_ref[...]
    nq = n // ATTN_QB

    nctx = kc_ref.shape[0]
    gsz = n // ATTN_GROUPS
    groups = [[(k_ref, vt_ref, g * gsz, (g + 1) * gsz, nctx + g * gsz)] for g in range(ATTN_GROUPS)]
    groups[0].insert(0, (kc_ref, vtc_ref, 0, nctx, 0))

    def scores(i, s_scr):
        r0 = pl.multiple_of(i * ATTN_QB, ATTN_QB)
        q = q_ref[pl.ds(r0, ATTN_QB), :]
        zero = jnp.zeros_like(q)
        qms = (jnp.where(first_map, q, zero), jnp.where(first_map, zero, q))
        ms = [None, None]
        for pieces in groups:
            for kr, _, lo, hi, dst in pieces:
                for mp in range(2):
                    s = lax.dot_general(kr[lo:hi, :], qms[mp], (((1,), (1,)), ((), ())),
                                        preferred_element_type=F32)
                    s_scr[mp, dst:dst + hi - lo, :] = s
                    mg = jnp.max(s, axis=0, keepdims=True)
                    ms[mp] = mg if ms[mp] is None else jnp.maximum(ms[mp], mg)
            yield tuple(ms)

    def store_block(i, accs):
        ots = [a[0:V_DIM, :] / a[V_DIM:V_DIM + 1, :] for a in accs]
        ot = ots[0] - lam * ots[1]
        ot = ot * lax.rsqrt(jnp.mean(ot * ot, axis=0, keepdims=True) + EPS)
        o = ot.T * (subg * (1.0 - LAM_INIT))
        r0 = pl.multiple_of(i * ATTN_QB, ATTN_QB)
        o_ref[pl.ds(r0, ATTN_QB), :] = o.astype(o_ref.dtype)

    def finish(i, s_scr, ms, defer=False):
        accs = [None, None]
        for g, pieces in enumerate(groups):
            for _, vr, lo, hi, dst in pieces:
                for mp in range(2):
                    p = jnp.exp2(s_scr[mp, dst:dst + hi - lo, :] - ms[mp]).astype(BF16)
                    part = jnp.dot(vr[:, lo:hi], p, preferred_element_type=F32)
                    accs[mp] = part if accs[mp] is None else accs[mp] + part
            if g == len(groups) - 1:
                if defer:
                    acc_scr[0], acc_scr[1] = accs
                else:
                    store_block(i, accs)
            yield None

    c = RET_C
    nc = n // c

    def log_gamma(dec):
        z = -dec
        return -(jnp.maximum(z, 0.0) + jnp.log(1.0 + jnp.exp(-jnp.abs(z))))

    hsel = lax.broadcasted_iota(jnp.int32, (1, HEADS), 1) == head

    def pick(vec):
        return jnp.sum(jnp.where(hsel, vec, 0.0), axis=-1, keepdims=True)

    rmask = (lane >= HEAD_DIM).astype(jnp.int32) == head % 2
    gn = gn_ref[...]
    rt = {}

    def kv_state(k, v, kdec):
        kd = (k.astype(F32) * kdec).astype(BF16)
        return lax.dot_general(kd, v, (((0,), (0,)), ((), ())), preferred_element_type=F32)

    def backward_states():
        lgf, lgb = pick(log_gamma(decf_ref[...])), pick(log_gamma(decb_ref[...]))
        ii = lax.broadcasted_iota(jnp.int32, (c, c), 0).astype(F32)
        jj = lax.broadcasted_iota(jnp.int32, (c, c), 1).astype(F32)
        rel = ii - jj
        pos = lax.broadcasted_iota(jnp.int32, (c, LANES), 0).astype(F32)
        rt["decay"] = (jnp.where(rel >= 0, jnp.exp(jnp.maximum(rel, 0.0) * lgf), 0.0)
                       + jnp.where(rel <= 0, jnp.exp(jnp.maximum(-rel, 0.0) * lgb), 0.0))
        rt["qdec_f"], rt["qdec_b"] = jnp.exp((pos + 1.0) * lgf), jnp.exp((c - pos) * lgb)
        rt["kdec_f"], kdec_b = jnp.exp((c - 1.0 - pos) * lgf), jnp.exp(pos * lgb)
        rt["cdec_f"], cdec_b = jnp.exp(c * lgf), jnp.exp(c * lgb)
        rsf_scr[...] = kv_state(rkc_ref[...], rvc_ref[...], rt["kdec_f"])
        sb = kv_state(rkc_ref[...], rvc_ref[...], kdec_b)
        for ci in reversed(range(nc)):
            rsb_scr[ci] = sb.astype(BF16)
            if ci > 0:
                sb = sb * cdec_b + kv_state(rk_ref[ci * c:(ci + 1) * c, :], rv_ref[ci * c:(ci + 1) * c, :], kdec_b)
            yield None

    def ret_chunk(ci):
        r0 = pl.multiple_of(ci * c, c)
        q = rq_ref[pl.ds(r0, c), :]
        k = rk_ref[pl.ds(r0, c), :]
        v = rv_ref[pl.ds(r0, c), :]
        qm = jnp.where(rmask, q, jnp.zeros_like(q))
        sc = lax.dot_general(qm, k, (((1,), (1,)), ((), ())), preferred_element_type=F32)
        a = (sc * rt["decay"]).astype(BF16)
        yield None
        sf = rsf_scr[...]
        o = jnp.dot(a, v, preferred_element_type=F32)
        o = o + jnp.dot(qm, sf.astype(BF16), preferred_element_type=F32) * rt["qdec_f"]
        o = o + jnp.dot(qm, rsb_scr[ci], preferred_element_type=F32) * rt["qdec_b"]
        yield None
        mu = jnp.mean(o, axis=-1, keepdims=True)
        var = jnp.mean(jnp.square(o - mu), axis=-1, keepdims=True)
        on = (o - mu) * lax.rsqrt(var + EPS) * gn
        gate = rg_ref[pl.ds(r0, c), :].astype(F32)
        ro_ref[pl.ds(r0, c), :] = (on * gate).astype(ro_ref.dtype)
        yield None
        rsf_scr[...] = sf * rt["cdec_f"] + kv_state(k, v, rt["kdec_f"])
        yield None

    def run(*stages, side=None, side_steps=4 / ATTN_GROUPS):
        last = [None] * len(stages)
        done = 0
        for t, vals in enumerate(zip(*stages)):
            last = list(vals)
            while side is not None and done < (t + 1) * side_steps:
                next(side)
                done += 1
        return last

    def step(j, m_a):
        store_block(jnp.maximum(2 * j - 1, 0), (acc_scr[0], acc_scr[1]))
        m_b, _ = run(scores(2 * j + 1, sb_scr), finish(2 * j, sa_scr, m_a), side=ret_chunk(2 * j))
        m_a, _ = run(scores(2 * j + 2, sa_scr), finish(2 * j + 1, sb_scr, m_b, defer=True),
                     side=ret_chunk(2 * j + 1))
        return m_a

    acc_scr[...] = jnp.ones(acc_scr.shape, F32)
    (m_a,) = run(scores(0, sa_scr), side=backward_states(), side_steps=nc / ATTN_GROUPS)
    m_a = lax.fori_loop(0, nq // 2 - 1, step, m_a)
    store_block(nq - 3, (acc_scr[0], acc_scr[1]))
    m_b, _ = run(scores(nq - 1, sb_scr), finish(nq - 2, sa_scr, m_a), side=ret_chunk(nq - 2))
    run(finish(nq - 1, sb_scr, m_b), side=ret_chunk(nq - 1))


def _token_mixers(lq1, lk1, lq2, lk2, subg, qd, kd, vt, kd_c, vt_c,
                  dec_f, dec_b, gn_g, qr, kr, vr, gr, kr_c, vr_c):
    b, n, _ = qd.shape
    nctx = kd_c.shape[1]
    assert (n // ATTN_QB) % 2 == 0 and n % (ATTN_GROUPS * MXU_DIM) == 0 and nctx % MXU_DIM == 0
    assert ATTN_GROUPS >= 4 and nctx == RET_C and RET_C == ATTN_QB
    vec = lambda w: pl.BlockSpec((1, w), lambda bi, h: (0, 0))
    head = lambda rows: pl.BlockSpec((None, rows, LANES), lambda bi, h: (bi, 0, h))
    pair = lambda rows: pl.BlockSpec((None, rows, LANES), lambda bi, h: (bi, 0, h // 2))
    headt = lambda cols: pl.BlockSpec((None, None, VT_ROWS, cols), lambda bi, h: (bi, h, 0, 0))
    out = jax.ShapeDtypeStruct((b, n, HEADS * V_DIM), BF16)
    return pl.pallas_call(
        _mixer_kernel,
        grid=(b, HEADS),
        in_specs=[vec(HEAD_DIM)] * 4 + [vec(V_DIM), head(n), head(n), headt(n), head(nctx), headt(nctx),
                  vec(HEADS), vec(HEADS), pl.BlockSpec((1, V_DIM), lambda bi, h: (0, h)),
                  pair(n), pair(n), head(n), head(n), pair(nctx), head(nctx)],
        out_specs=[head(n), head(n)],
        out_shape=[out, out],
        scratch_shapes=[pltpu.VMEM((2, n + nctx, ATTN_QB), F32)] * 2
        + [pltpu.VMEM((n // RET_C, LANES, V_DIM), BF16), pltpu.VMEM((LANES, V_DIM), F32),
           pltpu.VMEM((2, VT_ROWS, ATTN_QB), F32)],
        compiler_params=_cparams(("arbitrary", "arbitrary")),
        name="token_mixers",
    )(lq1, lk1, lq2, lk2, subg, qd, kd, vt, kd_c, vt_c, dec_f, dec_b, gn_g, qr, kr, vr, gr, kr_c, vr_c)


def _mlp_kernel(x_ref, od_ref, or_ref, g1_ref, sh2_ref, sc2_ref, g2_ref, n2_ref, fg_ref,
                wo_ref, w1_ref, w2_ref, o_ref):
    half = od_ref.shape[1]
    y = (jnp.dot(od_ref[...], wo_ref[0:half, :].astype(BF16), preferred_element_type=F32)
         + jnp.dot(or_ref[...], wo_ref[half:2 * half, :].astype(BF16), preferred_element_type=F32))
    x1 = x_ref[...] + g1_ref[...] * y
    h = _rmsnorm(x1, n2_ref[...]) * (1.0 + sc2_ref[...]) + sh2_ref[...]
    hb = h.astype(BF16)
    d_ff = w1_ref.shape[1]
    acc = None
    for f0 in range(0, d_ff, FF_CHUNK):
        u = jnp.dot(hb, w1_ref[:, f0:f0 + FF_CHUNK].astype(BF16), preferred_element_type=F32)
        u = jnp.square(jnp.maximum(u, 0.0)).astype(BF16)
        part = jnp.dot(u, w2_ref[f0:f0 + FF_CHUNK, :].astype(BF16), preferred_element_type=F32)
        acc = part if acc is None else acc + part
    x2 = x1 + g2_ref[...] * acc
    o_ref[...] = _rmsnorm(x2, fg_ref[...])


def _out_mlp(x, od, orr, g1, sh2, sc2, g2, n2, fg, wo, w1, w2):
    b, n, d = x.shape
    tm = MLP_TM
    tok = lambda w: pl.BlockSpec((None, tm, w), lambda bi, t: (bi, t, 0))
    row = pl.BlockSpec((None, 1, d), lambda bi, t: (bi, 0, 0))
    gain = pl.BlockSpec((1, d), lambda bi, t: (0, 0))
    resident = lambda a: pl.BlockSpec(a.shape, lambda bi, t: (0, 0), pipeline_mode=pl.Buffered(1))
    return pl.pallas_call(
        _mlp_kernel,
        grid=(b, n // tm),
        in_specs=[tok(d), tok(od.shape[2]), tok(orr.shape[2]), row, row, row, row, gain, gain,
                  resident(wo), resident(w1), resident(w2)],
        out_specs=tok(d),
        out_shape=jax.ShapeDtypeStruct((b, n, d), F32),
        compiler_params=_cparams(("arbitrary", "arbitrary")),
        name="out_mlp",
    )(x, od, orr, g1, sh2, sc2, g2, n2, fg, wo, w1, w2)


def kernel(x, c, ctx, c_ctx, w_ada, b_ada, norm1_g, norm2_g, w_in, lambda_q1, lambda_k1, lambda_q2,
           lambda_k2, diff_subln_g, ret_decay_fwd, ret_decay_bwd, ret_gn_g, w_out, w_mlp1, w_mlp2, final_g):
    assert w_ada.shape[0] == 1, "single-layer block"
    b, n, d = x.shape

    c_rows = jnp.concatenate([c, c_ctx[None], jnp.zeros((8 - b - 1, d), F32)], axis=0)
    qd, kd, vt, qr, kr, vr, gr, kd_c, vt_c, kr_c, vr_c, mod2 = _project(
        x, ctx, c_rows, w_ada[0], b_ada, norm1_g, w_in[0], _rope_tables(n))
    g1, sh2, sc2, g2 = [mod2.reshape(8, N_MOD - 2, d)[:b, i][:, None, :] for i in range(N_MOD - 2)]

    od, orr = _token_mixers(lambda_q1, lambda_k1, lambda_q2, lambda_k2, diff_subln_g, qd, kd, vt, kd_c, vt_c,
                            ret_decay_fwd, ret_decay_bwd, ret_gn_g, qr, kr, vr, gr, kr_c, vr_c)

    return _out_mlp(x, od, orr, g1, sh2, sc2, g2, norm2_g, final_g[None],
                    w_out[0], w_mlp1[0], w_mlp2[0])
```

```python
import math

import numpy as np
import jax
import jax.numpy as jnp
from jax import lax
from jax.experimental import pallas as pl
from jax.experimental.pallas import tpu as pltpu

F32 = jnp.float32
BF16 = jnp.bfloat16

GRID_W = 64
HEADS = 4
HEAD_DIM = 64
V_DIM = 128
VT_ROWS = V_DIM + 16
ROPE_PAIRS = 16
ROPE_BASE = 10000.0
N_MOD = 6
EPS = 1e-6
LAM_INIT = 0.8 - 0.6 * math.exp(-0.3 * 0)
LOG2E = 1.4426950408889634
QK_SCALE = HEAD_DIM ** -0.5

LANES = 128
MXU_DIM = 256
VMEM_LIMIT_BYTES = 56 * 1024 * 1024

PROJ_TM = 1024
ATTN_QB = 256
ATTN_GROUPS = 8
RET_C = 256
MLP_TM = 512
FF_CHUNK = 1024

W_DQK = HEADS * 2 * HEAD_DIM
W_RQK = HEADS * HEAD_DIM
W_V = HEADS * V_DIM


def _col_ranges(*widths):
    edges = [sum(widths[:i]) for i in range(len(widths) + 1)]
    return list(zip(edges[:-1], edges[1:]))


COL_QD, COL_KD, COL_VD, COL_QR, COL_KR, COL_VR, COL_GR = _col_ranges(W_DQK, W_DQK, W_V, W_RQK, W_RQK, W_V, W_V)


def _cparams(sem):
    return pltpu.CompilerParams(dimension_semantics=sem, vmem_limit_bytes=VMEM_LIMIT_BYTES)


def _rmsnorm(xf, g):
    return xf * lax.rsqrt(jnp.mean(xf * xf, axis=-1, keepdims=True) + EPS) * g


def _silu(x):
    return x * (1.0 / (1.0 + jnp.exp(-x)))


def _rope_tables(n_tokens):
    rows = n_tokens // GRID_W
    row = np.repeat(np.arange(rows), GRID_W).astype(np.float64)
    col = np.tile(np.arange(GRID_W), rows).astype(np.float64)
    inv = (np.float32(ROPE_BASE) ** (-np.arange(ROPE_PAIRS, dtype=np.float32) / ROPE_PAIRS)).astype(np.float64)
    ang_r = row[:, None] * inv
    ang_c = col[:, None] * inv
    zeros = np.zeros_like(ang_r)
    cos64 = np.concatenate([np.cos(ang_r)] * 2 + [np.cos(ang_c)] * 2, axis=1)
    sa64 = np.concatenate([-np.sin(ang_r), zeros, -np.sin(ang_c), zeros], axis=1)
    sb64 = np.concatenate([zeros, np.sin(ang_r), zeros, np.sin(ang_c)], axis=1)
    dup = lambda t: jnp.asarray(np.concatenate([t, t], axis=1), dtype=F32)
    return dup(cos64), dup(sa64), dup(sb64)


def _modulated(x_ref, shift, scale, g_ref):
    h = _rmsnorm(x_ref[...], g_ref[...])
    return (h * (1.0 + scale) + shift).astype(BF16)


def _proj_kernel(x_ref, g_ref, w_ref, c_ref, sa_ref, sb_ref, xc_ref, cond_ref, wa1_ref, ba1_ref, wa2_ref, ba2_ref,
                 qd_ref, kd_ref, vt_ref, qr_ref, kr_ref, vr_ref, gr_ref,
                 kdc_ref, vtc_ref, krc_ref, vrc_ref, mod2_ref, mod1_scr):
    d = x_ref.shape[1]
    sample = pl.program_id(0)
    n_samples = pl.num_programs(0)

    cond = _silu(cond_ref[...]).astype(BF16)

    @pl.when((sample == 0) & (pl.program_id(1) == 0))
    def _():
        mod1_scr[...] = jnp.dot(cond, wa1_ref[...].astype(BF16), preferred_element_type=F32) + ba1_ref[...]

    mod2_ref[...] = jnp.dot(cond, wa2_ref[...].astype(BF16), preferred_element_type=F32) + ba2_ref[...]

    def project(hb, cols):
        return jnp.dot(hb, w_ref[:, cols[0]:cols[1]].astype(BF16), preferred_element_type=F32)

    def put_vt(ref, vd):
        ones = jnp.ones((VT_ROWS - V_DIM, vd.shape[0]), BF16)
        for h in range(HEADS):
            ref[h, 0:V_DIM, :] = vd[:, h * V_DIM:(h + 1) * V_DIM].T.astype(BF16)
            ref[h, V_DIM:VT_ROWS, :] = ones

    hb = _modulated(x_ref, mod1_scr[pl.ds(sample, 1), 0:d], mod1_scr[pl.ds(sample, 1), d:2 * d], g_ref)
    c, sa, sb = c_ref[...], sa_ref[...], sb_ref[...]

    def rotate(y):
        slabs = []
        for s in range(y.shape[1] // LANES):
            ys = y[:, s * LANES:(s + 1) * LANES]
            slabs.append(ys * c + pltpu.roll(ys, LANES - ROPE_PAIRS, 1) * sa
                         + pltpu.roll(ys, ROPE_PAIRS, 1) * sb)
        return jnp.concatenate(slabs, axis=1)

    qd_ref[...] = (rotate(project(hb, COL_QD)) * (QK_SCALE * LOG2E)).astype(BF16)
    kd_ref[...] = rotate(project(hb, COL_KD)).astype(BF16)
    put_vt(vt_ref, project(hb, COL_VD))
    gr_ref[...] = _silu(project(hb, COL_GR)).astype(BF16)
    qr_ref[...] = rotate(project(hb, COL_QR)).astype(BF16)
    kr_ref[...] = (rotate(project(hb, COL_KR)) * QK_SCALE).astype(BF16)
    vr_ref[...] = project(hb, COL_VR).astype(BF16)

    @pl.when(pl.program_id(1) == 0)
    def _():
        hc = _modulated(xc_ref, mod1_scr[pl.ds(n_samples, 1), 0:d], mod1_scr[pl.ds(n_samples, 1), d:2 * d], g_ref)
        kdc_ref[...] = project(hc, COL_KD).astype(BF16)
        put_vt(vtc_ref, project(hc, COL_VD))
        krc_ref[...] = (project(hc, COL_KR) * QK_SCALE).astype(BF16)
        vrc_ref[...] = project(hc, COL_VR).astype(BF16)


def _project(x, ctx, cond_rows, w_ada, b_ada, g, w_in, tables):
    b, n, d = x.shape
    nctx = ctx.shape[1]
    tm = PROJ_TM
    tok = lambda w: pl.BlockSpec((None, tm, w), lambda bi, t: (bi, t, 0))
    ctok = lambda w: pl.BlockSpec((None, nctx, w), lambda bi, t: (bi, 0, 0))
    nt = n // tm
    wblk = (N_MOD - 2) * d // (b * nt)
    assert wblk % LANES == 0 and (2 * d) % wblk == 0
    shared = pl.BlockSpec((1, d), lambda bi, t: (0, 0))
    later = lambda rows: pl.BlockSpec((rows, wblk), lambda bi, t: (0, 2 * d // wblk + bi * nt + t))
    table = pl.BlockSpec((tm, LANES), lambda bi, t: (t, 0))
    out_specs = [tok(W_DQK), tok(W_DQK),
                 pl.BlockSpec((None, HEADS, VT_ROWS, tm), lambda bi, t: (bi, 0, 0, t)),
                 tok(W_RQK), tok(W_RQK), tok(W_V), tok(W_V),
                 ctok(W_DQK), pl.BlockSpec((None, HEADS, VT_ROWS, nctx), lambda bi, t: (bi, 0, 0, 0)),
                 ctok(W_RQK), ctok(W_V),
                 pl.BlockSpec((cond_rows.shape[0], wblk), lambda bi, t: (0, bi * nt + t))]
    out_shape = [jax.ShapeDtypeStruct((b, n, W_DQK), BF16),
                 jax.ShapeDtypeStruct((b, n, W_DQK), BF16),
                 jax.ShapeDtypeStruct((b, HEADS, VT_ROWS, n), BF16),
                 jax.ShapeDtypeStruct((b, n, W_RQK), BF16),
                 jax.ShapeDtypeStruct((b, n, W_RQK), BF16),
                 jax.ShapeDtypeStruct((b, n, W_V), BF16),
                 jax.ShapeDtypeStruct((b, n, W_V), BF16),
                 jax.ShapeDtypeStruct((b, nctx, W_DQK), BF16),
                 jax.ShapeDtypeStruct((b, HEADS, VT_ROWS, nctx), BF16),
                 jax.ShapeDtypeStruct((b, nctx, W_RQK), BF16),
                 jax.ShapeDtypeStruct((b, nctx, W_V), BF16),
                 jax.ShapeDtypeStruct((cond_rows.shape[0], (N_MOD - 2) * d), F32)]
    return pl.pallas_call(
        _proj_kernel,
        grid=(b, n // tm),
        in_specs=[tok(d), shared,
                  pl.BlockSpec(w_in.shape, lambda bi, t: (0, 0), pipeline_mode=pl.Buffered(1)),
                  table, table, table, ctok(d),
                  pl.BlockSpec(cond_rows.shape, lambda bi, t: (0, 0)),
                  pl.BlockSpec((d, 2 * d), lambda bi, t: (0, 0), pipeline_mode=pl.Buffered(1)),
                  pl.BlockSpec((1, 2 * d), lambda bi, t: (0, 0)),
                  later(d), later(1)],
        out_specs=out_specs,
        out_shape=out_shape,
        scratch_shapes=[pltpu.VMEM((cond_rows.shape[0], 2 * d), F32)],
        compiler_params=_cparams(("arbitrary", "arbitrary")),
        name="proj",
    )(x, g, w_in, *tables, ctx, cond_rows, w_ada, b_ada, w_ada, b_ada)


def _mixer_kernel(lq1_ref, lk1_ref, lq2_ref, lk2_ref, subg_ref, q_ref, k_ref, vt_ref, kc_ref, vtc_ref,
                  decf_ref, decb_ref, gn_ref, rq_ref, rk_ref, rv_ref, rg_ref, rkc_ref, rvc_ref,
                  o_ref, ro_ref, sa_scr, sb_scr, rsb_scr, rsf_scr, acc_scr):
    n = q_ref.shape[0]
    head = pl.program_id(1)
    lane = lax.broadcasted_iota(jnp.int32, (1, LANES), 1)

    lam = (jnp.exp(jnp.sum(lq1_ref[...] * lk1_ref[...], axis=-1, keepdims=True))
           - jnp.exp(jnp.sum(lq2_ref[...] * lk2_ref[...], axis=-1, keepdims=True)) + LAM_INIT)
    first_map = lane < HEAD_DIM
    subg = subg_ref[...]
    nq = n // ATTN_QB

    nctx = kc_ref.shape[0]
    gsz = n // ATTN_GROUPS
    groups = [[(k_ref, vt_ref, g * gsz, (g + 1) * gsz, nctx + g * gsz)] for g in range(ATTN_GROUPS)]
    groups[0].insert(0, (kc_ref, vtc_ref, 0, nctx, 0))

    def scores(i, s_scr):
        r0 = pl.multiple_of(i * ATTN_QB, ATTN_QB)
        q = q_ref[pl.ds(r0, ATTN_QB), :]
        zero = jnp.zeros_like(q)
        qms = (jnp.where(first_map, q, zero), jnp.where(first_map, zero, q))
        ms = [None, None]
        for pieces in groups:
            for kr, _, lo, hi, dst in pieces:
                for mp in range(2):
                    s = lax.dot_general(kr[lo:hi, :], qms[mp], (((1,), (1,)), ((), ())),
                                        preferred_element_type=F32)
                    s_scr[mp, dst:dst + hi - lo, :] = s
                    mg = jnp.max(s, axis=0, keepdims=True)
                    ms[mp] = mg if ms[mp] is None else jnp.maximum(ms[mp], mg)
            yield tuple(ms)

    def store_block(i, accs):
        ots = [a[0:V_DIM, :] / a[V_DIM:V_DIM + 1, :] for a in accs]
        ot = ots[0] - lam * ots[1]
        ot = ot * lax.rsqrt(jnp.mean(ot * ot, axis=0, keepdims=True) + EPS)
        o = ot.T * (subg * (1.0 - LAM_INIT))
        r0 = pl.multiple_of(i * ATTN_QB, ATTN_QB)
        o_ref[pl.ds(r0, ATTN_QB), :] = o.astype(o_ref.dtype)

    def finish(i, s_scr, ms, defer=False):
        accs = [None, None]
        for g, pieces in enumerate(groups):
            for _, vr, lo, hi, dst in pieces:
                for mp in range(2):
                    p = jnp.exp2(s_scr[mp, dst:dst + hi - lo, :] - ms[mp]).astype(BF16)
                    part = jnp.dot(vr[:, lo:hi], p, preferred_element_type=F32)
                    accs[mp] = part if accs[mp] is None else accs[mp] + part
            if g == len(groups) - 1:
                if defer:
                    acc_scr[0], acc_scr[1] = accs
                else:
                    store_block(i, accs)
            yield None

    c = RET_C
    nc = n // c

    def log_gamma(dec):
        z = -dec
        return -(jnp.maximum(z, 0.0) + jnp.log(1.0 + jnp.exp(-jnp.abs(z))))

    hsel = lax.broadcasted_iota(jnp.int32, (1, HEADS), 1) == head

    def pick(vec):
        return jnp.sum(jnp.where(hsel, vec, 0.0), axis=-1, keepdims=True)

    rmask = (lane >= HEAD_DIM).astype(jnp.int32) == head % 2
    gn = gn_ref[...]
    rt = {}

    def kv_state(k, v, kdec):
        kd = (k.astype(F32) * kdec).astype(BF16)
        return lax.dot_general(kd, v, (((0,), (0,)), ((), ())), preferred_element_type=F32)

    def backward_states():
        lgf, lgb = pick(log_gamma(decf_ref[...])), pick(log_gamma(decb_ref[...]))
        ii = lax.broadcasted_iota(jnp.int32, (c, c), 0).astype(F32)
        jj = lax.broadcasted_iota(jnp.int32, (c, c), 1).astype(F32)
        rel = ii - jj
        pos = lax.broadcasted_iota(jnp.int32, (c, LANES), 0).astype(F32)
        rt["decay"] = (jnp.where(rel >= 0, jnp.exp(jnp.maximum(rel, 0.0) * lgf), 0.0)
                       + jnp.where(rel <= 0, jnp.exp(jnp.maximum(-rel, 0.0) * lgb), 0.0))
        rt["qdec_f"], rt["qdec_b"] = jnp.exp((pos + 1.0) * lgf), jnp.exp((c - pos) * lgb)
        rt["kdec_f"], kdec_b = jnp.exp((c - 1.0 - pos) * lgf), jnp.exp(pos * lgb)
        rt["cdec_f"], cdec_b = jnp.exp(c * lgf), jnp.exp(c * lgb)
        rsf_scr[...] = kv_state(rkc_ref[...], rvc_ref[...], rt["kdec_f"])
        sb = kv_state(rkc_ref[...], rvc_ref[...], kdec_b)
        for ci in reversed(range(nc)):
            rsb_scr[ci] = sb.astype(BF16)
            if ci > 0:
                sb = sb * cdec_b + kv_state(rk_ref[ci * c:(ci + 1) * c, :], rv_ref[ci * c:(ci + 1) * c, :], kdec_b)
            yield None

    def ret_chunk(ci):
        r0 = pl.multiple_of(ci * c, c)
        q = rq_ref[pl.ds(r0, c), :]
        k = rk_ref[pl.ds(r0, c), :]
        v = rv_ref[pl.ds(r0, c), :]
        qm = jnp.where(rmask, q, jnp.zeros_like(q))
        sc = lax.dot_general(qm, k, (((1,), (1,)), ((), ())), preferred_element_type=F32)
        a = (sc * rt["decay"]).astype(BF16)
        yield None
        sf = rsf_scr[...]
        o = jnp.dot(a, v, preferred_element_type=F32)
        o = o + jnp.dot(qm, sf.astype(BF16), preferred_element_type=F32) * rt["qdec_f"]
        o = o + jnp.dot(qm, rsb_scr[ci], preferred_element_type=F32) * rt["qdec_b"]
        yield None
        mu = jnp.mean(o, axis=-1, keepdims=True)
        var = jnp.mean(jnp.square(o - mu), axis=-1, keepdims=True)
        on = (o - mu) * lax.rsqrt(var + EPS) * gn
        gate = rg_ref[pl.ds(r0, c), :].astype(F32)
        ro_ref[pl.ds(r0, c), :] = (on * gate).astype(ro_ref.dtype)
        yield None
        rsf_scr[...] = sf * rt["cdec_f"] + kv_state(k, v, rt["kdec_f"])
        yield None

    def run(*stages, side=None, side_steps=4 / ATTN_GROUPS):
        last = [None] * len(stages)
        done = 0
        for t, vals in enumerate(zip(*stages)):
            last = list(vals)
            while side is not None and done < (t + 1) * side_steps:
                next(side)
                done += 1
        return last

    def step(j, m_a):
        store_block(jnp.maximum(2 * j - 1, 0), (acc_scr[0], acc_scr[1]))
        m_b, _ = run(scores(2 * j + 1, sb_scr), finish(2 * j, sa_scr, m_a), side=ret_chunk(2 * j))
        m_a, _ = run(scores(2 * j + 2, sa_scr), finish(2 * j + 1, sb_scr, m_b, defer=True),
                     side=ret_chunk(2 * j + 1))
        return m_a

    acc_scr[...] = jnp.ones(acc_scr.shape, F32)
    (m_a,) = run(scores(0, sa_scr), side=backward_states(), side_steps=nc / ATTN_GROUPS)
    m_a = lax.fori_loop(0, nq // 2 - 1, step, m_a)
    store_block(nq - 3, (acc_scr[0], acc_scr[1]))
    m_b, _ = run(scores(nq - 1, sb_scr), finish(nq - 2, sa_scr, m_a), side=ret_chunk(nq - 2))
    run(finish(nq - 1, sb_scr, m_b), side=ret_chunk(nq - 1))


def _token_mixers(lq1, lk1, lq2, lk2, subg, qd, kd, vt, kd_c, vt_c,
                  dec_f, dec_b, gn_g, qr, kr, vr, gr, kr_c, vr_c):
    b, n, _ = qd.shape
    nctx = kd_c.shape[1]
    assert (n // ATTN_QB) % 2 == 0 and n % (ATTN_GROUPS * MXU_DIM) == 0 and nctx % MXU_DIM == 0
    assert ATTN_GROUPS >= 4 and nctx == RET_C and RET_C == ATTN_QB
    vec = lambda w: pl.BlockSpec((1, w), lambda bi, h: (0, 0))
    head = lambda rows: pl.BlockSpec((None, rows, LANES), lambda bi, h: (bi, 0, h))
    pair = lambda rows: pl.BlockSpec((None, rows, LANES), lambda bi, h: (bi, 0, h // 2))
    headt = lambda cols: pl.BlockSpec((None, None, VT_ROWS, cols), lambda bi, h: (bi, h, 0, 0))
    out = jax.ShapeDtypeStruct((b, n, HEADS * V_DIM), BF16)
    return pl.pallas_call(
        _mixer_kernel,
        grid=(b, HEADS),
        in_specs=[vec(HEAD_DIM)] * 4 + [vec(V_DIM), head(n), head(n), headt(n), head(nctx), headt(nctx),
                  vec(HEADS), vec(HEADS), pl.BlockSpec((1, V_DIM), lambda bi, h: (0, h)),
                  pair(n), pair(n), head(n), head(n), pair(nctx), head(nctx)],
        out_specs=[head(n), head(n)],
        out_shape=[out, out],
        scratch_shapes=[pltpu.VMEM((2, n + nctx, ATTN_QB), F32)] * 2
        + [pltpu.VMEM((n // RET_C, LANES, V_DIM), BF16), pltpu.VMEM((LANES, V_DIM), F32),
           pltpu.VMEM((2, VT_ROWS, ATTN_QB), F32)],
        compiler_params=_cparams(("arbitrary", "arbitrary")),
        name="token_mixers",
    )(lq1, lk1, lq2, lk2, subg, qd, kd, vt, kd_c, vt_c, dec_f, dec_b, gn_g, qr, kr, vr, gr, kr_c, vr_c)


def _mlp_kernel(x_ref, od_ref, or_ref, mod_ref, n2_ref, fg_ref, wo_ref, w1_ref, w2_ref, o_ref):
    d = x_ref.shape[1]
    mod = mod_ref[pl.ds(pl.program_id(0), 1), :]
    g1, sh2, sc2, g2 = [mod[:, i * d:(i + 1) * d] for i in range(4)]
    half = od_ref.shape[1]
    y = (jnp.dot(od_ref[...], wo_ref[0:half, :].astype(BF16), preferred_element_type=F32)
         + jnp.dot(or_ref[...], wo_ref[half:2 * half, :].astype(BF16), preferred_element_type=F32))
    x1 = x_ref[...] + g1 * y
    h = _rmsnorm(x1, n2_ref[...]) * (1.0 + sc2) + sh2
    hb = h.astype(BF16)
    d_ff = w1_ref.shape[1]
    acc = None
    for f0 in range(0, d_ff, FF_CHUNK):
        u = jnp.dot(hb, w1_ref[:, f0:f0 + FF_CHUNK].astype(BF16), preferred_element_type=F32)
        u = jnp.square(jnp.maximum(u, 0.0)).astype(BF16)
        part = jnp.dot(u, w2_ref[f0:f0 + FF_CHUNK, :].astype(BF16), preferred_element_type=F32)
        acc = part if acc is None else acc + part
    x2 = x1 + g2 * acc
    o_ref[...] = _rmsnorm(x2, fg_ref[...])


def _out_mlp(x, od, orr, mod, n2, fg, wo, w1, w2):
    b, n, d = x.shape
    tm = MLP_TM
    tok = lambda w: pl.BlockSpec((None, tm, w), lambda bi, t: (bi, t, 0))
    gain = pl.BlockSpec((1, d), lambda bi, t: (0, 0))
    resident = lambda a: pl.BlockSpec(a.shape, lambda bi, t: (0, 0), pipeline_mode=pl.Buffered(1))
    return pl.pallas_call(
        _mlp_kernel,
        grid=(b, n // tm),
        in_specs=[tok(d), tok(od.shape[2]), tok(orr.shape[2]),
                  pl.BlockSpec(mod.shape, lambda bi, t: (0, 0)), gain, gain,
                  resident(wo), resident(w1), resident(w2)],
        out_specs=tok(d),
        out_shape=jax.ShapeDtypeStruct((b, n, d), F32),
        compiler_params=_cparams(("arbitrary", "arbitrary")),
        name="out_mlp",
    )(x, od, orr, mod, n2, fg, wo, w1, w2)


def kernel(x, c, ctx, c_ctx, w_ada, b_ada, norm1_g, norm2_g, w_in, lambda_q1, lambda_k1, lambda_q2,
           lambda_k2, diff_subln_g, ret_decay_fwd, ret_decay_bwd, ret_gn_g, w_out, w_mlp1, w_mlp2, final_g):
    assert w_ada.shape[0] == 1, "single-layer block"
    b, n, d = x.shape

    c_rows = jnp.concatenate([c, c_ctx[None], jnp.zeros((8 - b - 1, d), F32)], axis=0)
    qd, kd, vt, qr, kr, vr, gr, kd_c, vt_c, kr_c, vr_c, mod2 = _project(
        x, ctx, c_rows, w_ada[0], b_ada, norm1_g, w_in[0], _rope_tables(n))

    od, orr = _token_mixers(lambda_q1, lambda_k1, lambda_q2, lambda_k2, diff_subln_g, qd, kd, vt, kd_c, vt_c,
                            ret_decay_fwd, ret_decay_bwd, ret_gn_g, qr, kr, vr, gr, kr_c, vr_c)

    return _out_mlp(x, od, orr, mod2, norm2_g, final_g[None],
                    w_out[0], w_mlp1[0], w_mlp2[0])
```

```python
import math

import numpy as np
import jax
import jax.numpy as jnp
from jax import lax
from jax.experimental import pallas as pl
from jax.experimental.pallas import tpu as pltpu

F32 = jnp.float32
BF16 = jnp.bfloat16

GRID_W = 64
HEADS = 4
HEAD_DIM = 64
V_DIM = 128
VT_ROWS = V_DIM + 16
ROPE_PAIRS = 16
ROPE_BASE = 10000.0
N_MOD = 6
EPS = 1e-6
LAM_INIT = 0.8 - 0.6 * math.exp(-0.3 * 0)
LOG2E = 1.4426950408889634
QK_SCALE = HEAD_DIM ** -0.5

LANES = 128
MXU_DIM = 256
VMEM_LIMIT_BYTES = 56 * 1024 * 1024

PROJ_TM = 1024
ATTN_QB = 256
ATTN_GROUPS = 8
RET_C = 256
MLP_TM = 512
FF_CHUNK = 1024

W_DQK = HEADS * 2 * HEAD_DIM
W_RQK = HEADS * HEAD_DIM
W_V = HEADS * V_DIM


def _col_ranges(*widths):
    edges = [sum(widths[:i]) for i in range(len(widths) + 1)]
    return list(zip(edges[:-1], edges[1:]))


COL_QD, COL_KD, COL_VD, COL_QR, COL_KR, COL_VR, COL_GR = _col_ranges(W_DQK, W_DQK, W_V, W_RQK, W_RQK, W_V, W_V)


def _cparams(sem):
    return pltpu.CompilerParams(dimension_semantics=sem, vmem_limit_bytes=VMEM_LIMIT_BYTES)


def _rmsnorm(xf, g):
    return xf * lax.rsqrt(jnp.mean(xf * xf, axis=-1, keepdims=True) + EPS) * g


def _silu(x):
    return x * (1.0 / (1.0 + jnp.exp(-x)))


def _rope_tables(n_tokens):
    rows = n_tokens // GRID_W
    row = np.repeat(np.arange(rows), GRID_W).astype(np.float64)
    col = np.tile(np.arange(GRID_W), rows).astype(np.float64)
    inv = (np.float32(ROPE_BASE) ** (-np.arange(ROPE_PAIRS, dtype=np.float32) / ROPE_PAIRS)).astype(np.float64)
    ang_r = row[:, None] * inv
    ang_c = col[:, None] * inv
    zeros = np.zeros_like(ang_r)
    cos64 = np.concatenate([np.cos(ang_r)] * 2 + [np.cos(ang_c)] * 2, axis=1)
    sa64 = np.concatenate([-np.sin(ang_r), zeros, -np.sin(ang_c), zeros], axis=1)
    sb64 = np.concatenate([zeros, np.sin(ang_r), zeros, np.sin(ang_c)], axis=1)
    dup = lambda t: jnp.asarray(np.concatenate([t, t], axis=1), dtype=F32)
    return dup(cos64), dup(sa64), dup(sb64)


def _modulated(x_ref, shift, scale, g_ref):
    h = _rmsnorm(x_ref[...], g_ref[...])
    return (h * (1.0 + scale) + shift).astype(BF16)


def _proj_kernel(x_ref, g_ref, w_ref, c_ref, sa_ref, sb_ref, xc_ref, cond_ref, wa1_ref, ba1_ref, wa2_ref, ba2_ref,
                 qd_ref, kd_ref, vt_ref, qr_ref, kr_ref, vr_ref, gr_ref,
                 kdc_ref, vtc_ref, krc_ref, vrc_ref, mod2_ref, mod1_scr):
    d = x_ref.shape[1]
    sample = pl.program_id(0)
    n_samples = pl.num_programs(0)

    cond = _silu(cond_ref[...]).astype(BF16)

    @pl.when((sample == 0) & (pl.program_id(1) == 0))
    def _():
        mod1_scr[...] = jnp.dot(cond, wa1_ref[...].astype(BF16), preferred_element_type=F32) + ba1_ref[...]

    mod2_ref[...] = jnp.dot(cond, wa2_ref[...].astype(BF16), preferred_element_type=F32) + ba2_ref[...]

    def project(hb, cols):
        return jnp.dot(hb, w_ref[:, cols[0]:cols[1]].astype(BF16), preferred_element_type=F32)

    def put_vt(ref, vd):
        ones = jnp.ones((VT_ROWS - V_DIM, vd.shape[0]), BF16)
        for h in range(HEADS):
            ref[h, 0:V_DIM, :] = vd[:, h * V_DIM:(h + 1) * V_DIM].T.astype(BF16)
            ref[h, V_DIM:VT_ROWS, :] = ones

    hb = _modulated(x_ref, mod1_scr[pl.ds(sample, 1), 0:d], mod1_scr[pl.ds(sample, 1), d:2 * d], g_ref)
    c, sa, sb = c_ref[...], sa_ref[...], sb_ref[...]

    def rotate(y):
        slabs = []
        for s in range(y.shape[1] // LANES):
            ys = y[:, s * LANES:(s + 1) * LANES]
            slabs.append(ys * c + pltpu.roll(ys, LANES - ROPE_PAIRS, 1) * sa
                         + pltpu.roll(ys, ROPE_PAIRS, 1) * sb)
        return jnp.concatenate(slabs, axis=1)

    qd_ref[...] = (rotate(project(hb, COL_QD)) * (QK_SCALE * LOG2E)).astype(BF16)
    kd_ref[...] = rotate(project(hb, COL_KD)).astype(BF16)
    put_vt(vt_ref, project(hb, COL_VD))
    gr_ref[...] = _silu(project(hb, COL_GR)).astype(BF16)
    qr_ref[...] = rotate(project(hb, COL_QR)).astype(BF16)
    kr_ref[...] = (rotate(project(hb, COL_KR)) * QK_SCALE).astype(BF16)
    vr_ref[...] = project(hb, COL_VR).astype(BF16)

    @pl.when(pl.program_id(1) == 0)
    def _():
        hc = _modulated(xc_ref, mod1_scr[pl.ds(n_samples, 1), 0:d], mod1_scr[pl.ds(n_samples, 1), d:2 * d], g_ref)
        kdc_ref[...] = project(hc, COL_KD).astype(BF16)
        put_vt(vtc_ref, project(hc, COL_VD))
        krc_ref[...] = (project(hc, COL_KR) * QK_SCALE).astype(BF16)
        vrc_ref[...] = project(hc, COL_VR).astype(BF16)


def _project(x, ctx, cond_rows, w_ada, b_ada, g, w_in, tables):
    b, n, d = x.shape
    nctx = ctx.shape[1]
    tm = PROJ_TM
    tok = lambda w: pl.BlockSpec((None, tm, w), lambda bi, t: (bi, t, 0))
    ctok = lambda w: pl.BlockSpec((None, nctx, w), lambda bi, t: (bi, 0, 0))
    nt = n // tm
    wblk = (N_MOD - 2) * d // (b * nt)
    assert wblk % LANES == 0 and (2 * d) % wblk == 0
    shared = pl.BlockSpec((1, d), lambda bi, t: (0, 0))
    later = lambda rows: pl.BlockSpec((rows, wblk), lambda bi, t: (0, 2 * d // wblk + bi * nt + t))
    table = pl.BlockSpec((tm, LANES), lambda bi, t: (t, 0))
    out_specs = [tok(W_DQK), tok(W_DQK),
                 pl.BlockSpec((None, HEADS, VT_ROWS, tm), lambda bi, t: (bi, 0, 0, t)),
                 tok(W_RQK), tok(W_RQK), tok(W_V), tok(W_V),
                 ctok(W_DQK), pl.BlockSpec((None, HEADS, VT_ROWS, nctx), lambda bi, t: (bi, 0, 0, 0)),
                 ctok(W_RQK), ctok(W_V),
                 pl.BlockSpec((cond_rows.shape[0], wblk), lambda bi, t: (0, bi * nt + t))]
    out_shape = [jax.ShapeDtypeStruct((b, n, W_DQK), BF16),
                 jax.ShapeDtypeStruct((b, n, W_DQK), BF16),
                 jax.ShapeDtypeStruct((b, HEADS, VT_ROWS, n), BF16),
                 jax.ShapeDtypeStruct((b, n, W_RQK), BF16),
                 jax.ShapeDtypeStruct((b, n, W_RQK), BF16),
                 jax.ShapeDtypeStruct((b, n, W_V), BF16),
                 jax.ShapeDtypeStruct((b, n, W_V), BF16),
                 jax.ShapeDtypeStruct((b, nctx, W_DQK), BF16),
                 jax.ShapeDtypeStruct((b, HEADS, VT_ROWS, nctx), BF16),
                 jax.ShapeDtypeStruct((b, nctx, W_RQK), BF16),
                 jax.ShapeDtypeStruct((b, nctx, W_V), BF16),
                 jax.ShapeDtypeStruct((cond_rows.shape[0], (N_MOD - 2) * d), F32)]
    return pl.pallas_call(
        _proj_kernel,
        grid=(b, n // tm),
        in_specs=[tok(d), shared,
                  pl.BlockSpec(w_in.shape, lambda bi, t: (0, 0), pipeline_mode=pl.Buffered(1)),
                  table, table, table, ctok(d),
                  pl.BlockSpec(cond_rows.shape, lambda bi, t: (0, 0)),
                  pl.BlockSpec((d, 2 * d), lambda bi, t: (0, 0), pipeline_mode=pl.Buffered(1)),
                  pl.BlockSpec((1, 2 * d), lambda bi, t: (0, 0)),
                  later(d), later(1)],
        out_specs=out_specs,
        out_shape=out_shape,
        scratch_shapes=[pltpu.VMEM((cond_rows.shape[0], 2 * d), F32)],
        compiler_params=_cparams(("arbitrary", "arbitrary")),
        name="proj",
    )(x, g, w_in, *tables, ctx, cond_rows, w_ada, b_ada, w_ada, b_ada)


def _mixer_kernel(lq1_ref, lk1_ref, lq2_ref, lk2_ref, subg_ref, q_ref, k_ref, vt_ref, kc_ref, vtc_ref,
                  decf_ref, decb_ref, gn_ref, rq_ref, rk_ref, rv_ref, rg_ref, rkc_ref, rvc_ref,
                  o_ref, ro_ref, sa_scr, sb_scr, rsb_scr, rsf_scr, acc_scr):
    n = q_ref.shape[0]
    head = pl.program_id(1)
    lane = lax.broadcasted_iota(jnp.int32, (1, LANES), 1)

    lam = (jnp.exp(jnp.sum(lq1_ref[...] * lk1_ref[...], axis=-1, keepdims=True))
           - jnp.exp(jnp.sum(lq2_ref[...] * lk2_ref[...], axis=-1, keepdims=True)) + LAM_INIT)
    first_map = lane < HEAD_DIM
    subg = subg_ref[...]
    nq = n // ATTN_QB

    nctx = kc_ref.shape[0]
    gsz = n // ATTN_GROUPS
    groups = [[(k_ref, vt_ref, g * gsz, (g + 1) * gsz, nctx + g * gsz)] for g in range(ATTN_GROUPS)]
    groups[0].insert(0, (kc_ref, vtc_ref, 0, nctx, 0))

    def scores(i, s_scr):
        r0 = pl.multiple_of(i * ATTN_QB, ATTN_QB)
        q = q_ref[pl.ds(r0, ATTN_QB), :]
        zero = jnp.zeros_like(q)
        qms = (jnp.where(first_map, q, zero), jnp.where(first_map, zero, q))
        ms = [None, None]
        for pieces in groups:
            for kr, _, lo, hi, dst in pieces:
                for mp in range(2):
                    s = lax.dot_general(kr[lo:hi, :], qms[mp], (((1,), (1,)), ((), ())),
                                        preferred_element_type=F32)
                    s_scr[mp, dst:dst + hi - lo, :] = s
                    mg = jnp.max(s, axis=0, keepdims=True)
                    ms[mp] = mg if ms[mp] is None else jnp.maximum(ms[mp], mg)
            yield tuple(ms)

    def store_block(i, accs):
        ots = [a[0:V_DIM, :] / a[V_DIM:V_DIM + 1, :] for a in accs]
        ot = ots[0] - lam * ots[1]
        ot = ot * lax.rsqrt(jnp.mean(ot * ot, axis=0, keepdims=True) + EPS)
        o = ot.T * (subg * (1.0 - LAM_INIT))
        r0 = pl.multiple_of(i * ATTN_QB, ATTN_QB)
        o_ref[pl.ds(r0, ATTN_QB), :] = o.astype(o_ref.dtype)

    def finish(i, s_scr, ms, defer=False):
        accs = [None, None]
        for g, pieces in enumerate(groups):
            for _, vr, lo, hi, dst in pieces:
                for mp in range(2):
                    p = jnp.exp2(s_scr[mp, dst:dst + hi - lo, :] - ms[mp]).astype(BF16)
                    part = jnp.dot(vr[:, lo:hi], p, preferred_element_type=F32)
                    accs[mp] = part if accs[mp] is None else accs[mp] + part
            if g == len(groups) - 1:
                if defer:
                    acc_scr[0], acc_scr[1] = accs
                else:
                    store_block(i, accs)
            yield None

    c = RET_C
    nc = n // c

    def log_gamma(dec):
        z = -dec
        return -(jnp.maximum(z, 0.0) + jnp.log(1.0 + jnp.exp(-jnp.abs(z))))

    hsel = lax.broadcasted_iota(jnp.int32, (1, HEADS), 1) == head

    def pick(vec):
        return jnp.sum(jnp.where(hsel, vec, 0.0), axis=-1, keepdims=True)

    rmask = (lane >= HEAD_DIM).astype(jnp.int32) == head % 2
    gn = gn_ref[...]
    rt = {}

    def kv_state(k, v, kdec):
        kd = (k.astype(F32) * kdec).astype(BF16)
        return lax.dot_general(kd, v, (((0,), (0,)), ((), ())), preferred_element_type=F32)

    def backward_states():
        lgf, lgb = pick(log_gamma(decf_ref[...])), pick(log_gamma(decb_ref[...]))
        ii = lax.broadcasted_iota(jnp.int32, (c, c), 0).astype(F32)
        jj = lax.broadcasted_iota(jnp.int32, (c, c), 1).astype(F32)
        rel = ii - jj
        pos = lax.broadcasted_iota(jnp.int32, (c, LANES), 0).astype(F32)
        rt["decay"] = (jnp.where(rel >= 0, jnp.exp(jnp.maximum(rel, 0.0) * lgf), 0.0)
                       + jnp.where(rel <= 0, jnp.exp(jnp.maximum(-rel, 0.0) * lgb), 0.0))
        rt["qdec_f"], rt["qdec_b"] = jnp.exp((pos + 1.0) * lgf), jnp.exp((c - pos) * lgb)
        rt["kdec_f"], kdec_b = jnp.exp((c - 1.0 - pos) * lgf), jnp.exp(pos * lgb)
        rt["cdec_f"], cdec_b = jnp.exp(c * lgf), jnp.exp(c * lgb)
        rsf_scr[...] = kv_state(rkc_ref[...], rvc_ref[...], rt["kdec_f"])
        sb = kv_state(rkc_ref[...], rvc_ref[...], kdec_b)
        for ci in reversed(range(nc)):
            rsb_scr[ci] = sb.astype(BF16)
            if ci > 0:
                sb = sb * cdec_b + kv_state(rk_ref[ci * c:(ci + 1) * c, :], rv_ref[ci * c:(ci + 1) * c, :], kdec_b)
            yield None

    def ret_chunk(ci):
        r0 = pl.multiple_of(ci * c, c)
        q = rq_ref[pl.ds(r0, c), :]
        k = rk_ref[pl.ds(r0, c), :]
        v = rv_ref[pl.ds(r0, c), :]
        qm = jnp.where(rmask, q, jnp.zeros_like(q))
        sc = lax.dot_general(qm, k, (((1,), (1,)), ((), ())), preferred_element_type=F32)
        a = (sc * rt["decay"]).astype(BF16)
        yield None
        sf = rsf_scr[...]
        o = jnp.dot(a, v, preferred_element_type=F32)
        o = o + jnp.dot(qm, sf.astype(BF16), preferred_element_type=F32) * rt["qdec_f"]
        o = o + jnp.dot(qm, rsb_scr[ci], preferred_element_type=F32) * rt["qdec_b"]
        yield None
        mu = jnp.mean(o, axis=-1, keepdims=True)
        var = jnp.mean(jnp.square(o - mu), axis=-1, keepdims=True)
        on = (o - mu) * lax.rsqrt(var + EPS) * gn
        gate = rg_ref[pl.ds(r0, c), :].astype(F32)
        ro_ref[pl.ds(r0, c), :] = (on * gate).astype(ro_ref.dtype)
        yield None
        rsf_scr[...] = sf * rt["cdec_f"] + kv_state(k, v, rt["kdec_f"])
        yield None

    def run(*stages, side=None, side_steps=4 / ATTN_GROUPS):
        last = [None] * len(stages)
        done = 0
        for t, vals in enumerate(zip(*stages)):
            last = list(vals)
            while side is not None and done < (t + 1) * side_steps:
                next(side)
                done += 1
        return last

    def step(j, m_a):
        store_block(jnp.maximum(2 * j - 1, 0), (acc_scr[0], acc_scr[1]))
        m_b, _ = run(scores(2 * j + 1, sb_scr), finish(2 * j, sa_scr, m_a), side=ret_chunk(2 * j))
        m_a, _ = run(scores(2 * j + 2, sa_scr), finish(2 * j + 1, sb_scr, m_b, defer=True),
                     side=ret_chunk(2 * j + 1))
        return m_a

    acc_scr[...] = jnp.ones(acc_scr.shape, F32)
    (m_a,) = run(scores(0, sa_scr), side=backward_states(), side_steps=nc / ATTN_GROUPS)
    m_a = lax.fori_loop(0, nq // 2 - 1, step, m_a)
    store_block(nq - 3, (acc_scr[0], acc_scr[1]))
    m_b, _ = run(scores(nq - 1, sb_scr), finish(nq - 2, sa_scr, m_a), side=ret_chunk(nq - 2))
    run(finish(nq - 1, sb_scr, m_b), side=ret_chunk(nq - 1))


def _token_mixers(lq1, lk1, lq2, lk2, subg, qd, kd, vt, kd_c, vt_c,
                  dec_f, dec_b, gn_g, qr, kr, vr, gr, kr_c, vr_c):
    b, n, _ = qd.shape
    nctx = kd_c.shape[1]
    assert (n // ATTN_QB) % 2 == 0 and n % (ATTN_GROUPS * MXU_DIM) == 0 and nctx % MXU_DIM == 0
    assert ATTN_GROUPS >= 4 and nctx == RET_C and RET_C == ATTN_QB
    vec = lambda w: pl.BlockSpec((1, w), lambda bi, h: (0, 0))
    head = lambda rows: pl.BlockSpec((None, rows, LANES), lambda bi, h: (bi, 0, h))
    pair = lambda rows: pl.BlockSpec((None, rows, LANES), lambda bi, h: (bi, 0, h // 2))
    headt = lambda cols: pl.BlockSpec((None, None, VT_ROWS, cols), lambda bi, h: (bi, h, 0, 0))
    out = jax.ShapeDtypeStruct((b, n, HEADS * V_DIM), BF16)
    return pl.pallas_call(
        _mixer_kernel,
        grid=(b, HEADS),
        in_specs=[vec(HEAD_DIM)] * 4 + [vec(V_DIM), head(n), head(n), headt(n), head(nctx), headt(nctx),
                  vec(HEADS), vec(HEADS), pl.BlockSpec((1, V_DIM), lambda bi, h: (0, h)),
                  pair(n), pair(n), head(n), head(n), pair(nctx), head(nctx)],
        out_specs=[head(n), head(n)],
        out_shape=[out, out],
        scratch_shapes=[pltpu.VMEM((2, n + nctx, ATTN_QB), F32)] * 2
        + [pltpu.VMEM((n // RET_C, LANES, V_DIM), BF16), pltpu.VMEM((LANES, V_DIM), F32),
           pltpu.VMEM((2, VT_ROWS, ATTN_QB), F32)],
        compiler_params=_cparams(("arbitrary", "arbitrary")),
        name="token_mixers",
    )(lq1, lk1, lq2, lk2, subg, qd, kd, vt, kd_c, vt_c, dec_f, dec_b, gn_g, qr, kr, vr, gr, kr_c, vr_c)


def _mlp_kernel(x_ref, od_ref, or_ref, mod_ref, n2_ref, fg_ref, wo_ref, w1_ref, w2_ref, o_ref):
    d = x_ref.shape[1]
    mod = mod_ref[pl.ds(pl.program_id(0), 1), :]
    g1, sh2, sc2, g2 = [mod[:, i * d:(i + 1) * d] for i in range(4)]
    half = od_ref.shape[1]
    y = (jnp.dot(od_ref[...], wo_ref[0:half, :].astype(BF16), preferred_element_type=F32)
         + jnp.dot(or_ref[...], wo_ref[half:2 * half, :].astype(BF16), preferred_element_type=F32))
    x1 = x_ref[...] + g1 * y
    h = _rmsnorm(x1, n2_ref[...]) * (1.0 + sc2) + sh2
    hb = h.astype(BF16)
    d_ff = w1_ref.shape[1]
    tm = x_ref.shape[0]
    acc = None
    for f0 in range(0, d_ff, FF_CHUNK):
        u = jnp.dot(hb, w1_ref[:, f0:f0 + FF_CHUNK].astype(BF16), preferred_element_type=F32)
        u = jnp.square(jnp.maximum(u, 0.0)).astype(BF16)
        w2c = w2_ref[f0:f0 + FF_CHUNK, :].astype(BF16)
        if f0 + FF_CHUNK < d_ff:
            part = jnp.dot(u, w2c, preferred_element_type=F32)
            acc = part if acc is None else acc + part
        else:
            for lo in range(0, tm, tm // 2):
                rows = slice(lo, lo + tm // 2)
                x2 = x1[rows] + g2 * (acc[rows] + jnp.dot(u[rows], w2c, preferred_element_type=F32))
                o_ref[rows, :] = _rmsnorm(x2, fg_ref[...])


def _out_mlp(x, od, orr, mod, n2, fg, wo, w1, w2):
    b, n, d = x.shape
    tm = MLP_TM
    tok = lambda w: pl.BlockSpec((None, tm, w), lambda bi, t: (bi, t, 0))
    gain = pl.BlockSpec((1, d), lambda bi, t: (0, 0))
    resident = lambda a: pl.BlockSpec(a.shape, lambda bi, t: (0, 0), pipeline_mode=pl.Buffered(1))
    return pl.pallas_call(
        _mlp_kernel,
        grid=(b, n // tm),
        in_specs=[tok(d), tok(od.shape[2]), tok(orr.shape[2]),
                  pl.BlockSpec(mod.shape, lambda bi, t: (0, 0)), gain, gain,
                  resident(wo), resident(w1), resident(w2)],
        out_specs=tok(d),
        out_shape=jax.ShapeDtypeStruct((b, n, d), F32),
        compiler_params=_cparams(("arbitrary", "arbitrary")),
        name="out_mlp",
    )(x, od, orr, mod, n2, fg, wo, w1, w2)


def kernel(x, c, ctx, c_ctx, w_ada, b_ada, norm1_g, norm2_g, w_in, lambda_q1, lambda_k1, lambda_q2,
           lambda_k2, diff_subln_g, ret_decay_fwd, ret_decay_bwd, ret_gn_g, w_out, w_mlp1, w_mlp2, final_g):
    assert w_ada.shape[0] == 1, "single-layer block"
    b, n, d = x.shape

    c_rows = jnp.concatenate([c, c_ctx[None], jnp.zeros((8 - b - 1, d), F32)], axis=0)
    qd, kd, vt, qr, kr, vr, gr, kd_c, vt_c, kr_c, vr_c, mod2 = _project(
        x, ctx, c_rows, w_ada[0], b_ada, norm1_g, w_in[0], _rope_tables(n))

    od, orr = _token_mixers(lambda_q1, lambda_k1, lambda_q2, lambda_k2, diff_subln_g, qd, kd, vt, kd_c, vt_c,
                            ret_decay_fwd, ret_decay_bwd, ret_gn_g, qr, kr, vr, gr, kr_c, vr_c)

    return _out_mlp(x, od, orr, mod2, norm2_g, final_g[None],
                    w_out[0], w_mlp1[0], w_mlp2[0])
```

```python
import math

import numpy as np
import jax
import jax.numpy as jnp
from jax import lax
from jax.experimental import pallas as pl
from jax.experimental.pallas import tpu as pltpu

F32 = jnp.float32
BF16 = jnp.bfloat16

GRID_W = 64
HEADS = 4
HEAD_DIM = 64
V_DIM = 128
VT_ROWS = V_DIM + 16
ROPE_PAIRS = 16
ROPE_BASE = 10000.0
N_MOD = 6
MOD_ROWS = 8
EPS = 1e-6
LAM_INIT = 0.8 - 0.6 * math.exp(-0.3 * 0)
LOG2E = 1.4426950408889634
QK_SCALE = HEAD_DIM ** -0.5

LANES = 128
MXU_DIM = 256
VMEM_LIMIT_BYTES = 56 * 1024 * 1024

PROJ_TM = 1024
ATTN_QB = 256
ATTN_GROUPS = 8
RET_C = 256
MLP_TM = 512
FF_CHUNK = 1024

W_DQK = HEADS * 2 * HEAD_DIM
W_RQK = HEADS * HEAD_DIM
W_V = HEADS * V_DIM


def _col_ranges(*widths):
    edges = [sum(widths[:i]) for i in range(len(widths) + 1)]
    return list(zip(edges[:-1], edges[1:]))


COL_QD, COL_KD, COL_VD, COL_QR, COL_KR, COL_VR, COL_GR = _col_ranges(W_DQK, W_DQK, W_V, W_RQK, W_RQK, W_V, W_V)


def _cparams(sem):
    return pltpu.CompilerParams(dimension_semantics=sem, vmem_limit_bytes=VMEM_LIMIT_BYTES)


def _rmsnorm(xf, g):
    return xf * lax.rsqrt(jnp.mean(xf * xf, axis=-1, keepdims=True) + EPS) * g


def _silu(x):
    return x * (1.0 / (1.0 + jnp.exp(-x)))


def _rope_tables(n_tokens):
    rows = n_tokens // GRID_W
    row = np.repeat(np.arange(rows), GRID_W).astype(np.float64)
    col = np.tile(np.arange(GRID_W), rows).astype(np.float64)
    inv = (np.float32(ROPE_BASE) ** (-np.arange(ROPE_PAIRS, dtype=np.float32) / ROPE_PAIRS)).astype(np.float64)
    ang_r = row[:, None] * inv
    ang_c = col[:, None] * inv
    zeros = np.zeros_like(ang_r)
    cos64 = np.concatenate([np.cos(ang_r)] * 2 + [np.cos(ang_c)] * 2, axis=1)
    sa64 = np.concatenate([-np.sin(ang_r), zeros, -np.sin(ang_c), zeros], axis=1)
    sb64 = np.concatenate([zeros, np.sin(ang_r), zeros, np.sin(ang_c)], axis=1)
    dup = lambda t: jnp.asarray(np.concatenate([t, t], axis=1), dtype=F32)
    return dup(cos64), dup(sa64), dup(sb64)


def _modulated(x_ref, shift, scale, g_ref):
    h = _rmsnorm(x_ref[...], g_ref[...])
    return (h * (1.0 + scale) + shift).astype(BF16)


def _proj_kernel(x_ref, g_ref, w_ref, c_ref, sa_ref, sb_ref, xc_ref, cond_ref, cctx_ref, wa1_ref, ba1_ref, wa2_ref, ba2_ref,
                 qd_ref, kd_ref, vt_ref, qr_ref, kr_ref, vr_ref, gr_ref,
                 kdc_ref, vtc_ref, krc_ref, vrc_ref, mod2_ref, mod1_scr):
    d = x_ref.shape[1]
    sample = pl.program_id(0)
    n_samples = pl.num_programs(0)

    rows = mod1_scr.shape[0]
    cond = jnp.concatenate([cond_ref[...], cctx_ref[...],
                            jnp.zeros((rows - cond_ref.shape[0] - 1, d), F32)], axis=0)
    cond = _silu(cond).astype(BF16)

    @pl.when((sample == 0) & (pl.program_id(1) == 0))
    def _():
        mod1_scr[...] = jnp.dot(cond, wa1_ref[...].astype(BF16), preferred_element_type=F32) + ba1_ref[...]

    mod2_ref[...] = jnp.dot(cond, wa2_ref[...].astype(BF16), preferred_element_type=F32) + ba2_ref[...]

    def project(hb, cols):
        return jnp.dot(hb, w_ref[:, cols[0]:cols[1]].astype(BF16), preferred_element_type=F32)

    def put_vt(ref, vd):
        ones = jnp.ones((VT_ROWS - V_DIM, vd.shape[0]), BF16)
        for h in range(HEADS):
            ref[h, 0:V_DIM, :] = vd[:, h * V_DIM:(h + 1) * V_DIM].T.astype(BF16)
            ref[h, V_DIM:VT_ROWS, :] = ones

    hb = _modulated(x_ref, mod1_scr[pl.ds(sample, 1), 0:d], mod1_scr[pl.ds(sample, 1), d:2 * d], g_ref)
    c, sa, sb = c_ref[...], sa_ref[...], sb_ref[...]

    def rotate(y):
        slabs = []
        for s in range(y.shape[1] // LANES):
            ys = y[:, s * LANES:(s + 1) * LANES]
            slabs.append(ys * c + pltpu.roll(ys, LANES - ROPE_PAIRS, 1) * sa
                         + pltpu.roll(ys, ROPE_PAIRS, 1) * sb)
        return jnp.concatenate(slabs, axis=1)

    qd_ref[...] = (rotate(project(hb, COL_QD)) * (QK_SCALE * LOG2E)).astype(BF16)
    kd_ref[...] = rotate(project(hb, COL_KD)).astype(BF16)
    put_vt(vt_ref, project(hb, COL_VD))
    gr_ref[...] = _silu(project(hb, COL_GR)).astype(BF16)
    qr_ref[...] = rotate(project(hb, COL_QR)).astype(BF16)
    kr_ref[...] = (rotate(project(hb, COL_KR)) * QK_SCALE).astype(BF16)
    vr_ref[...] = project(hb, COL_VR).astype(BF16)

    @pl.when(pl.program_id(1) == 0)
    def _():
        hc = _modulated(xc_ref, mod1_scr[pl.ds(n_samples, 1), 0:d], mod1_scr[pl.ds(n_samples, 1), d:2 * d], g_ref)
        kdc_ref[...] = project(hc, COL_KD).astype(BF16)
        put_vt(vtc_ref, project(hc, COL_VD))
        krc_ref[...] = (project(hc, COL_KR) * QK_SCALE).astype(BF16)
        vrc_ref[...] = project(hc, COL_VR).astype(BF16)


def _project(x, ctx, c, c_ctx, w_ada, b_ada, g, w_in, tables):
    b, n, d = x.shape
    nctx = ctx.shape[1]
    tm = PROJ_TM
    tok = lambda w: pl.BlockSpec((None, tm, w), lambda bi, t: (bi, t, 0))
    ctok = lambda w: pl.BlockSpec((None, nctx, w), lambda bi, t: (bi, 0, 0))
    nt = n // tm
    wblk = (N_MOD - 2) * d // (b * nt)
    assert wblk % LANES == 0 and (2 * d) % wblk == 0
    shared = pl.BlockSpec((1, d), lambda bi, t: (0, 0))
    later = lambda rows: pl.BlockSpec((rows, wblk), lambda bi, t: (0, 2 * d // wblk + bi * nt + t))
    table = pl.BlockSpec((tm, LANES), lambda bi, t: (t, 0))
    out_specs = [tok(W_DQK), tok(W_DQK),
                 pl.BlockSpec((None, HEADS, VT_ROWS, tm), lambda bi, t: (bi, 0, 0, t)),
                 tok(W_RQK), tok(W_RQK), tok(W_V), tok(W_V),
                 ctok(W_DQK), pl.BlockSpec((None, HEADS, VT_ROWS, nctx), lambda bi, t: (bi, 0, 0, 0)),
                 ctok(W_RQK), ctok(W_V),
                 pl.BlockSpec((MOD_ROWS, wblk), lambda bi, t: (0, bi * nt + t))]
    out_shape = [jax.ShapeDtypeStruct((b, n, W_DQK), BF16),
                 jax.ShapeDtypeStruct((b, n, W_DQK), BF16),
                 jax.ShapeDtypeStruct((b, HEADS, VT_ROWS, n), BF16),
                 jax.ShapeDtypeStruct((b, n, W_RQK), BF16),
                 jax.ShapeDtypeStruct((b, n, W_RQK), BF16),
                 jax.ShapeDtypeStruct((b, n, W_V), BF16),
                 jax.ShapeDtypeStruct((b, n, W_V), BF16),
                 jax.ShapeDtypeStruct((b, nctx, W_DQK), BF16),
                 jax.ShapeDtypeStruct((b, HEADS, VT_ROWS, nctx), BF16),
                 jax.ShapeDtypeStruct((b, nctx, W_RQK), BF16),
                 jax.ShapeDtypeStruct((b, nctx, W_V), BF16),
                 jax.ShapeDtypeStruct((MOD_ROWS, (N_MOD - 2) * d), F32)]
    return pl.pallas_call(
        _proj_kernel,
        grid=(b, n // tm),
        in_specs=[tok(d), shared,
                  pl.BlockSpec(w_in.shape, lambda bi, t: (0, 0), pipeline_mode=pl.Buffered(1)),
                  table, table, table, ctok(d),
                  pl.BlockSpec(c.shape, lambda bi, t: (0, 0)), shared,
                  pl.BlockSpec((d, 2 * d), lambda bi, t: (0, 0), pipeline_mode=pl.Buffered(1)),
                  pl.BlockSpec((1, 2 * d), lambda bi, t: (0, 0)),
                  later(d), later(1)],
        out_specs=out_specs,
        out_shape=out_shape,
        scratch_shapes=[pltpu.VMEM((MOD_ROWS, 2 * d), F32)],
        compiler_params=_cparams(("arbitrary", "arbitrary")),
        name="proj",
    )(x, g, w_in, *tables, ctx, c, c_ctx[None], w_ada, b_ada, w_ada, b_ada)


def _mixer_kernel(lq1_ref, lk1_ref, lq2_ref, lk2_ref, subg_ref, q_ref, k_ref, vt_ref, kc_ref, vtc_ref,
                  decf_ref, decb_ref, gn_ref, rq_ref, rk_ref, rv_ref, rg_ref, rkc_ref, rvc_ref,
                  o_ref, ro_ref, sa_scr, sb_scr, rsb_scr, rsf_scr, acc_scr):
    n = q_ref.shape[0]
    head = pl.program_id(1)
    lane = lax.broadcasted_iota(jnp.int32, (1, LANES), 1)

    lam = (jnp.exp(jnp.sum(lq1_ref[...] * lk1_ref[...], axis=-1, keepdims=True))
           - jnp.exp(jnp.sum(lq2_ref[...] * lk2_ref[...], axis=-1, keepdims=True)) + LAM_INIT)
    first_map = lane < HEAD_DIM
    subg = subg_ref[...]
    nq = n // ATTN_QB

    nctx = kc_ref.shape[0]
    gsz = n // ATTN_GROUPS
    groups = [[(k_ref, vt_ref, g * gsz, (g + 1) * gsz, nctx + g * gsz)] for g in range(ATTN_GROUPS)]
    groups[0].insert(0, (kc_ref, vtc_ref, 0, nctx, 0))

    def scores(i, s_scr):
        r0 = pl.multiple_of(i * ATTN_QB, ATTN_QB)
        q = q_ref[pl.ds(r0, ATTN_QB), :]
        zero = jnp.zeros_like(q)
        qms = (jnp.where(first_map, q, zero), jnp.where(first_map, zero, q))
        ms = [None, None]
        for pieces in groups:
            for kr, _, lo, hi, dst in pieces:
                for mp in range(2):
                    s = lax.dot_general(kr[lo:hi, :], qms[mp], (((1,), (1,)), ((), ())),
                                        preferred_element_type=F32)
                    s_scr[mp, dst:dst + hi - lo, :] = s
                    mg = jnp.max(s, axis=0, keepdims=True)
                    ms[mp] = mg if ms[mp] is None else jnp.maximum(ms[mp], mg)
            yield tuple(ms)

    def store_block(i, accs):
        ots = [a[0:V_DIM, :] / a[V_DIM:V_DIM + 1, :] for a in accs]
        ot = ots[0] - lam * ots[1]
        ot = ot * lax.rsqrt(jnp.mean(ot * ot, axis=0, keepdims=True) + EPS)
        o = ot.T * (subg * (1.0 - LAM_INIT))
        r0 = pl.multiple_of(i * ATTN_QB, ATTN_QB)
        o_ref[pl.ds(r0, ATTN_QB), :] = o.astype(o_ref.dtype)

    def finish(i, s_scr, ms, defer=False):
        accs = [None, None]
        for g, pieces in enumerate(groups):
            for _, vr, lo, hi, dst in pieces:
                for mp in range(2):
                    p = jnp.exp2(s_scr[mp, dst:dst + hi - lo, :] - ms[mp]).astype(BF16)
                    part = jnp.dot(vr[:, lo:hi], p, preferred_element_type=F32)
                    accs[mp] = part if accs[mp] is None else accs[mp] + part
            if g == len(groups) - 1:
                if defer:
                    acc_scr[0], acc_scr[1] = accs
                else:
                    store_block(i, accs)
            yield None

    c = RET_C
    nc = n // c

    def log_gamma(dec):
        z = -dec
        return -(jnp.maximum(z, 0.0) + jnp.log(1.0 + jnp.exp(-jnp.abs(z))))

    hsel = lax.broadcasted_iota(jnp.int32, (1, HEADS), 1) == head

    def pick(vec):
        return jnp.sum(jnp.where(hsel, vec, 0.0), axis=-1, keepdims=True)

    rmask = (lane >= HEAD_DIM).astype(jnp.int32) == head % 2
    gn = gn_ref[...]
    rt = {}

    def kv_state(k, v, kdec):
        kd = (k.astype(F32) * kdec).astype(BF16)
        return lax.dot_general(kd, v, (((0,), (0,)), ((), ())), preferred_element_type=F32)

    def backward_states():
        lgf, lgb = pick(log_gamma(decf_ref[...])), pick(log_gamma(decb_ref[...]))
        ii = lax.broadcasted_iota(jnp.int32, (c, c), 0).astype(F32)
        jj = lax.broadcasted_iota(jnp.int32, (c, c), 1).astype(F32)
        rel = ii - jj
        pos = lax.broadcasted_iota(jnp.int32, (c, LANES), 0).astype(F32)
        rt["decay"] = (jnp.where(rel >= 0, jnp.exp(jnp.maximum(rel, 0.0) * lgf), 0.0)
                       + jnp.where(rel <= 0, jnp.exp(jnp.maximum(-rel, 0.0) * lgb), 0.0))
        rt["qdec_f"], rt["qdec_b"] = jnp.exp((pos + 1.0) * lgf), jnp.exp((c - pos) * lgb)
        rt["kdec_f"], kdec_b = jnp.exp((c - 1.0 - pos) * lgf), jnp.exp(pos * lgb)
        rt["cdec_f"], cdec_b = jnp.exp(c * lgf), jnp.exp(c * lgb)
        rsf_scr[...] = kv_state(rkc_ref[...], rvc_ref[...], rt["kdec_f"])
        sb = kv_state(rkc_ref[...], rvc_ref[...], kdec_b)
        for ci in reversed(range(nc)):
            rsb_scr[ci] = sb.astype(BF16)
            if ci > 0:
                sb = sb * cdec_b + kv_state(rk_ref[ci * c:(ci + 1) * c, :], rv_ref[ci * c:(ci + 1) * c, :], kdec_b)
            yield None

    def ret_chunk(ci):
        r0 = pl.multiple_of(ci * c, c)
        q = rq_ref[pl.ds(r0, c), :]
        k = rk_ref[pl.ds(r0, c), :]
        v = rv_ref[pl.ds(r0, c), :]
        qm = jnp.where(rmask, q, jnp.zeros_like(q))
        sc = lax.dot_general(qm, k, (((1,), (1,)), ((), ())), preferred_element_type=F32)
        a = (sc * rt["decay"]).astype(BF16)
        yield None
        sf = rsf_scr[...]
        o = jnp.dot(a, v, preferred_element_type=F32)
        o = o + jnp.dot(qm, sf.astype(BF16), preferred_element_type=F32) * rt["qdec_f"]
        o = o + jnp.dot(qm, rsb_scr[ci], preferred_element_type=F32) * rt["qdec_b"]
        yield None
        mu = jnp.mean(o, axis=-1, keepdims=True)
        var = jnp.mean(jnp.square(o - mu), axis=-1, keepdims=True)
        on = (o - mu) * lax.rsqrt(var + EPS) * gn
        gate = rg_ref[pl.ds(r0, c), :].astype(F32)
        ro_ref[pl.ds(r0, c), :] = (on * gate).astype(ro_ref.dtype)
        yield None
        rsf_scr[...] = sf * rt["cdec_f"] + kv_state(k, v, rt["kdec_f"])
        yield None

    def run(*stages, side=None, side_steps=4 / ATTN_GROUPS):
        last = [None] * len(stages)
        done = 0
        for t, vals in enumerate(zip(*stages)):
            last = list(vals)
            while side is not None and done < (t + 1) * side_steps:
                next(side)
                done += 1
        return last

    def step(j, m_a):
        store_block(jnp.maximum(2 * j - 1, 0), (acc_scr[0], acc_scr[1]))
        m_b, _ = run(scores(2 * j + 1, sb_scr), finish(2 * j, sa_scr, m_a), side=ret_chunk(2 * j))
        m_a, _ = run(scores(2 * j + 2, sa_scr), finish(2 * j + 1, sb_scr, m_b, defer=True),
                     side=ret_chunk(2 * j + 1))
        return m_a

    acc_scr[...] = jnp.ones(acc_scr.shape, F32)
    (m_a,) = run(scores(0, sa_scr), side=backward_states(), side_steps=nc / ATTN_GROUPS)
    m_a = lax.fori_loop(0, nq // 2 - 1, step, m_a)
    store_block(nq - 3, (acc_scr[0], acc_scr[1]))
    m_b, _ = run(scores(nq - 1, sb_scr), finish(nq - 2, sa_scr, m_a), side=ret_chunk(nq - 2))
    run(finish(nq - 1, sb_scr, m_b), side=ret_chunk(nq - 1))


def _token_mixers(lq1, lk1, lq2, lk2, subg, qd, kd, vt, kd_c, vt_c,
                  dec_f, dec_b, gn_g, qr, kr, vr, gr, kr_c, vr_c):
    b, n, _ = qd.shape
    nctx = kd_c.shape[1]
    assert (n // ATTN_QB) % 2 == 0 and n % (ATTN_GROUPS * MXU_DIM) == 0 and nctx % MXU_DIM == 0
    assert ATTN_GROUPS >= 4 and nctx == RET_C and RET_C == ATTN_QB
    vec = lambda w: pl.BlockSpec((1, w), lambda bi, h: (0, 0))
    head = lambda rows: pl.BlockSpec((None, rows, LANES), lambda bi, h: (bi, 0, h))
    pair = lambda rows: pl.BlockSpec((None, rows, LANES), lambda bi, h: (bi, 0, h // 2))
    headt = lambda cols: pl.BlockSpec((None, None, VT_ROWS, cols), lambda bi, h: (bi, h, 0, 0))
    out = jax.ShapeDtypeStruct((b, n, HEADS * V_DIM), BF16)
    return pl.pallas_call(
        _mixer_kernel,
        grid=(b, HEADS),
        in_specs=[vec(HEAD_DIM)] * 4 + [vec(V_DIM), head(n), head(n), headt(n), head(nctx), headt(nctx),
                  vec(HEADS), vec(HEADS), pl.BlockSpec((1, V_DIM), lambda bi, h: (0, h)),
                  pair(n), pair(n), head(n), head(n), pair(nctx), head(nctx)],
        out_specs=[head(n), head(n)],
        out_shape=[out, out],
        scratch_shapes=[pltpu.VMEM((2, n + nctx, ATTN_QB), F32)] * 2
        + [pltpu.VMEM((n // RET_C, LANES, V_DIM), BF16), pltpu.VMEM((LANES, V_DIM), F32),
           pltpu.VMEM((2, VT_ROWS, ATTN_QB), F32)],
        compiler_params=_cparams(("arbitrary", "arbitrary")),
        name="token_mixers",
    )(lq1, lk1, lq2, lk2, subg, qd, kd, vt, kd_c, vt_c, dec_f, dec_b, gn_g, qr, kr, vr, gr, kr_c, vr_c)


def _mlp_kernel(x_ref, od_ref, or_ref, mod_ref, n2_ref, fg_ref, wo_ref, w1_ref, w2_ref, o_ref):
    d = x_ref.shape[1]
    mod = mod_ref[pl.ds(pl.program_id(0), 1), :]
    g1, sh2, sc2, g2 = [mod[:, i * d:(i + 1) * d] for i in range(4)]
    half = od_ref.shape[1]
    y = (jnp.dot(od_ref[...], wo_ref[0:half, :].astype(BF16), preferred_element_type=F32)
         + jnp.dot(or_ref[...], wo_ref[half:2 * half, :].astype(BF16), preferred_element_type=F32))
    x1 = x_ref[...] + g1 * y
    h = _rmsnorm(x1, n2_ref[...]) * (1.0 + sc2) + sh2
    hb = h.astype(BF16)
    d_ff = w1_ref.shape[1]
    tm = x_ref.shape[0]
    acc = None
    for f0 in range(0, d_ff, FF_CHUNK):
        u = jnp.dot(hb, w1_ref[:, f0:f0 + FF_CHUNK].astype(BF16), preferred_element_type=F32)
        u = jnp.square(jnp.maximum(u, 0.0)).astype(BF16)
        w2c = w2_ref[f0:f0 + FF_CHUNK, :].astype(BF16)
        if f0 + FF_CHUNK < d_ff:
            part = jnp.dot(u, w2c, preferred_element_type=F32)
            acc = part if acc is None else acc + part
        else:
            for lo in range(0, tm, tm // 2):
                rows = slice(lo, lo + tm // 2)
                x2 = x1[rows] + g2 * (acc[rows] + jnp.dot(u[rows], w2c, preferred_element_type=F32))
                o_ref[rows, :] = _rmsnorm(x2, fg_ref[...])


def _out_mlp(x, od, orr, mod, n2, fg, wo, w1, w2):
    b, n, d = x.shape
    tm = MLP_TM
    tok = lambda w: pl.BlockSpec((None, tm, w), lambda bi, t: (bi, t, 0))
    gain = pl.BlockSpec((1, d), lambda bi, t: (0, 0))
    resident = lambda a: pl.BlockSpec(a.shape, lambda bi, t: (0, 0), pipeline_mode=pl.Buffered(1))
    return pl.pallas_call(
        _mlp_kernel,
        grid=(b, n // tm),
        in_specs=[tok(d), tok(od.shape[2]), tok(orr.shape[2]),
                  pl.BlockSpec(mod.shape, lambda bi, t: (0, 0)), gain, gain,
                  resident(wo), resident(w1), resident(w2)],
        out_specs=tok(d),
        out_shape=jax.ShapeDtypeStruct((b, n, d), F32),
        compiler_params=_cparams(("arbitrary", "arbitrary")),
        name="out_mlp",
    )(x, od, orr, mod, n2, fg, wo, w1, w2)


def kernel(x, c, ctx, c_ctx, w_ada, b_ada, norm1_g, norm2_g, w_in, lambda_q1, lambda_k1, lambda_q2,
           lambda_k2, diff_subln_g, ret_decay_fwd, ret_decay_bwd, ret_gn_g, w_out, w_mlp1, w_mlp2, final_g):
    assert w_ada.shape[0] == 1, "single-layer block"
    b, n, d = x.shape

    qd, kd, vt, qr, kr, vr, gr, kd_c, vt_c, kr_c, vr_c, mod2 = _project(
        x, ctx, c, c_ctx, w_ada[0], b_ada, norm1_g, w_in[0], _rope_tables(n))

    od, orr = _token_mixers(lambda_q1, lambda_k1, lambda_q2, lambda_k2, diff_subln_g, qd, kd, vt, kd_c, vt_c,
                            ret_decay_fwd, ret_decay_bwd, ret_gn_g, qr, kr, vr, gr, kr_c, vr_c)

    return _out_mlp(x, od, orr, mod2, norm2_g, final_g[None],
                    w_out[0], w_mlp1[0], w_mlp2[0])
```

```python
import math

import numpy as np
import jax
import jax.numpy as jnp
from jax import lax
from jax.experimental import pallas as pl
from jax.experimental.pallas import tpu as pltpu

F32 = jnp.float32
BF16 = jnp.bfloat16

GRID_W = 64
HEADS = 4
HEAD_DIM = 64
V_DIM = 128
VT_ROWS = V_DIM + 16
ROPE_PAIRS = 16
ROPE_BASE = 10000.0
N_MOD = 6
MOD_ROWS = 8
EPS = 1e-6
LAM_INIT = 0.8 - 0.6 * math.exp(-0.3 * 0)
LOG2E = 1.4426950408889634
QK_SCALE = HEAD_DIM ** -0.5

LANES = 128
MXU_DIM = 256
VMEM_LIMIT_BYTES = 56 * 1024 * 1024

PROJ_TM = 1024
ATTN_QB = 256
ATTN_GROUPS = 8
RET_C = 256
MLP_TM = 512
FF_CHUNK = 1024

W_DQK = HEADS * 2 * HEAD_DIM
W_RQK = HEADS * HEAD_DIM
W_V = HEADS * V_DIM


def _col_ranges(*widths):
    edges = [sum(widths[:i]) for i in range(len(widths) + 1)]
    return list(zip(edges[:-1], edges[1:]))


COL_QD, COL_KD, COL_VD, COL_QR, COL_KR, COL_VR, COL_GR = _col_ranges(W_DQK, W_DQK, W_V, W_RQK, W_RQK, W_V, W_V)


def _cparams(sem):
    return pltpu.CompilerParams(dimension_semantics=sem, vmem_limit_bytes=VMEM_LIMIT_BYTES)


def _rmsnorm(xf, g):
    return xf * lax.rsqrt(jnp.mean(xf * xf, axis=-1, keepdims=True) + EPS) * g


def _silu(x):
    return x * (1.0 / (1.0 + jnp.exp(-x)))


def _rope_tables(n_tokens):
    rows = n_tokens // GRID_W
    row = np.repeat(np.arange(rows), GRID_W).astype(np.float64)
    col = np.tile(np.arange(GRID_W), rows).astype(np.float64)
    inv = (np.float32(ROPE_BASE) ** (-np.arange(ROPE_PAIRS, dtype=np.float32) / ROPE_PAIRS)).astype(np.float64)
    ang_r = row[:, None] * inv
    ang_c = col[:, None] * inv
    zeros = np.zeros_like(ang_r)
    cos64 = np.concatenate([np.cos(ang_r)] * 2 + [np.cos(ang_c)] * 2, axis=1)
    sa64 = np.concatenate([-np.sin(ang_r), zeros, -np.sin(ang_c), zeros], axis=1)
    sb64 = np.concatenate([zeros, np.sin(ang_r), zeros, np.sin(ang_c)], axis=1)
    dup = lambda t: jnp.asarray(np.concatenate([t, t], axis=1), dtype=F32)
    return dup(cos64), dup(sa64), dup(sb64)


def _modulated(x_ref, shift, scale, g_ref):
    h = _rmsnorm(x_ref[...], g_ref[...])
    return (h * (1.0 + scale) + shift).astype(BF16)


def _proj_kernel(x_ref, g_ref, w_ref, c_ref, sa_ref, sb_ref, xc_ref, cond_ref, cctx_ref, wa1_ref, ba1_ref, wa2_ref, ba2_ref, gn_ref,
                 qd_ref, kd_ref, vt_ref, qr_ref, kr_ref, vr_ref, gr_ref,
                 kdc_ref, vtc_ref, krc_ref, vrc_ref, mod2_ref, mod1_scr):
    d = x_ref.shape[1]
    sample = pl.program_id(0)
    n_samples = pl.num_programs(0)

    rows = mod1_scr.shape[0]
    cond = jnp.concatenate([cond_ref[...], cctx_ref[...],
                            jnp.zeros((rows - cond_ref.shape[0] - 1, d), F32)], axis=0)
    cond = _silu(cond).astype(BF16)

    @pl.when((sample == 0) & (pl.program_id(1) == 0))
    def _():
        mod1_scr[...] = jnp.dot(cond, wa1_ref[...].astype(BF16), preferred_element_type=F32) + ba1_ref[...]

    mod2_ref[...] = jnp.dot(cond, wa2_ref[...].astype(BF16), preferred_element_type=F32) + ba2_ref[...]

    def project(hb, cols):
        return jnp.dot(hb, w_ref[:, cols[0]:cols[1]].astype(BF16), preferred_element_type=F32)

    def put_vt(ref, vd):
        ones = jnp.ones((VT_ROWS - V_DIM, vd.shape[0]), BF16)
        for h in range(HEADS):
            ref[h, 0:V_DIM, :] = vd[:, h * V_DIM:(h + 1) * V_DIM].T.astype(BF16)
            ref[h, V_DIM:VT_ROWS, :] = ones

    hb = _modulated(x_ref, mod1_scr[pl.ds(sample, 1), 0:d], mod1_scr[pl.ds(sample, 1), d:2 * d], g_ref)
    c, sa, sb = c_ref[...], sa_ref[...], sb_ref[...]

    def rotate(y):
        slabs = []
        for s in range(y.shape[1] // LANES):
            ys = y[:, s * LANES:(s + 1) * LANES]
            slabs.append(ys * c + pltpu.roll(ys, LANES - ROPE_PAIRS, 1) * sa
                         + pltpu.roll(ys, ROPE_PAIRS, 1) * sb)
        return jnp.concatenate(slabs, axis=1)

    qd_ref[...] = (rotate(project(hb, COL_QD)) * (QK_SCALE * LOG2E)).astype(BF16)
    kd_ref[...] = rotate(project(hb, COL_KD)).astype(BF16)
    put_vt(vt_ref, project(hb, COL_VD))
    gr_ref[...] = (_silu(project(hb, COL_GR)) * gn_ref[...]).astype(BF16)
    qr_ref[...] = rotate(project(hb, COL_QR)).astype(BF16)
    kr_ref[...] = (rotate(project(hb, COL_KR)) * QK_SCALE).astype(BF16)
    vr_ref[...] = project(hb, COL_VR).astype(BF16)

    @pl.when(pl.program_id(1) == 0)
    def _():
        hc = _modulated(xc_ref, mod1_scr[pl.ds(n_samples, 1), 0:d], mod1_scr[pl.ds(n_samples, 1), d:2 * d], g_ref)
        kdc_ref[...] = project(hc, COL_KD).astype(BF16)
        put_vt(vtc_ref, project(hc, COL_VD))
        krc_ref[...] = (project(hc, COL_KR) * QK_SCALE).astype(BF16)
        vrc_ref[...] = project(hc, COL_VR).astype(BF16)


def _project(x, ctx, c, c_ctx, w_ada, b_ada, g, w_in, tables, gn_g):
    b, n, d = x.shape
    nctx = ctx.shape[1]
    tm = PROJ_TM
    tok = lambda w: pl.BlockSpec((None, tm, w), lambda bi, t: (bi, t, 0))
    ctok = lambda w: pl.BlockSpec((None, nctx, w), lambda bi, t: (bi, 0, 0))
    nt = n // tm
    wblk = (N_MOD - 2) * d // (b * nt)
    assert wblk % LANES == 0 and (2 * d) % wblk == 0
    shared = pl.BlockSpec((1, d), lambda bi, t: (0, 0))
    later = lambda rows: pl.BlockSpec((rows, wblk), lambda bi, t: (0, 2 * d // wblk + bi * nt + t))
    table = pl.BlockSpec((tm, LANES), lambda bi, t: (t, 0))
    out_specs = [tok(W_DQK), tok(W_DQK),
                 pl.BlockSpec((None, HEADS, VT_ROWS, tm), lambda bi, t: (bi, 0, 0, t)),
                 tok(W_RQK), tok(W_RQK), tok(W_V), tok(W_V),
                 ctok(W_DQK), pl.BlockSpec((None, HEADS, VT_ROWS, nctx), lambda bi, t: (bi, 0, 0, 0)),
                 ctok(W_RQK), ctok(W_V),
                 pl.BlockSpec((MOD_ROWS, wblk), lambda bi, t: (0, bi * nt + t))]
    out_shape = [jax.ShapeDtypeStruct((b, n, W_DQK), BF16),
                 jax.ShapeDtypeStruct((b, n, W_DQK), BF16),
                 jax.ShapeDtypeStruct((b, HEADS, VT_ROWS, n), BF16),
                 jax.ShapeDtypeStruct((b, n, W_RQK), BF16),
                 jax.ShapeDtypeStruct((b, n, W_RQK), BF16),
                 jax.ShapeDtypeStruct((b, n, W_V), BF16),
                 jax.ShapeDtypeStruct((b, n, W_V), BF16),
                 jax.ShapeDtypeStruct((b, nctx, W_DQK), BF16),
                 jax.ShapeDtypeStruct((b, HEADS, VT_ROWS, nctx), BF16),
                 jax.ShapeDtypeStruct((b, nctx, W_RQK), BF16),
                 jax.ShapeDtypeStruct((b, nctx, W_V), BF16),
                 jax.ShapeDtypeStruct((MOD_ROWS, (N_MOD - 2) * d), F32)]
    return pl.pallas_call(
        _proj_kernel,
        grid=(b, n // tm),
        in_specs=[tok(d), shared,
                  pl.BlockSpec(w_in.shape, lambda bi, t: (0, 0), pipeline_mode=pl.Buffered(1)),
                  table, table, table, ctok(d),
                  pl.BlockSpec(c.shape, lambda bi, t: (0, 0)), shared,
                  pl.BlockSpec((d, 2 * d), lambda bi, t: (0, 0), pipeline_mode=pl.Buffered(1)),
                  pl.BlockSpec((1, 2 * d), lambda bi, t: (0, 0)),
                  later(d), later(1), pl.BlockSpec(gn_g.shape, lambda bi, t: (0, 0))],
        out_specs=out_specs,
        out_shape=out_shape,
        scratch_shapes=[pltpu.VMEM((MOD_ROWS, 2 * d), F32)],
        compiler_params=_cparams(("arbitrary", "arbitrary")),
        name="proj",
    )(x, g, w_in, *tables, ctx, c, c_ctx[None], w_ada, b_ada, w_ada, b_ada, gn_g)


def _mixer_kernel(lq1_ref, lk1_ref, lq2_ref, lk2_ref, subg_ref, q_ref, k_ref, vt_ref, kc_ref, vtc_ref,
                  decf_ref, decb_ref, rq_ref, rk_ref, rv_ref, rg_ref, rkc_ref, rvc_ref,
                  o_ref, ro_ref, sa_scr, sb_scr, rsb_scr, rsf_scr, acc_scr):
    n = q_ref.shape[0]
    head = pl.program_id(1)
    lane = lax.broadcasted_iota(jnp.int32, (1, LANES), 1)

    lam = (jnp.exp(jnp.sum(lq1_ref[...] * lk1_ref[...], axis=-1, keepdims=True))
           - jnp.exp(jnp.sum(lq2_ref[...] * lk2_ref[...], axis=-1, keepdims=True)) + LAM_INIT)
    first_map = lane < HEAD_DIM
    subg = subg_ref[...]
    nq = n // ATTN_QB

    nctx = kc_ref.shape[0]
    gsz = n // ATTN_GROUPS
    groups = [[(k_ref, vt_ref, g * gsz, (g + 1) * gsz, nctx + g * gsz)] for g in range(ATTN_GROUPS)]
    groups[0].insert(0, (kc_ref, vtc_ref, 0, nctx, 0))

    def scores(i, s_scr):
        r0 = pl.multiple_of(i * ATTN_QB, ATTN_QB)
        q = q_ref[pl.ds(r0, ATTN_QB), :]
        zero = jnp.zeros_like(q)
        qms = (jnp.where(first_map, q, zero), jnp.where(first_map, zero, q))
        ms = [None, None]
        for pieces in groups:
            for kr, _, lo, hi, dst in pieces:
                for mp in range(2):
                    s = lax.dot_general(kr[lo:hi, :], qms[mp], (((1,), (1,)), ((), ())),
                                        preferred_element_type=F32)
                    s_scr[mp, dst:dst + hi - lo, :] = s
                    mg = jnp.max(s, axis=0, keepdims=True)
                    ms[mp] = mg if ms[mp] is None else jnp.maximum(ms[mp], mg)
            yield tuple(ms)

    def store_block(i, accs):
        ots = [a[0:V_DIM, :] / a[V_DIM:V_DIM + 1, :] for a in accs]
        ot = ots[0] - lam * ots[1]
        ot = ot * lax.rsqrt(jnp.mean(ot * ot, axis=0, keepdims=True) + EPS)
        o = ot.T * (subg * (1.0 - LAM_INIT))
        r0 = pl.multiple_of(i * ATTN_QB, ATTN_QB)
        o_ref[pl.ds(r0, ATTN_QB), :] = o.astype(o_ref.dtype)

    def finish(i, s_scr, ms, defer=False):
        accs = [None, None]
        for g, pieces in enumerate(groups):
            for _, vr, lo, hi, dst in pieces:
                for mp in range(2):
                    p = jnp.exp2(s_scr[mp, dst:dst + hi - lo, :] - ms[mp]).astype(BF16)
                    part = jnp.dot(vr[:, lo:hi], p, preferred_element_type=F32)
                    accs[mp] = part if accs[mp] is None else accs[mp] + part
            if g == len(groups) - 1:
                if defer:
                    acc_scr[0], acc_scr[1] = accs
                else:
                    store_block(i, accs)
            yield None

    c = RET_C
    nc = n // c

    def log_gamma(dec):
        z = -dec
        return -(jnp.maximum(z, 0.0) + jnp.log(1.0 + jnp.exp(-jnp.abs(z))))

    hsel = lax.broadcasted_iota(jnp.int32, (1, HEADS), 1) == head

    def pick(vec):
        return jnp.sum(jnp.where(hsel, vec, 0.0), axis=-1, keepdims=True)

    rmask = (lane >= HEAD_DIM).astype(jnp.int32) == head % 2
    rt = {}

    def kv_state(k, v, kdec):
        kd = (k.astype(F32) * kdec).astype(BF16)
        return lax.dot_general(kd, v, (((0,), (0,)), ((), ())), preferred_element_type=F32)

    def backward_states():
        lgf, lgb = pick(log_gamma(decf_ref[...])), pick(log_gamma(decb_ref[...]))
        ii = lax.broadcasted_iota(jnp.int32, (c, c), 0).astype(F32)
        jj = lax.broadcasted_iota(jnp.int32, (c, c), 1).astype(F32)
        rel = ii - jj
        pos = lax.broadcasted_iota(jnp.int32, (c, LANES), 0).astype(F32)
        rt["decay"] = (jnp.where(rel >= 0, jnp.exp(jnp.maximum(rel, 0.0) * lgf), 0.0)
                       + jnp.where(rel <= 0, jnp.exp(jnp.maximum(-rel, 0.0) * lgb), 0.0))
        rt["qdec_f"], rt["qdec_b"] = jnp.exp((pos + 1.0) * lgf), jnp.exp((c - pos) * lgb)
        rt["kdec_f"], kdec_b = jnp.exp((c - 1.0 - pos) * lgf), jnp.exp(pos * lgb)
        rt["cdec_f"], cdec_b = jnp.exp(c * lgf), jnp.exp(c * lgb)
        rsf_scr[...] = kv_state(rkc_ref[...], rvc_ref[...], rt["kdec_f"])
        sb = kv_state(rkc_ref[...], rvc_ref[...], kdec_b)
        for ci in reversed(range(nc)):
            rsb_scr[ci] = sb.astype(BF16)
            if ci > 0:
                sb = sb * cdec_b + kv_state(rk_ref[ci * c:(ci + 1) * c, :], rv_ref[ci * c:(ci + 1) * c, :], kdec_b)
            yield None

    def ret_chunk(ci):
        r0 = pl.multiple_of(ci * c, c)
        q = rq_ref[pl.ds(r0, c), :]
        k = rk_ref[pl.ds(r0, c), :]
        v = rv_ref[pl.ds(r0, c), :]
        qm = jnp.where(rmask, q, jnp.zeros_like(q))
        sc = lax.dot_general(qm, k, (((1,), (1,)), ((), ())), preferred_element_type=F32)
        a = (sc * rt["decay"]).astype(BF16)
        yield None
        sf = rsf_scr[...]
        o = jnp.dot(a, v, preferred_element_type=F32)
        o = o + jnp.dot(qm, sf.astype(BF16), preferred_element_type=F32) * rt["qdec_f"]
        o = o + jnp.dot(qm, rsb_scr[ci], preferred_element_type=F32) * rt["qdec_b"]
        yield None
        mu = jnp.mean(o, axis=-1, keepdims=True)
        var = jnp.mean(jnp.square(o - mu), axis=-1, keepdims=True)
        on = (o - mu) * lax.rsqrt(var + EPS)
        gate = rg_ref[pl.ds(r0, c), :].astype(F32)
        ro_ref[pl.ds(r0, c), :] = (on * gate).astype(ro_ref.dtype)
        yield None
        rsf_scr[...] = sf * rt["cdec_f"] + kv_state(k, v, rt["kdec_f"])
        yield None

    def run(*stages, side=None, side_steps=4 / ATTN_GROUPS):
        last = [None] * len(stages)
        done = 0
        for t, vals in enumerate(zip(*stages)):
            last = list(vals)
            while side is not None and done < (t + 1) * side_steps:
                next(side)
                done += 1
        return last

    def step(j, m_a):
        store_block(jnp.maximum(2 * j - 1, 0), (acc_scr[0], acc_scr[1]))
        m_b, _ = run(scores(2 * j + 1, sb_scr), finish(2 * j, sa_scr, m_a), side=ret_chunk(2 * j))
        m_a, _ = run(scores(2 * j + 2, sa_scr), finish(2 * j + 1, sb_scr, m_b, defer=True),
                     side=ret_chunk(2 * j + 1))
        return m_a

    acc_scr[...] = jnp.ones(acc_scr.shape, F32)
    (m_a,) = run(scores(0, sa_scr), side=backward_states(), side_steps=nc / ATTN_GROUPS)
    m_a = lax.fori_loop(0, nq // 2 - 1, step, m_a)
    store_block(nq - 3, (acc_scr[0], acc_scr[1]))
    m_b, _ = run(scores(nq - 1, sb_scr), finish(nq - 2, sa_scr, m_a), side=ret_chunk(nq - 2))
    run(finish(nq - 1, sb_scr, m_b), side=ret_chunk(nq - 1))


def _token_mixers(lq1, lk1, lq2, lk2, subg, qd, kd, vt, kd_c, vt_c,
                  dec_f, dec_b, qr, kr, vr, gr, kr_c, vr_c):
    b, n, _ = qd.shape
    nctx = kd_c.shape[1]
    assert (n // ATTN_QB) % 2 == 0 and n % (ATTN_GROUPS * MXU_DIM) == 0 and nctx % MXU_DIM == 0
    assert ATTN_GROUPS >= 4 and nctx == RET_C and RET_C == ATTN_QB
    vec = lambda w: pl.BlockSpec((1, w), lambda bi, h: (0, 0))
    head = lambda rows: pl.BlockSpec((None, rows, LANES), lambda bi, h: (bi, 0, h))
    pair = lambda rows: pl.BlockSpec((None, rows, LANES), lambda bi, h: (bi, 0, h // 2))
    headt = lambda cols: pl.BlockSpec((None, None, VT_ROWS, cols), lambda bi, h: (bi, h, 0, 0))
    out = jax.ShapeDtypeStruct((b, n, HEADS * V_DIM), BF16)
    return pl.pallas_call(
        _mixer_kernel,
        grid=(b, HEADS),
        in_specs=[vec(HEAD_DIM)] * 4 + [vec(V_DIM), head(n), head(n), headt(n), head(nctx), headt(nctx),
                  vec(HEADS), vec(HEADS),
                  pair(n), pair(n), head(n), head(n), pair(nctx), head(nctx)],
        out_specs=[head(n), head(n)],
        out_shape=[out, out],
        scratch_shapes=[pltpu.VMEM((2, n + nctx, ATTN_QB), F32)] * 2
        + [pltpu.VMEM((n // RET_C, LANES, V_DIM), BF16), pltpu.VMEM((LANES, V_DIM), F32),
           pltpu.VMEM((2, VT_ROWS, ATTN_QB), F32)],
        compiler_params=_cparams(("arbitrary", "arbitrary")),
        name="token_mixers",
    )(lq1, lk1, lq2, lk2, subg, qd, kd, vt, kd_c, vt_c, dec_f, dec_b, qr, kr, vr, gr, kr_c, vr_c)


def _mlp_kernel(x_ref, od_ref, or_ref, mod_ref, n2_ref, fg_ref, wo_ref, w1_ref, w2_ref, o_ref):
    d = x_ref.shape[1]
    mod = mod_ref[pl.ds(pl.program_id(0), 1), :]
    g1, sh2, sc2, g2 = [mod[:, i * d:(i + 1) * d] for i in range(4)]
    half = od_ref.shape[1]
    y = (jnp.dot(od_ref[...], wo_ref[0:half, :].astype(BF16), preferred_element_type=F32)
         + jnp.dot(or_ref[...], wo_ref[half:2 * half, :].astype(BF16), preferred_element_type=F32))
    x1 = x_ref[...] + g1 * y
    h = _rmsnorm(x1, n2_ref[...]) * (1.0 + sc2) + sh2
    hb = h.astype(BF16)
    d_ff = w1_ref.shape[1]
    tm = x_ref.shape[0]
    acc = None
    for f0 in range(0, d_ff, FF_CHUNK):
        u = jnp.dot(hb, w1_ref[:, f0:f0 + FF_CHUNK].astype(BF16), preferred_element_type=F32)
        u = jnp.square(jnp.maximum(u, 0.0)).astype(BF16)
        w2c = w2_ref[f0:f0 + FF_CHUNK, :].astype(BF16)
        if f0 + FF_CHUNK < d_ff:
            part = jnp.dot(u, w2c, preferred_element_type=F32)
            acc = part if acc is None else acc + part
        else:
            for lo in range(0, tm, tm // 2):
                rows = slice(lo, lo + tm // 2)
                x2 = x1[rows] + g2 * (acc[rows] + jnp.dot(u[rows], w2c, preferred_element_type=F32))
                o_ref[rows, :] = _rmsnorm(x2, fg_ref[...])


def _out_mlp(x, od, orr, mod, n2, fg, wo, w1, w2):
    b, n, d = x.shape
    tm = MLP_TM
    tok = lambda w: pl.BlockSpec((None, tm, w), lambda bi, t: (bi, t, 0))
    gain = pl.BlockSpec((1, d), lambda bi, t: (0, 0))
    resident = lambda a: pl.BlockSpec(a.shape, lambda bi, t: (0, 0), pipeline_mode=pl.Buffered(1))
    return pl.pallas_call(
        _mlp_kernel,
        grid=(b, n // tm),
        in_specs=[tok(d), tok(od.shape[2]), tok(orr.shape[2]),
                  pl.BlockSpec(mod.shape, lambda bi, t: (0, 0)), gain, gain,
                  resident(wo), resident(w1), resident(w2)],
        out_specs=tok(d),
        out_shape=jax.ShapeDtypeStruct((b, n, d), F32),
        compiler_params=_cparams(("arbitrary", "arbitrary")),
        name="out_mlp",
    )(x, od, orr, mod, n2, fg, wo, w1, w2)


def kernel(x, c, ctx, c_ctx, w_ada, b_ada, norm1_g, norm2_g, w_in, lambda_q1, lambda_k1, lambda_q2,
           lambda_k2, diff_subln_g, ret_decay_fwd, ret_decay_bwd, ret_gn_g, w_out, w_mlp1, w_mlp2, final_g):
    assert w_ada.shape[0] == 1, "single-layer block"
    b, n, d = x.shape

    qd, kd, vt, qr, kr, vr, gr, kd_c, vt_c, kr_c, vr_c, mod2 = _project(
        x, ctx, c, c_ctx, w_ada[0], b_ada, norm1_g, w_in[0], _rope_tables(n), ret_gn_g)

    od, orr = _token_mixers(lambda_q1, lambda_k1, lambda_q2, lambda_k2, diff_subln_g, qd, kd, vt, kd_c, vt_c,
                            ret_decay_fwd, ret_decay_bwd, qr, kr, vr, gr, kr_c, vr_c)

    return _out_mlp(x, od, orr, mod2, norm2_g, final_g[None],
                    w_out[0], w_mlp1[0], w_mlp2[0])
```

```python
import math

import numpy as np
import jax
import jax.numpy as jnp
from jax import lax
from jax.experimental import pallas as pl
from jax.experimental.pallas import tpu as pltpu

F32 = jnp.float32
BF16 = jnp.bfloat16

GRID_W = 64
HEADS = 4
HEAD_DIM = 64
V_DIM = 128
VT_ROWS = V_DIM + 16
ROPE_PAIRS = 16
ROPE_BASE = 10000.0
N_MOD = 6
MOD_ROWS = 8
EPS = 1e-6
LAM_INIT = 0.8 - 0.6 * math.exp(-0.3 * 0)
LOG2E = 1.4426950408889634
QK_SCALE = HEAD_DIM ** -0.5

LANES = 128
MXU_DIM = 256
VMEM_LIMIT_BYTES = 56 * 1024 * 1024

PROJ_TM = 1024
ATTN_QB = 256
ATTN_GROUPS = 8
RET_C = 256
MLP_TM = 512
FF_CHUNK = 1024

W_DQK = HEADS * 2 * HEAD_DIM
W_RQK = HEADS * HEAD_DIM
W_V = HEADS * V_DIM


def _col_ranges(*widths):
    edges = [sum(widths[:i]) for i in range(len(widths) + 1)]
    return list(zip(edges[:-1], edges[1:]))


COL_QD, COL_KD, COL_VD, COL_QR, COL_KR, COL_VR, COL_GR = _col_ranges(W_DQK, W_DQK, W_V, W_RQK, W_RQK, W_V, W_V)


def _cparams(sem):
    return pltpu.CompilerParams(dimension_semantics=sem, vmem_limit_bytes=VMEM_LIMIT_BYTES)


def _rmsnorm(xf, g):
    return xf * lax.rsqrt(jnp.mean(xf * xf, axis=-1, keepdims=True) + EPS) * g


def _silu(x):
    return x * (1.0 / (1.0 + jnp.exp(-x)))


def _rope_tables(n_tokens):
    rows = n_tokens // GRID_W
    row = np.repeat(np.arange(rows), GRID_W).astype(np.float64)
    col = np.tile(np.arange(GRID_W), rows).astype(np.float64)
    inv = (np.float32(ROPE_BASE) ** (-np.arange(ROPE_PAIRS, dtype=np.float32) / ROPE_PAIRS)).astype(np.float64)
    ang_r = row[:, None] * inv
    ang_c = col[:, None] * inv
    zeros = np.zeros_like(ang_r)
    cos64 = np.concatenate([np.cos(ang_r)] * 2 + [np.cos(ang_c)] * 2, axis=1)
    sa64 = np.concatenate([-np.sin(ang_r), zeros, -np.sin(ang_c), zeros], axis=1)
    sb64 = np.concatenate([zeros, np.sin(ang_r), zeros, np.sin(ang_c)], axis=1)
    dup = lambda t: jnp.asarray(np.concatenate([t, t], axis=1), dtype=F32)
    return dup(cos64), dup(sa64), dup(sb64)


def _modulated(x_ref, shift, scale, g_ref):
    h = _rmsnorm(x_ref[...], g_ref[...])
    return (h * (1.0 + scale) + shift).astype(BF16)


def _proj_kernel(x_ref, g_ref, w_ref, c_ref, sa_ref, sb_ref, xc_ref, cond_ref, cctx_ref, wa1_ref, ba1_ref, wa2_ref, ba2_ref,
                 qd_ref, kd_ref, vt_ref, qr_ref, kr_ref, vr_ref, gr_ref,
                 kdc_ref, vtc_ref, krc_ref, vrc_ref, mod2_ref, mod1_scr):
    d = x_ref.shape[1]
    sample = pl.program_id(0)
    n_samples = pl.num_programs(0)

    rows = mod1_scr.shape[0]
    cond = jnp.concatenate([cond_ref[...], cctx_ref[...],
                            jnp.zeros((rows - cond_ref.shape[0] - 1, d), F32)], axis=0)
    cond = _silu(cond).astype(BF16)

    @pl.when((sample == 0) & (pl.program_id(1) == 0))
    def _():
        mod1_scr[...] = jnp.dot(cond, wa1_ref[...].astype(BF16), preferred_element_type=F32) + ba1_ref[...]

    mod2_ref[...] = jnp.dot(cond, wa2_ref[...].astype(BF16), preferred_element_type=F32) + ba2_ref[...]

    def project(hb, cols):
        return jnp.dot(hb, w_ref[:, cols[0]:cols[1]].astype(BF16), preferred_element_type=F32)

    def put_vt(ref, vd):
        ones = jnp.ones((VT_ROWS - V_DIM, vd.shape[0]), BF16)
        for h in range(HEADS):
            ref[h, 0:V_DIM, :] = vd[:, h * V_DIM:(h + 1) * V_DIM].T.astype(BF16)
            ref[h, V_DIM:VT_ROWS, :] = ones

    hb = _modulated(x_ref, mod1_scr[pl.ds(sample, 1), 0:d], mod1_scr[pl.ds(sample, 1), d:2 * d], g_ref)
    c, sa, sb = c_ref[...], sa_ref[...], sb_ref[...]

    def rotate(y):
        slabs = []
        for s in range(y.shape[1] // LANES):
            ys = y[:, s * LANES:(s + 1) * LANES]
            slabs.append(ys * c + pltpu.roll(ys, LANES - ROPE_PAIRS, 1) * sa
                         + pltpu.roll(ys, ROPE_PAIRS, 1) * sb)
        return jnp.concatenate(slabs, axis=1)

    qd_ref[...] = (rotate(project(hb, COL_QD)) * (QK_SCALE * LOG2E)).astype(BF16)
    kd_ref[...] = rotate(project(hb, COL_KD)).astype(BF16)
    put_vt(vt_ref, project(hb, COL_VD))
    gr_ref[...] = _silu(project(hb, COL_GR)).astype(BF16)
    qr_ref[...] = rotate(project(hb, COL_QR)).astype(BF16)
    kr_ref[...] = (rotate(project(hb, COL_KR)) * QK_SCALE).astype(BF16)
    vr_ref[...] = project(hb, COL_VR).astype(BF16)

    @pl.when(pl.program_id(1) == 0)
    def _():
        hc = _modulated(xc_ref, mod1_scr[pl.ds(n_samples, 1), 0:d], mod1_scr[pl.ds(n_samples, 1), d:2 * d], g_ref)
        kdc_ref[...] = project(hc, COL_KD).astype(BF16)
        put_vt(vtc_ref, project(hc, COL_VD))
        krc_ref[...] = (project(hc, COL_KR) * QK_SCALE).astype(BF16)
        vrc_ref[...] = project(hc, COL_VR).astype(BF16)


def _project(x, ctx, c, c_ctx, w_ada, b_ada, g, w_in, tables):
    b, n, d = x.shape
    nctx = ctx.shape[1]
    tm = PROJ_TM
    tok = lambda w: pl.BlockSpec((None, tm, w), lambda bi, t: (bi, t, 0))
    ctok = lambda w: pl.BlockSpec((None, nctx, w), lambda bi, t: (bi, 0, 0))
    nt = n // tm
    wblk = (N_MOD - 2) * d // (b * nt)
    assert wblk % LANES == 0 and (2 * d) % wblk == 0
    shared = pl.BlockSpec((1, d), lambda bi, t: (0, 0))
    later = lambda rows: pl.BlockSpec((rows, wblk), lambda bi, t: (0, 2 * d // wblk + bi * nt + t))
    table = pl.BlockSpec((tm, LANES), lambda bi, t: (t, 0))
    out_specs = [tok(W_DQK), tok(W_DQK),
                 pl.BlockSpec((None, HEADS, VT_ROWS, tm), lambda bi, t: (bi, 0, 0, t)),
                 tok(W_RQK), tok(W_RQK), tok(W_V), tok(W_V),
                 ctok(W_DQK), pl.BlockSpec((None, HEADS, VT_ROWS, nctx), lambda bi, t: (bi, 0, 0, 0)),
                 ctok(W_RQK), ctok(W_V),
                 pl.BlockSpec((MOD_ROWS, wblk), lambda bi, t: (0, bi * nt + t))]
    out_shape = [jax.ShapeDtypeStruct((b, n, W_DQK), BF16),
                 jax.ShapeDtypeStruct((b, n, W_DQK), BF16),
                 jax.ShapeDtypeStruct((b, HEADS, VT_ROWS, n), BF16),
                 jax.ShapeDtypeStruct((b, n, W_RQK), BF16),
                 jax.ShapeDtypeStruct((b, n, W_RQK), BF16),
                 jax.ShapeDtypeStruct((b, n, W_V), BF16),
                 jax.ShapeDtypeStruct((b, n, W_V), BF16),
                 jax.ShapeDtypeStruct((b, nctx, W_DQK), BF16),
                 jax.ShapeDtypeStruct((b, HEADS, VT_ROWS, nctx), BF16),
                 jax.ShapeDtypeStruct((b, nctx, W_RQK), BF16),
                 jax.ShapeDtypeStruct((b, nctx, W_V), BF16),
                 jax.ShapeDtypeStruct((MOD_ROWS, (N_MOD - 2) * d), F32)]
    return pl.pallas_call(
        _proj_kernel,
        grid=(b, n // tm),
        in_specs=[tok(d), shared,
                  pl.BlockSpec(w_in.shape, lambda bi, t: (0, 0), pipeline_mode=pl.Buffered(1)),
                  table, table, table, ctok(d),
                  pl.BlockSpec(c.shape, lambda bi, t: (0, 0)), shared,
                  pl.BlockSpec((d, 2 * d), lambda bi, t: (0, 0), pipeline_mode=pl.Buffered(1)),
                  pl.BlockSpec((1, 2 * d), lambda bi, t: (0, 0)),
                  later(d), later(1)],
        out_specs=out_specs,
        out_shape=out_shape,
        scratch_shapes=[pltpu.VMEM((MOD_ROWS, 2 * d), F32)],
        compiler_params=_cparams(("arbitrary", "arbitrary")),
        name="proj",
    )(x, g, w_in, *tables, ctx, c, c_ctx[None], w_ada, b_ada, w_ada, b_ada)


def _mixer_kernel(lq1_ref, lk1_ref, lq2_ref, lk2_ref, subg_ref, q_ref, k_ref, vt_ref, kc_ref, vtc_ref,
                  decf_ref, decb_ref, gn_ref, rq_ref, rk_ref, rv_ref, rg_ref, rkc_ref, rvc_ref,
                  o_ref, ro_ref, sa_scr, sb_scr, rsb_scr, rsf_scr, acc_scr):
    n = q_ref.shape[0]
    head = pl.program_id(1)
    lane = lax.broadcasted_iota(jnp.int32, (1, LANES), 1)

    lam = (jnp.exp(jnp.sum(lq1_ref[...] * lk1_ref[...], axis=-1, keepdims=True))
           - jnp.exp(jnp.sum(lq2_ref[...] * lk2_ref[...], axis=-1, keepdims=True)) + LAM_INIT)
    first_map = lane < HEAD_DIM
    subg = subg_ref[...]
    nq = n // ATTN_QB

    nctx = kc_ref.shape[0]
    gsz = n // ATTN_GROUPS
    groups = [[(k_ref, vt_ref, g * gsz, (g + 1) * gsz, nctx + g * gsz)] for g in range(ATTN_GROUPS)]
    groups[0].insert(0, (kc_ref, vtc_ref, 0, nctx, 0))

    def scores(i, s_scr):
        r0 = pl.multiple_of(i * ATTN_QB, ATTN_QB)
        q = q_ref[pl.ds(r0, ATTN_QB), :]
        zero = jnp.zeros_like(q)
        qms = (jnp.where(first_map, q, zero), jnp.where(first_map, zero, q))
        ms = [None, None]
        for pieces in groups:
            for kr, _, lo, hi, dst in pieces:
                for mp in range(2):
                    s = lax.dot_general(kr[lo:hi, :], qms[mp], (((1,), (1,)), ((), ())),
                                        preferred_element_type=F32)
                    s_scr[mp, dst:dst + hi - lo, :] = s
                    mg = jnp.max(s, axis=0, keepdims=True)
                    ms[mp] = mg if ms[mp] is None else jnp.maximum(ms[mp], mg)
            yield tuple(ms)

    def store_block(i, accs):
        ots = [a[0:V_DIM, :] / a[V_DIM:V_DIM + 1, :] for a in accs]
        ot = ots[0] - lam * ots[1]
        ot = ot * lax.rsqrt(jnp.mean(ot * ot, axis=0, keepdims=True) + EPS)
        o = ot.T * (subg * (1.0 - LAM_INIT))
        r0 = pl.multiple_of(i * ATTN_QB, ATTN_QB)
        o_ref[pl.ds(r0, ATTN_QB), :] = o.astype(o_ref.dtype)

    def finish(i, s_scr, ms, defer=False):
        accs = [None, None]
        for g, pieces in enumerate(groups):
            for _, vr, lo, hi, dst in pieces:
                for mp in range(2):
                    p = jnp.exp2(s_scr[mp, dst:dst + hi - lo, :] - ms[mp]).astype(BF16)
                    part = jnp.dot(vr[:, lo:hi], p, preferred_element_type=F32)
                    accs[mp] = part if accs[mp] is None else accs[mp] + part
            if g == len(groups) - 1:
                if defer:
                    acc_scr[0], acc_scr[1] = accs
                else:
                    store_block(i, accs)
            yield None

    c = RET_C
    nc = n // c

    def log_gamma(dec):
        z = -dec
        return -(jnp.maximum(z, 0.0) + jnp.log(1.0 + jnp.exp(-jnp.abs(z))))

    hsel = lax.broadcasted_iota(jnp.int32, (1, HEADS), 1) == head

    def pick(vec):
        return jnp.sum(jnp.where(hsel, vec, 0.0), axis=-1, keepdims=True)

    rmask = (lane >= HEAD_DIM).astype(jnp.int32) == head % 2
    gn = gn_ref[...]
    rt = {}

    def kv_state(k, v, kdec):
        kd = (k.astype(F32) * kdec).astype(BF16)
        return lax.dot_general(kd, v, (((0,), (0,)), ((), ())), preferred_element_type=F32)

    def backward_states():
        lgf, lgb = pick(log_gamma(decf_ref[...])), pick(log_gamma(decb_ref[...]))
        ii = lax.broadcasted_iota(jnp.int32, (c, c), 0).astype(F32)
        jj = lax.broadcasted_iota(jnp.int32, (c, c), 1).astype(F32)
        rel = ii - jj
        pos = lax.broadcasted_iota(jnp.int32, (c, LANES), 0).astype(F32)
        rt["decay"] = (jnp.where(rel >= 0, jnp.exp(jnp.maximum(rel, 0.0) * lgf), 0.0)
                       + jnp.where(rel <= 0, jnp.exp(jnp.maximum(-rel, 0.0) * lgb), 0.0))
        rt["qdec_f"], rt["qdec_b"] = jnp.exp((pos + 1.0) * lgf), jnp.exp((c - pos) * lgb)
        rt["kdec_f"], kdec_b = jnp.exp((c - 1.0 - pos) * lgf), jnp.exp(pos * lgb)
        rt["cdec_f"], cdec_b = jnp.exp(c * lgf), jnp.exp(c * lgb)
        rsf_scr[...] = kv_state(rkc_ref[...], rvc_ref[...], rt["kdec_f"])
        sb = kv_state(rkc_ref[...], rvc_ref[...], kdec_b)
        for ci in reversed(range(nc)):
            rsb_scr[ci] = sb.astype(BF16)
            if ci > 0:
                sb = sb * cdec_b + kv_state(rk_ref[ci * c:(ci + 1) * c, :], rv_ref[ci * c:(ci + 1) * c, :], kdec_b)
            yield None

    def ret_chunk(ci):
        r0 = pl.multiple_of(ci * c, c)
        q = rq_ref[pl.ds(r0, c), :]
        k = rk_ref[pl.ds(r0, c), :]
        v = rv_ref[pl.ds(r0, c), :]
        qm = jnp.where(rmask, q, jnp.zeros_like(q))
        sc = lax.dot_general(qm, k, (((1,), (1,)), ((), ())), preferred_element_type=F32)
        yield None
        a = (sc * rt["decay"]).astype(BF16)
        yield None
        sf = rsf_scr[...]
        o = jnp.dot(a, v, preferred_element_type=F32)
        yield None
        o = o + jnp.dot(qm, sf.astype(BF16), preferred_element_type=F32) * rt["qdec_f"]
        o = o + jnp.dot(qm, rsb_scr[ci], preferred_element_type=F32) * rt["qdec_b"]
        yield None
        mu = jnp.mean(o, axis=-1, keepdims=True)
        var = jnp.mean(jnp.square(o - mu), axis=-1, keepdims=True)
        yield None
        on = (o - mu) * lax.rsqrt(var + EPS) * gn
        gate = rg_ref[pl.ds(r0, c), :].astype(F32)
        ro_ref[pl.ds(r0, c), :] = (on * gate).astype(ro_ref.dtype)
        yield None
        rsf_scr[...] = sf * rt["cdec_f"] + kv_state(k, v, rt["kdec_f"])
        yield None
        yield None

    def run(*stages, side=None, side_steps=8 / ATTN_GROUPS):
        last = [None] * len(stages)
        done = 0
        for t, vals in enumerate(zip(*stages)):
            last = list(vals)
            while side is not None and done < (t + 1) * side_steps:
                next(side)
                done += 1
        return last

    def step(j, m_a):
        store_block(jnp.maximum(2 * j - 1, 0), (acc_scr[0], acc_scr[1]))
        m_b, _ = run(scores(2 * j + 1, sb_scr), finish(2 * j, sa_scr, m_a), side=ret_chunk(2 * j))
        m_a, _ = run(scores(2 * j + 2, sa_scr), finish(2 * j + 1, sb_scr, m_b, defer=True),
                     side=ret_chunk(2 * j + 1))
        return m_a

    acc_scr[...] = jnp.ones(acc_scr.shape, F32)
    (m_a,) = run(scores(0, sa_scr), side=backward_states(), side_steps=nc / ATTN_GROUPS)
    m_a = lax.fori_loop(0, nq // 2 - 1, step, m_a)
    store_block(nq - 3, (acc_scr[0], acc_scr[1]))
    m_b, _ = run(scores(nq - 1, sb_scr), finish(nq - 2, sa_scr, m_a), side=ret_chunk(nq - 2))
    run(finish(nq - 1, sb_scr, m_b), side=ret_chunk(nq - 1))


def _token_mixers(lq1, lk1, lq2, lk2, subg, qd, kd, vt, kd_c, vt_c,
                  dec_f, dec_b, gn_g, qr, kr, vr, gr, kr_c, vr_c):
    b, n, _ = qd.shape
    nctx = kd_c.shape[1]
    assert (n // ATTN_QB) % 2 == 0 and n % (ATTN_GROUPS * MXU_DIM) == 0 and nctx % MXU_DIM == 0
    assert ATTN_GROUPS >= 4 and nctx == RET_C and RET_C == ATTN_QB
    vec = lambda w: pl.BlockSpec((1, w), lambda bi, h: (0, 0))
    head = lambda rows: pl.BlockSpec((None, rows, LANES), lambda bi, h: (bi, 0, h))
    pair = lambda rows: pl.BlockSpec((None, rows, LANES), lambda bi, h: (bi, 0, h // 2))
    headt = lambda cols: pl.BlockSpec((None, None, VT_ROWS, cols), lambda bi, h: (bi, h, 0, 0))
    out = jax.ShapeDtypeStruct((b, n, HEADS * V_DIM), BF16)
    return pl.pallas_call(
        _mixer_kernel,
        grid=(b, HEADS),
        in_specs=[vec(HEAD_DIM)] * 4 + [vec(V_DIM), head(n), head(n), headt(n), head(nctx), headt(nctx),
                  vec(HEADS), vec(HEADS), pl.BlockSpec((1, V_DIM), lambda bi, h: (0, h)),
                  pair(n), pair(n), head(n), head(n), pair(nctx), head(nctx)],
        out_specs=[head(n), head(n)],
        out_shape=[out, out],
        scratch_shapes=[pltpu.VMEM((2, n + nctx, ATTN_QB), F32)] * 2
        + [pltpu.VMEM((n // RET_C, LANES, V_DIM), BF16), pltpu.VMEM((LANES, V_DIM), F32),
           pltpu.VMEM((2, VT_ROWS, ATTN_QB), F32)],
        compiler_params=_cparams(("arbitrary", "arbitrary")),
        name="token_mixers",
    )(lq1, lk1, lq2, lk2, subg, qd, kd, vt, kd_c, vt_c, dec_f, dec_b, gn_g, qr, kr, vr, gr, kr_c, vr_c)


def _mlp_kernel(x_ref, od_ref, or_ref, mod_ref, n2_ref, fg_ref, wo_ref, w1_ref, w2_ref, o_ref):
    d = x_ref.shape[1]
    mod = mod_ref[pl.ds(pl.program_id(0), 1), :]
    g1, sh2, sc2, g2 = [mod[:, i * d:(i + 1) * d] for i in range(4)]
    half = od_ref.shape[1]
    y = (jnp.dot(od_ref[...], wo_ref[0:half, :].astype(BF16), preferred_element_type=F32)
         + jnp.dot(or_ref[...], wo_ref[half:2 * half, :].astype(BF16), preferred_element_type=F32))
    x1 = x_ref[...] + g1 * y
    h = _rmsnorm(x1, n2_ref[...]) * (1.0 + sc2) + sh2
    hb = h.astype(BF16)
    d_ff = w1_ref.shape[1]
    tm = x_ref.shape[0]
    acc = None
    for f0 in range(0, d_ff, FF_CHUNK):
        u = jnp.dot(hb, w1_ref[:, f0:f0 + FF_CHUNK].astype(BF16), preferred_element_type=F32)
        u = jnp.square(jnp.maximum(u, 0.0)).astype(BF16)
        w2c = w2_ref[f0:f0 + FF_CHUNK, :].astype(BF16)
        if f0 + FF_CHUNK < d_ff:
            part = jnp.dot(u, w2c, preferred_element_type=F32)
            acc = part if acc is None else acc + part
        else:
            for lo in range(0, tm, tm // 2):
                rows = slice(lo, lo + tm // 2)
                x2 = x1[rows] + g2 * (acc[rows] + jnp.dot(u[rows], w2c, preferred_element_type=F32))
                o_ref[rows, :] = _rmsnorm(x2, fg_ref[...])


def _out_mlp(x, od, orr, mod, n2, fg, wo, w1, w2):
    b, n, d = x.shape
    tm = MLP_TM
    tok = lambda w: pl.BlockSpec((None, tm, w), lambda bi, t: (bi, t, 0))
    gain = pl.BlockSpec((1, d), lambda bi, t: (0, 0))
    resident = lambda a: pl.BlockSpec(a.shape, lambda bi, t: (0, 0), pipeline_mode=pl.Buffered(1))
    return pl.pallas_call(
        _mlp_kernel,
        grid=(b, n // tm),
        in_specs=[tok(d), tok(od.shape[2]), tok(orr.shape[2]),
                  pl.BlockSpec(mod.shape, lambda bi, t: (0, 0)), gain, gain,
                  resident(wo), resident(w1), resident(w2)],
        out_specs=tok(d),
        out_shape=jax.ShapeDtypeStruct((b, n, d), F32),
        compiler_params=_cparams(("arbitrary", "arbitrary")),
        name="out_mlp",
    )(x, od, orr, mod, n2, fg, wo, w1, w2)


def kernel(x, c, ctx, c_ctx, w_ada, b_ada, norm1_g, norm2_g, w_in, lambda_q1, lambda_k1, lambda_q2,
           lambda_k2, diff_subln_g, ret_decay_fwd, ret_decay_bwd, ret_gn_g, w_out, w_mlp1, w_mlp2, final_g):
    assert w_ada.shape[0] == 1, "single-layer block"
    b, n, d = x.shape

    qd, kd, vt, qr, kr, vr, gr, kd_c, vt_c, kr_c, vr_c, mod2 = _project(
        x, ctx, c, c_ctx, w_ada[0], b_ada, norm1_g, w_in[0], _rope_tables(n))

    od, orr = _token_mixers(lambda_q1, lambda_k1, lambda_q2, lambda_k2, diff_subln_g, qd, kd, vt, kd_c, vt_c,
                            ret_decay_fwd, ret_decay_bwd, ret_gn_g, qr, kr, vr, gr, kr_c, vr_c)

    return _out_mlp(x, od, orr, mod2, norm2_g, final_g[None],
                    w_out[0], w_mlp1[0], w_mlp2[0])
```
